```python
import jax, jax.numpy as jnp
from jax import lax
import numpy as np

D_MODEL = 1024
BATCH = 8
SEQ = 4096
DEPTH = 2

GRID_W = 64
CTX_LEN = 256
HEAD_DIM = 64
CONV_W = 512
CONV_K = 31
NA_HEADS = 8
NA_W = NA_HEADS * HEAD_DIM
NA_KH_MAX = 8
NA_KW = 16
NA_QCW = 16
NA_KCW = 32
NA_NCB = GRID_W // NA_QCW
GQA_Q_HEADS = 8
GQA_KV_HEADS = 2
GQA_Q_W = GQA_Q_HEADS * HEAD_DIM
GQA_KV_W = GQA_KV_HEADS * HEAD_DIM
Q_BLOCK = 128
N_BRANCH = 3
FFN_HIDDEN = ((8 * D_MODEL + 767) // 768) * 256
ROPE_THETA = 10000.0
EPS = 1e-6
NEG = -1e30
IN_SPLITS = (2 * CONV_W, NA_W, NA_W, NA_W, GQA_Q_W, GQA_KV_W, GQA_KV_W, N_BRANCH * D_MODEL)
IN_WIDTH = sum(IN_SPLITS)

kernel_name = "hybrid_conv_natten_gqa_diffusion_trunk"


def rmsnorm(x, g):
    xf = x.astype(jnp.float32)
    y = xf * lax.rsqrt(jnp.mean(xf * xf, axis=-1, keepdims=True) + EPS)
    return (y * g.astype(jnp.float32)).astype(x.dtype)


def layernorm(x, g, b):
    xf = x.astype(jnp.float32)
    mu = jnp.mean(xf, axis=-1, keepdims=True)
    var = jnp.mean(jnp.square(xf - mu), axis=-1, keepdims=True)
    y = (xf - mu) * lax.rsqrt(var + EPS)
    return (y * g.astype(jnp.float32) + b.astype(jnp.float32)).astype(x.dtype)


def modulate(h, shift, scale):
    return h * (1 + scale) + shift


def split_in(p):
    return jnp.split(p, np.cumsum(IN_SPLITS)[:-1].tolist(), axis=-1)


def heads(t, n):
    return t.reshape(t.shape[:-1] + (n, HEAD_DIM))


def axial_rope(x, pos_row, pos_col):
    half = x.shape[-1] // 2
    freqs = jnp.power(ROPE_THETA, -jnp.arange(0, half, 2, dtype=jnp.float32) / half)

    def rot(xa, pos):
        ang = pos[:, None] * freqs
        cos = jnp.cos(ang)[None, :, None, :].astype(x.dtype)
        sin = jnp.sin(ang)[None, :, None, :].astype(x.dtype)
        x1, x2 = jnp.split(xa, 2, axis=-1)
        return jnp.concatenate([x1 * cos - x2 * sin, x1 * sin + x2 * cos], axis=-1)

    return jnp.concatenate([rot(x[..., :half], pos_row), rot(x[..., half:], pos_col)], axis=-1)


def conv_branch(a, conv_w, conv_b, ln_g, ln_b, w_out):
    u = a[..., :CONV_W] * jax.nn.sigmoid(a[..., CONV_W:])
    y = lax.conv_general_dilated(
        u, conv_w[:, None, :], window_strides=(1,), padding=[(CONV_K // 2, CONV_K // 2)],
        dimension_numbers=("NWC", "WIO", "NWC"), feature_group_count=CONV_W) + conv_b
    y = jax.nn.silu(layernorm(y, ln_g, ln_b))
    return y @ w_out


def dense_attention(q, k, v):
    B, T, Hq, dh = q.shape
    Hkv = k.shape[2]
    qg = q.reshape(B, T, Hkv, Hq // Hkv, dh) * (dh ** -0.5)
    s = jnp.einsum("bqhgd,bkhd->bhgqk", qg, k).astype(jnp.float32)
    p = jax.nn.softmax(s, axis=-1).astype(v.dtype)
    return jnp.einsum("bhgqk,bkhd->bqhgd", p, v).reshape(B, T, Hq * dh)


def gqa_latent(q, k, v, kc, vc):
    B, S, Hq, dh = q.shape
    Hkv = k.shape[2]
    G = Hq // Hkv
    k_all = jnp.concatenate([k, kc], axis=1)
    v_all = jnp.concatenate([v, vc], axis=1)
    nblk = S // Q_BLOCK
    qb = jnp.moveaxis((q * (dh ** -0.5)).reshape(B, nblk, Q_BLOCK, Hkv, G, dh), 1, 0)

    def one_block(q_blk):
        s = jnp.einsum("bqhgd,bkhd->bhgqk", q_blk, k_all).astype(jnp.float32)
        p = jax.nn.softmax(s, axis=-1).astype(v.dtype)
        return jnp.einsum("bhgqk,bkhd->bqhgd", p, v_all)

    o = lax.map(one_block, qb)
    return jnp.moveaxis(o, 0, 1).reshape(B, S, Hq * dh)


def na_latent(q, k, v, kc, vc, rpb):
    B, S, H, dh = q.shape
    rows = S // GRID_W
    kh = min(NA_KH_MAX, rows)
    q_cols = np.arange(GRID_W).reshape(NA_NCB, NA_QCW)
    key_cols = (np.clip(np.arange(NA_NCB) * NA_QCW - NA_KW // 2, 0, GRID_W - NA_KCW)[:, None]
                + np.arange(NA_KCW))
    win0 = np.clip(q_cols - NA_KW // 2, 0, GRID_W - NA_KW)[..., None]
    kcols = key_cols[:, None, :]
    col_mask = jnp.asarray((kcols >= win0) & (kcols < win0 + NA_KW))
    col_bias_idx = np.clip(kcols - q_cols[..., None] + NA_KW - 1, 0, 2 * NA_KW - 2)
    qg = (q * (dh ** -0.5)).reshape(B, rows, NA_NCB, NA_QCW, H, dh)
    kg = k.reshape(B, rows, GRID_W, H, dh)
    vg = v.reshape(B, rows, GRID_W, H, dh)
    n_loc = kh * NA_KCW

    def one_row(r):
        r0 = jnp.clip(r - kh // 2, 0, rows - kh)
        kb = lax.dynamic_slice_in_dim(kg, r0, kh, axis=1)[:, :, key_cols]
        vb = lax.dynamic_slice_in_dim(vg, r0, kh, axis=1)[:, :, key_cols]
        qr = lax.dynamic_index_in_dim(qg, r, axis=1, keepdims=False)
        drow = r0 + jnp.arange(kh) - r + NA_KH_MAX - 1
        bias = rpb[:, drow][:, :, col_bias_idx].transpose(0, 2, 3, 1, 4)
        s_loc = jnp.einsum("bnqhd,bknchd->bhnqkc", qr, kb).astype(jnp.float32) + bias.astype(jnp.float32)
        s_loc = jnp.where(col_mask[:, :, None, :], s_loc, NEG).reshape(B, H, NA_NCB, NA_QCW, n_loc)
        s_ctx = jnp.einsum("bnqhd,bthd->bhnqt", qr, kc).astype(jnp.float32)
        p = jax.nn.softmax(jnp.concatenate([s_loc, s_ctx], axis=-1), axis=-1).astype(v.dtype)
        p_loc = p[..., :n_loc].reshape(B, H, NA_NCB, NA_QCW, kh, NA_KCW)
        p_ctx = p[..., n_loc:]
        return (jnp.einsum("bhnqkc,bknchd->bnqhd", p_loc, vb)
                + jnp.einsum("bhnqt,bthd->bnqhd", p_ctx, vc))

    o = lax.map(one_row, jnp.arange(rows))
    return jnp.moveaxis(o, 0, 1).reshape(B, S, H * dh)


def merge(br_a, br_b, br_c, gates, w_out):
    ga, gb, gc = jnp.split(gates, N_BRANCH, axis=-1)
    y = jax.nn.sigmoid(ga) * br_a + jax.nn.sigmoid(gb) * br_b + jax.nn.sigmoid(gc) * br_c
    return y @ w_out


def swiglu(h, w_in, w_out):
    gate, up = jnp.split(h @ w_in, 2, axis=-1)
    return (jax.nn.silu(gate) * up) @ w_out


def setup_inputs(seed: int = 0) -> dict:
    key = jax.random.key(seed)
    ks = jax.random.split(key, 23)
    L, D = DEPTH, D_MODEL

    def nrm(k, shape, s):
        return jax.random.normal(k, shape, jnp.float32) * s

    return {
        "x": nrm(ks[0], (BATCH, SEQ, D), 1.0),
        "c": nrm(ks[1], (BATCH, D), 1.0),
        "ctx": nrm(ks[2], (BATCH, CTX_LEN, D), 1.0),
        "c_ctx": nrm(ks[3], (D,), 1.0),
        "w_mod": nrm(ks[4], (L, D, 6 * D), 0.5 * D ** -0.5),
        "b_mod": nrm(ks[5], (L, 6 * D), 0.01),
        "norm1_g": 1.0 + nrm(ks[6], (L, D), 0.05),
        "norm2_g": 1.0 + nrm(ks[7], (L, D), 0.05),
        "w_in": nrm(ks[8], (L, D, IN_WIDTH), D ** -0.5),
        "conv_w": nrm(ks[9], (L, CONV_K, CONV_W), CONV_K ** -0.5),
        "conv_b": nrm(ks[10], (L, CONV_W), 0.01),
        "conv_ln_g": 1.0 + nrm(ks[11], (L, CONV_W), 0.05),
        "conv_ln_b": nrm(ks[12], (L, CONV_W), 0.01),
        "w_conv_out": nrm(ks[13], (L, CONV_W, D), CONV_W ** -0.5),
        "na_rpb": nrm(ks[14], (L, NA_HEADS, 2 * NA_KH_MAX - 1, 2 * NA_KW - 1), 0.1),
        "w_na_out": nrm(ks[15], (L, NA_W, D), NA_W ** -0.5),
        "q_norm_g": 1.0 + nrm(ks[16], (L, HEAD_DIM), 0.05),
        "k_norm_g": 1.0 + nrm(ks[17], (L, HEAD_DIM), 0.05),
        "w_gqa_out": nrm(ks[18], (L, GQA_Q_W, D), GQA_Q_W ** -0.5),
        "w_out": nrm(ks[19], (L, D, D), D ** -0.5),
        "w_ffn_in": nrm(ks[20], (L, D, 2 * FFN_HIDDEN), D ** -0.5),
        "w_ffn_out": nrm(ks[21], (L, FFN_HIDDEN, D), FFN_HIDDEN ** -0.5),
        "final_g": 1.0 + nrm(ks[22], (D,), 0.05),
    }


def reference(x, c, ctx, c_ctx, w_mod, b_mod, norm1_g, norm2_g, w_in, conv_w, conv_b, conv_ln_g,
              conv_ln_b, w_conv_out, na_rpb, w_na_out, q_norm_g, k_norm_g, w_gqa_out, w_out,
              w_ffn_in, w_ffn_out, final_g):
    S = x.shape[1]
    t = jnp.arange(S)
    pos_row = (t // GRID_W).astype(jnp.float32)
    pos_col = (t % GRID_W).astype(jnp.float32)
    silu_c = jax.nn.silu(c)
    silu_cc = jax.nn.silu(c_ctx)
    xc = ctx
    for l in range(DEPTH):
        last = l == DEPTH - 1
        sh1, sc1, g1, sh2, sc2, g2 = jnp.split((silu_c @ w_mod[l] + b_mod[l])[:, None, :], 6, axis=-1)
        csh1, csc1, cg1, csh2, csc2, cg2 = jnp.split(silu_cc @ w_mod[l] + b_mod[l], 6, axis=-1)

        h = modulate(rmsnorm(x, norm1_g[l]), sh1, sc1)
        hc = modulate(rmsnorm(xc, norm1_g[l]), csh1, csc1)
        a, naq, nak, nav, gq, gk, gv, gates = split_in(h @ w_in[l])
        ac, naqc, nakc, navc, gqc, gkc, gvc, gatesc = split_in(hc @ w_in[l])

        na_kc, na_vc = heads(nakc, NA_HEADS), heads(navc, NA_HEADS)
        g_kc = rmsnorm(heads(gkc, GQA_KV_HEADS), k_norm_g[l])
        g_vc = heads(gvc, GQA_KV_HEADS)

        br_a = conv_branch(a, conv_w[l], conv_b[l], conv_ln_g[l], conv_ln_b[l], w_conv_out[l])
        br_b = na_latent(heads(naq, NA_HEADS), heads(nak, NA_HEADS), heads(nav, NA_HEADS),
                         na_kc, na_vc, na_rpb[l]) @ w_na_out[l]
        q = axial_rope(rmsnorm(heads(gq, GQA_Q_HEADS), q_norm_g[l]), pos_row, pos_col)
        k = axial_rope(rmsnorm(heads(gk, GQA_KV_HEADS), k_norm_g[l]), pos_row, pos_col)
        br_c = gqa_latent(q, k, heads(gv, GQA_KV_HEADS), g_kc, g_vc) @ w_gqa_out[l]
        x = x + g1 * merge(br_a, br_b, br_c, gates, w_out[l])
        x = x + g2 * swiglu(modulate(rmsnorm(x, norm2_g[l]), sh2, sc2), w_ffn_in[l], w_ffn_out[l])

        if not last:
            cbr_a = conv_branch(ac, conv_w[l], conv_b[l], conv_ln_g[l], conv_ln_b[l], w_conv_out[l])
            cbr_b = dense_attention(heads(naqc, NA_HEADS), na_kc, na_vc) @ w_na_out[l]
            cq = rmsnorm(heads(gqc, GQA_Q_HEADS), q_norm_g[l])
            cbr_c = dense_attention(cq, g_kc, g_vc) @ w_gqa_out[l]
            xc = xc + cg1 * merge(cbr_a, cbr_b, cbr_c, gatesc, w_out[l])
            xc = xc + cg2 * swiglu(modulate(rmsnorm(xc, norm2_g[l]), csh2, csc2), w_ffn_in[l], w_ffn_out[l])
    return rmsnorm(x, final_g)
```

```python
import functools

import numpy as np
import jax
import jax.numpy as jnp
from jax import lax
from jax.experimental import pallas as pl
from jax.experimental.pallas import tpu as pltpu

F32 = jnp.float32
BF16 = jnp.bfloat16

GRID_W = 64
HEAD_DIM = 64
CONV_W = 512
CONV_K = 31
CONV_HALO = 16
NA_HEADS = 8
NA_W = NA_HEADS * HEAD_DIM
NA_KH = 8
NA_KW = 16
NA_QROWS = 8
NA_WROWS = 16
GQA_Q_HEADS = 8
GQA_KV_HEADS = 2
GQA_Q_W = GQA_Q_HEADS * HEAD_DIM
GQA_KV_W = GQA_KV_HEADS * HEAD_DIM
ROPE_THETA = 10000.0
EPS = 1e-6
NEG = -1e30
SCALE = HEAD_DIM ** -0.5
LANES = 128
KEY_CHUNK = 256
VMEM_LIMIT = 56 * 1024 * 1024

_OFF_GLU = 0
_OFF_NAQ = _OFF_GLU + 2 * CONV_W
_OFF_NAK = _OFF_NAQ + NA_W
_OFF_NAV = _OFF_NAK + NA_W
_OFF_GQ = _OFF_NAV + NA_W
_OFF_GK = _OFF_GQ + GQA_Q_W
_OFF_GV = _OFF_GK + GQA_KV_W
_OFF_GATES = _OFF_GV + GQA_KV_W


def _params(n_axes):
    return pltpu.CompilerParams(dimension_semantics=("arbitrary",) * n_axes, vmem_limit_bytes=VMEM_LIMIT)


def _const_spec(shape):
    zeros = (0,) * len(shape)
    return pl.BlockSpec(shape, lambda *_: zeros, pipeline_mode=pl.Buffered(1))


def _norm_modulate(x, g, shift, scale):
    ms = jnp.mean(x * x, axis=-1, keepdims=True)
    h = x * lax.rsqrt(ms + EPS) * g
    return h * (1.0 + scale) + shift


def _mod_kernel(c_ref, w_ref, b_ref, o_ref):
    c = c_ref[...]
    s = c * jax.nn.sigmoid(c)
    o_ref[0] = jnp.dot(s, w_ref[0], preferred_element_type=F32, precision=lax.Precision.HIGHEST) + b_ref[0]


def _modulation(crows, w_mod, b_mod):
    n_layers, d, d6 = w_mod.shape
    r = crows.shape[0]
    nb = d6 // d
    return pl.pallas_call(
        _mod_kernel,
        grid=(n_layers, nb),
        in_specs=[
            pl.BlockSpec((r, d), lambda l, j: (0, 0)),
            pl.BlockSpec((1, d, d), lambda l, j: (l, 0, j)),
            pl.BlockSpec((1, 1, d), lambda l, j: (l, 0, j)),
        ],
        out_specs=pl.BlockSpec((1, r, d), lambda l, j: (l, 0, j)),
        out_shape=jax.ShapeDtypeStruct((n_layers, r, d6), F32),
        compiler_params=_params(2),
        name="modulation",
    )(crows, w_mod, b_mod.reshape(n_layers, 1, d6))


def _head_rms(t, bd):
    ms = jnp.dot((t * t).astype(BF16), bd, preferred_element_type=F32)
    return t * lax.rsqrt(ms + EPS)


def _rope(t, cos, sin_signed, first_half):
    fwd = pltpu.roll(t, LANES - 16, axis=1)
    bwd = pltpu.roll(t, 16, axis=1)
    return t * cos + jnp.where(first_half, fwd, bwd) * sin_signed


def _inproj_kernel(x_ref, sh_ref, sc_ref, g_ref, w_ref, qg_ref, kg_ref, bd_ref, cos_ref, sin_ref,
                   u_ref, naq_ref, nak_ref, nav_ref, gq_ref, gk_ref, gv_ref, *, rope):
    h = _norm_modulate(x_ref[...], g_ref[...], sh_ref[0], sc_ref[0]).astype(BF16)

    def proj(lo, hi):
        return jnp.dot(h, w_ref[:, lo:hi], preferred_element_type=F32)

    a = proj(_OFF_GLU, _OFF_NAQ)
    u_ref[...] = (a[:, :CONV_W] * jax.nn.sigmoid(a[:, CONV_W:])).astype(BF16)
    naq_ref[...] = (proj(_OFF_NAQ, _OFF_NAK) * SCALE).astype(BF16)
    nak_ref[...] = proj(_OFF_NAK, _OFF_NAV).astype(BF16)
    nav_ref[...] = proj(_OFF_NAV, _OFF_GQ).astype(BF16)
    gv_ref[...] = proj(_OFF_GV, _OFF_GATES).astype(BF16)

    bd = bd_ref[...]
    if rope:
        cos = cos_ref[...]
        sin = sin_ref[...]
        lane = lax.broadcasted_iota(jnp.int32, cos.shape, 1)
        first_half = (lane % 32) < 16

    def finish(t, gain):
        t = _head_rms(t, bd) * gain
        if rope:
            t = _rope(t, cos, sin, first_half)
        return t

    q = proj(_OFF_GQ, _OFF_GK)
    for j in range(GQA_Q_W // LANES):
        qj = finish(q[:, j * LANES:(j + 1) * LANES], qg_ref[...])
        gq_ref[:, j * LANES:(j + 1) * LANES] = (qj * SCALE).astype(BF16)
    gk_ref[...] = finish(proj(_OFF_GK, _OFF_GV), kg_ref[...]).astype(BF16)


def _inproj(x2, mod3, norm_g, w, qg2, kg2, bd, cos_t, sin_t, *, n_batch, seq, tm, rope):
    nt, d = x2.shape
    tps = seq // tm
    tok = lambda s, b: (b * tps + s, 0)
    outs = [CONV_W, NA_W, NA_W, NA_W, GQA_Q_W, GQA_KV_W, GQA_KV_W]
    return pl.pallas_call(
        functools.partial(_inproj_kernel, rope=rope),
        grid=(tps, n_batch),
        in_specs=[
            pl.BlockSpec((tm, d), tok),
            pl.BlockSpec((1, 1, d), lambda s, b: (b * 6 + 0, 0, 0)),
            pl.BlockSpec((1, 1, d), lambda s, b: (b * 6 + 1, 0, 0)),
            _const_spec((1, d)),
            _const_spec(w.shape),
            _const_spec((1, LANES)),
            _const_spec((1, LANES)),
            _const_spec((LANES, LANES)),
            pl.BlockSpec((tm, LANES), lambda s, b: (s, 0)),
            pl.BlockSpec((tm, LANES), lambda s, b: (s, 0)),
        ],
        out_specs=[pl.BlockSpec((tm, n), tok) for n in outs],
        out_shape=[jax.ShapeDtypeStruct((nt, n), BF16) for n in outs],
        compiler_params=_params(2),
        name="inproj_rope" if rope else "inproj_ctx",
    )(x2, mod3, mod3, norm_g, w, qg2, kg2, bd, cos_t, sin_t)


def _conv_kernel(prev_ref, cur_ref, next_ref, w_ref, b_ref, lg_ref, lb_ref, o_ref, win_ref, *, tiles_per_seq):
    i = pl.program_id(1)
    tl = cur_ref.shape[0]
    prev = jnp.where(i == 0, 0.0, prev_ref[...].astype(F32))
    nxt = jnp.where(i == tiles_per_seq - 1, 0.0, next_ref[...].astype(F32))
    win_ref[0:CONV_HALO, :] = prev
    win_ref[CONV_HALO:CONV_HALO + tl, :] = cur_ref[...].astype(F32)
    win_ref[CONV_HALO + tl:, :] = nxt
    base = CONV_HALO - CONV_K // 2
    acc = jnp.zeros((tl, CONV_W), F32)
    for k in range(CONV_K):
        acc = acc + win_ref[base + k:base + k + tl, :] * w_ref[k:k + 1, :]
    y = acc + b_ref[...]
    mu = jnp.mean(y, axis=-1, keepdims=True)
    yc = y - mu
    var = jnp.mean(yc * yc, axis=-1, keepdims=True)
    z = yc * lax.rsqrt(var + EPS) * lg_ref[...] + lb_ref[...]
    o_ref[...] = (z * jax.nn.sigmoid(z)).astype(BF16)


def _conv(u, conv_w, conv_b, ln_g, ln_b, *, n_batch, seq, tl):
    nt = u.shape[0]
    tps = seq // tl
    hb = tl // CONV_HALO
    last_hb = nt // CONV_HALO - 1
    return pl.pallas_call(
        functools.partial(_conv_kernel, tiles_per_seq=tps),
        grid=(n_batch, tps),
        in_specs=[
            pl.BlockSpec((CONV_HALO, CONV_W), lambda b, i: (jnp.maximum((b * tps + i) * hb - 1, 0), 0)),
            pl.BlockSpec((tl, CONV_W), lambda b, i: (b * tps + i, 0)),
            pl.BlockSpec((CONV_HALO, CONV_W), lambda b, i: (jnp.minimum((b * tps + i + 1) * hb, last_hb), 0)),
            _const_spec((CONV_K, CONV_W)),
            _const_spec((1, CONV_W)),
            _const_spec((1, CONV_W)),
            _const_spec((1, CONV_W)),
        ],
        out_specs=pl.BlockSpec((tl, CONV_W), lambda b, i: (b * tps + i, 0)),
        out_shape=jax.ShapeDtypeStruct((nt, CONV_W), BF16),
        scratch_shapes=[pltpu.VMEM((tl + 2 * CONV_HALO, CONV_W), F32)],
        compiler_params=_params(2),
        name="conv_ln_swish",
    )(u, u, u, conv_w, conv_b, ln_g, ln_b)


def _na_bias_tables(rpb, rows):
    h = rpb.shape[0]
    cols = np.arange(GRID_W)
    win0 = np.clip(cols - NA_KW // 2, 0, GRID_W - NA_KW)
    col_ok = (cols[None, :] >= win0[:, None]) & (cols[None, :] < win0[:, None] + NA_KW)
    dc = np.clip(cols[None, :] - cols[:, None] + NA_KW - 1, 0, 2 * NA_KW - 2)
    tiles = jnp.where(jnp.asarray(col_ok)[None, None], rpb[:, :, dc], NEG)
    tiles = jnp.concatenate([tiles, jnp.full((h, 1, GRID_W, GRID_W), NEG, rpb.dtype)], axis=1)
    n_groups = rows // NA_QROWS
    tile_idx = np.zeros((3, NA_QROWS, NA_WROWS), np.int32)
    for v, g in enumerate((0, min(1, n_groups - 1), n_groups - 1)):
        w = int(np.clip(NA_QROWS * g - NA_KH // 2, 0, rows - NA_WROWS))
        for ql in range(NA_QROWS):
            r = NA_QROWS * g + ql
            r0 = int(np.clip(r - NA_KH // 2, 0, rows - NA_KH))
            for kl in range(NA_WROWS):
                kr = w + kl
                tile_idx[v, ql, kl] = kr - r + NA_KH - 1 if r0 <= kr < r0 + NA_KH else 2 * NA_KH - 1
    t = jnp.take(tiles, jnp.asarray(tile_idx.reshape(-1)), axis=1)
    t = t.reshape(h, 3, NA_QROWS, NA_WROWS, GRID_W, GRID_W).transpose(1, 0, 2, 4, 3, 5)
    t = t.reshape(3, h, NA_QROWS * GRID_W, NA_WROWS * GRID_W)
    return t.astype(BF16)


def _na_kernel(q_ref, k_ref, v_ref, kc_ref, vc_ref, bias_ref, o_ref, *, rows):
    g = pl.program_id(2)
    nq = NA_QROWS * GRID_W
    nk = NA_WROWS * GRID_W
    start = jnp.clip(NA_QROWS * g - NA_KH // 2, 0, rows - NA_WROWS) * GRID_W
    start = pl.multiple_of(start, 256)
    kw = k_ref[pl.ds(start, nk), :]
    vw = v_ref[pl.ds(start, nk), :]
    kc = kc_ref[...]
    vall = jnp.concatenate([vw, vc_ref[...]], axis=0)
    lane = lax.broadcasted_iota(jnp.int32, vall.shape, 1)
    res = []
    inv = []
    for j in range(2):
        sl = slice(j * HEAD_DIM, (j + 1) * HEAD_DIM)
        qh = q_ref[:, sl]
        dn = (((1,), (1,)), ((), ()))
        s_loc = lax.dot_general(qh, kw[:, sl], dn, preferred_element_type=F32) + bias_ref[0, j].astype(F32)
        s_ctx = lax.dot_general(qh, kc[:, sl], dn, preferred_element_type=F32)
        m = jnp.maximum(jnp.max(s_loc, axis=-1, keepdims=True), jnp.max(s_ctx, axis=-1, keepdims=True))
        p = jnp.concatenate([jnp.exp(s_loc - m), jnp.exp(s_ctx - m)], axis=-1).astype(BF16)
        own = (lane >= j * HEAD_DIM) & (lane < (j + 1) * HEAD_DIM)
        pv = jnp.dot(p, jnp.where(own, vall, jnp.ones_like(vall)), preferred_element_type=F32)
        den = pv[:, (1 - j) * HEAD_DIM:(1 - j) * HEAD_DIM + 1]
        res.append(pv)
        inv.append(1.0 / den)
    lane_o = lax.broadcasted_iota(jnp.int32, (nq, LANES), 1)
    first = lane_o < HEAD_DIM
    o_ref[...] = (jnp.where(first, res[0], res[1]) * jnp.where(first, inv[0], inv[1])).astype(BF16)


def _na_attention(q, k, v, kc, vc, bias, *, n_batch, seq, ctx_len):
    rows = seq // GRID_W
    n_groups = rows // NA_QROWS
    nq = NA_QROWS * GRID_W
    n_pairs = NA_W // LANES
    variant = lambda g: jnp.where(g == 0, 0, jnp.where(g == n_groups - 1, 2, 1))
    return pl.pallas_call(
        functools.partial(_na_kernel, rows=rows),
        grid=(n_batch, n_pairs, n_groups),
        in_specs=[
            pl.BlockSpec((nq, LANES), lambda b, hp, g: (b * n_groups + g, hp)),
            pl.BlockSpec((seq, LANES), lambda b, hp, g: (b, hp)),
            pl.BlockSpec((seq, LANES), lambda b, hp, g: (b, hp)),
            pl.BlockSpec((ctx_len, LANES), lambda b, hp, g: (b, hp)),
            pl.BlockSpec((ctx_len, LANES), lambda b, hp, g: (b, hp)),
            pl.BlockSpec((1, 2, nq, NA_WROWS * GRID_W), lambda b, hp, g: (variant(g), hp, 0, 0)),
        ],
        out_specs=pl.BlockSpec((nq, LANES), lambda b, hp, g: (b * n_groups + g, hp)),
        out_shape=jax.ShapeDtypeStruct(q.shape, BF16),
        compiler_params=_params(3),
        name="na_attention",
    )(q, k, v, kc, vc, bias)


def _attn_kernel(q_ref, k_ref, v_ref, o_ref, *, n_kv, group, n_chunks):
    tq = q_ref.shape[0]
    dn = (((1,), (1,)), ((), ()))
    for hk in range(n_kv):
        blk = (hk * HEAD_DIM) // LANES * LANES
        off = hk * HEAD_DIM - blk
        qs = jnp.concatenate(
            [q_ref[:, (hk * group + gi) * HEAD_DIM:(hk * group + gi + 1) * HEAD_DIM] for gi in range(group)], axis=0)
        lane = lax.broadcasted_iota(jnp.int32, (KEY_CHUNK, LANES), 1)
        own = (lane >= off) & (lane < off + HEAD_DIM)

        def body(c, carry):
            m, acc = carry
            ks = pl.multiple_of(c * KEY_CHUNK, KEY_CHUNK)
            kch = k_ref[pl.ds(ks, KEY_CHUNK), blk + off:blk + off + HEAD_DIM]
            vch = v_ref[pl.ds(ks, KEY_CHUNK), blk:blk + LANES]
            s = lax.dot_general(qs, kch, dn, preferred_element_type=F32)
            m_new = jnp.maximum(m, jnp.max(s, axis=-1, keepdims=True))
            p = jnp.exp(s - m_new).astype(BF16)
            pv = jnp.dot(p, jnp.where(own, vch, jnp.ones_like(vch)), preferred_element_type=F32)
            return m_new, acc * jnp.exp(m - m_new) + pv

        m0 = jnp.full((group * tq, 1), NEG, F32)
        acc0 = jnp.zeros((group * tq, LANES), F32)
        _, acc = lax.fori_loop(0, n_chunks, body, (m0, acc0))
        den = acc[:, HEAD_DIM - off:HEAD_DIM - off + 1]
        out = acc[:, off:off + HEAD_DIM] * (1.0 / den)
        for gi in range(group):
            hq = hk * group + gi
            o_ref[:, hq * HEAD_DIM:(hq + 1) * HEAD_DIM] = out[gi * tq:(gi + 1) * tq].astype(BF16)


def _attention(q, k, v, *, n_batch, sq, sk, tq, n_kv, group):
    tpq = sq // tq
    kvw = n_kv * HEAD_DIM
    return pl.pallas_call(
        functools.partial(_attn_kernel, n_kv=n_kv, group=group, n_chunks=sk // KEY_CHUNK),
        grid=(n_batch, tpq),
        in_specs=[
            pl.BlockSpec((tq, q.shape[1]), lambda b, i: (b * tpq + i, 0)),
            pl.BlockSpec((sk, kvw), lambda b, i: (b, 0)),
            pl.BlockSpec((sk, kvw), lambda b, i: (b, 0)),
        ],
        out_specs=pl.BlockSpec((tq, q.shape[1]), lambda b, i: (b * tpq + i, 0)),
        out_shape=jax.ShapeDtypeStruct(q.shape, BF16),
        compiler_params=_params(2),
        name=f"attention_g{group}",
    )(q, k, v)


def _merge_kernel(x_ref, sh_ref, sc_ref, gate_ref, g_ref, wg_ref, a_ref, b_ref, c_ref, wa_ref, wb_ref, wc_ref,
                  wo_ref, o_ref):
    x = x_ref[...]
    d = x.shape[1]
    h = _norm_modulate(x, g_ref[...], sh_ref[0], sc_ref[0]).astype(BF16)
    y = None
    for j, (br_ref, w_ref) in enumerate(((a_ref, wa_ref), (b_ref, wb_ref), (c_ref, wc_ref))):
        gate = jax.nn.sigmoid(jnp.dot(h, wg_ref[:, j * d:(j + 1) * d], preferred_element_type=F32))
        t = gate * jnp.dot(br_ref[...], w_ref[...], preferred_element_type=F32)
        y = t if y is None else y + t
    o_ref[...] = x + gate_ref[0] * jnp.dot(y.astype(BF16), wo_ref[...], preferred_element_type=F32)


def _merge(x2, mod3, norm_g, w_gates, br_a, br_b, br_c, wa, wb, wc, wo, *, n_batch, seq, tm):
    nt, d = x2.shape
    tps = seq // tm
    tok = lambda i: (i, 0)
    mrow = lambda j: (lambda i: ((i // tps) * 6 + j, 0, 0))
    return pl.pallas_call(
        _merge_kernel,
        grid=(nt // tm,),
        in_specs=[
            pl.BlockSpec((tm, d), tok),
            pl.BlockSpec((1, 1, d), mrow(0)),
            pl.BlockSpec((1, 1, d), mrow(1)),
            pl.BlockSpec((1, 1, d), mrow(2)),
            _const_spec((1, d)),
            _const_spec(w_gates.shape),
            pl.BlockSpec((tm, CONV_W), tok),
            pl.BlockSpec((tm, NA_W), tok),
            pl.BlockSpec((tm, GQA_Q_W), tok),
            _const_spec(wa.shape),
            _const_spec(wb.shape),
            _const_spec(wc.shape),
            _const_spec(wo.shape),
        ],
        out_specs=pl.BlockSpec((tm, d), tok),
        out_shape=jax.ShapeDtypeStruct((nt, d), F32),
        compiler_params=_params(1),
        name="merge",
    )(x2, mod3, mod3, mod3, norm_g, w_gates, br_a, br_b, br_c, wa, wb, wc, wo)


def _ffn_kernel(x_ref, sh_ref, sc_ref, gate_ref, g_ref, wi_ref, wo_ref, fg_ref, o_ref, *, n_split, final_norm):
    x = x_ref[...]
    hidden = wo_ref.shape[0]
    h = _norm_modulate(x, g_ref[...], sh_ref[0], sc_ref[0]).astype(BF16)
    step = hidden // n_split
    y = None
    for j in range(n_split):
        gt = jnp.dot(h, wi_ref[:, j * step:(j + 1) * step], preferred_element_type=F32)
        up = jnp.dot(h, wi_ref[:, hidden + j * step:hidden + (j + 1) * step], preferred_element_type=F32)
        act = (gt * jax.nn.sigmoid(gt) * up).astype(BF16)
        t = jnp.dot(act, wo_ref[j * step:(j + 1) * step, :], preferred_element_type=F32)
        y = t if y is None else y + t
    out = x + gate_ref[0] * y
    if final_norm:
        ms = jnp.mean(out * out, axis=-1, keepdims=True)
        out = out * lax.rsqrt(ms + EPS) * fg_ref[...]
    o_ref[...] = out


def _ffn(x2, mod3, norm_g, w_in, w_out, final_g, *, n_batch, seq, tm, final_norm):
    nt, d = x2.shape
    tps = seq // tm
    tok = lambda i: (i, 0)
    mrow = lambda j: (lambda i: ((i // tps) * 6 + j, 0, 0))
    return pl.pallas_call(
        functools.partial(_ffn_kernel, n_split=2, final_norm=final_norm),
        grid=(nt // tm,),
        in_specs=[
            pl.BlockSpec((tm, d), tok),
            pl.BlockSpec((1, 1, d), mrow(3)),
            pl.BlockSpec((1, 1, d), mrow(4)),
            pl.BlockSpec((1, 1, d), mrow(5)),
            _const_spec((1, d)),
            _const_spec(w_in.shape),
            _const_spec(w_out.shape),
            _const_spec((1, d)),
        ],
        out_specs=pl.BlockSpec((tm, d), tok),
        out_shape=jax.ShapeDtypeStruct((nt, d), F32),
        compiler_params=_params(1),
        name="ffn_final" if final_norm else "ffn",
    )(x2, mod3, mod3, mod3, norm_g, w_in, w_out, final_g)


def _rope_tables(seq):
    t = jnp.arange(seq)
    pos_row = (t // GRID_W).astype(F32)
    pos_col = (t % GRID_W).astype(F32)
    half = HEAD_DIM // 2
    freqs = jnp.power(ROPE_THETA, -jnp.arange(0, half, 2, dtype=F32) / half)

    def part(pos):
        ang = pos[:, None] * freqs
        c, s = jnp.cos(ang), jnp.sin(ang)
        return jnp.concatenate([c, c], axis=-1), jnp.concatenate([-s, s], axis=-1)

    cr, sr = part(pos_row)
    cc, sc = part(pos_col)
    cos = jnp.concatenate([cr, cc], axis=-1)
    sin = jnp.concatenate([sr, sc], axis=-1)
    return jnp.tile(cos, (1, LANES // HEAD_DIM)), jnp.tile(sin, (1, LANES // HEAD_DIM))


def kernel(x, c, ctx, c_ctx, w_mod, b_mod, norm1_g, norm2_g, w_in, conv_w, conv_b, conv_ln_g, conv_ln_b,
           w_conv_out, na_rpb, w_na_out, q_norm_g, k_norm_g, w_gqa_out, w_out, w_ffn_in, w_ffn_out, final_g):
    n_batch, seq, d = x.shape
    ctx_len = ctx.shape[1]
    depth = w_mod.shape[0]
    rows = seq // GRID_W
    assert seq % (NA_QROWS * GRID_W) == 0 and rows >= NA_WROWS and ctx_len % KEY_CHUNK == 0
    tm = 512
    tmc = ctx_len

    n_rows = -(-(n_batch + 1) // 8) * 8
    crows = jnp.zeros((n_rows, d), F32).at[:n_batch].set(c).at[n_batch].set(c_ctx)
    mod = _modulation(crows, w_mod, b_mod)

    cos_t, sin_t = _rope_tables(seq)
    bd = jnp.asarray(np.kron(np.eye(LANES // HEAD_DIM), np.full((HEAD_DIM, HEAD_DIM), 1.0 / HEAD_DIM)), BF16)
    tile2 = lambda g: jnp.tile(g, LANES // HEAD_DIM).reshape(1, LANES)
    row = lambda g: g.reshape(1, -1)

    xl = x.reshape(n_batch * seq, d)
    xc = ctx.reshape(n_batch * ctx_len, d)
    for l in range(depth):
        last = l == depth - 1
        w_proj = w_in[l, :, :_OFF_GATES].astype(BF16)
        w_gates = w_in[l, :, _OFF_GATES:].astype(BF16)
        mod_l = mod[l, :n_batch].reshape(n_batch * 6, 1, d)
        mod_c = jnp.broadcast_to(mod[l, n_batch], (n_batch, 6 * d)).reshape(n_batch * 6, 1, d)
        qg2, kg2 = tile2(q_norm_g[l]), tile2(k_norm_g[l])
        g1, g2 = row(norm1_g[l]), row(norm2_g[l])
        wa, wb, wc = w_conv_out[l].astype(BF16), w_na_out[l].astype(BF16), w_gqa_out[l].astype(BF16)
        wo = w_out[l].astype(BF16)
        wfi, wfo = w_ffn_in[l].astype(BF16), w_ffn_out[l].astype(BF16)
        conv_args = (conv_w[l], row(conv_b[l]), row(conv_ln_g[l]), row(conv_ln_b[l]))

        u, naq, nak, nav, gq, gk, gv = _inproj(xl, mod_l, g1, w_proj, qg2, kg2, bd, cos_t, sin_t,
                                               n_batch=n_batch, seq=seq, tm=tm, rope=True)
        uc, naqc, nakc, navc, gqc, gkc, gvc = _inproj(xc, mod_c, g1, w_proj, qg2, kg2, bd,
                                                      cos_t[:ctx_len], sin_t[:ctx_len],
                                                      n_batch=n_batch, seq=ctx_len, tm=tmc, rope=False)

        br_a = _conv(u, *conv_args, n_batch=n_batch, seq=seq, tl=tm)
        bias = _na_bias_tables(na_rpb[l], rows)
        br_b = _na_attention(naq, nak, nav, nakc, navc, bias, n_batch=n_batch, seq=seq, ctx_len=ctx_len)
        cat = lambda a, b, w: jnp.concatenate(
            [a.reshape(n_batch, seq, w), b.reshape(n_batch, ctx_len, w)], axis=1).reshape(-1, w)
        br_c = _attention(gq, cat(gk, gkc, GQA_KV_W), cat(gv, gvc, GQA_KV_W), n_batch=n_batch, sq=seq,
                          sk=seq + ctx_len, tq=128, n_kv=GQA_KV_HEADS, group=GQA_Q_HEADS // GQA_KV_HEADS)
        xl = _merge(xl, mod_l, g1, w_gates, br_a, br_b, br_c, wa, wb, wc, wo, n_batch=n_batch, seq=seq, tm=tm)
        xl = _ffn(xl, mod_l, g2, wfi, wfo, row(final_g), n_batch=n_batch, seq=seq, tm=tm, final_norm=last)

        if not last:
            cbr_a = _conv(uc, *conv_args, n_batch=n_batch, seq=ctx_len, tl=tmc)
            cbr_b = _attention(naqc, nakc, navc, n_batch=n_batch, sq=ctx_len, sk=ctx_len, tq=ctx_len,
                               n_kv=NA_HEADS, group=1)
            cbr_c = _attention(gqc, gkc, gvc, n_batch=n_batch, sq=ctx_len, sk=ctx_len, tq=ctx_len,
                               n_kv=GQA_KV_HEADS, group=GQA_Q_HEADS // GQA_KV_HEADS)
            xc = _merge(xc, mod_c, g1, w_gates, cbr_a, cbr_b, cbr_c, wa, wb, wc, wo,
                        n_batch=n_batch, seq=ctx_len, tm=tmc)
            xc = _ffn(xc, mod_c, g2, wfi, wfo, row(final_g), n_batch=n_batch, seq=ctx_len, tm=tmc,
                      final_norm=False)
    return xl.reshape(n_batch, seq, d)
```

```python
import functools

import numpy as np
import jax
import jax.numpy as jnp
from jax import lax
from jax.experimental import pallas as pl
from jax.experimental.pallas import tpu as pltpu

F32 = jnp.float32
BF16 = jnp.bfloat16

GRID_W = 64
HEAD_DIM = 64
CONV_W = 512
CONV_K = 31
CONV_HALO = 16
NA_HEADS = 8
NA_W = NA_HEADS * HEAD_DIM
NA_KH = 8
NA_KW = 16
NA_QROWS = 8
NA_WROWS = 16
GQA_Q_HEADS = 8
GQA_KV_HEADS = 2
GQA_Q_W = GQA_Q_HEADS * HEAD_DIM
GQA_KV_W = GQA_KV_HEADS * HEAD_DIM
ROPE_THETA = 10000.0
EPS = 1e-6
NEG = -1e30
SCALE = HEAD_DIM ** -0.5
LOG2E = 1.4426950408889634
LANES = 128
KEY_CHUNK = 256
VT_ROWS = HEAD_DIM + 16
VMEM_LIMIT = 56 * 1024 * 1024

_OFF_GLU = 0
_OFF_NAQ = _OFF_GLU + 2 * CONV_W
_OFF_NAK = _OFF_NAQ + NA_W
_OFF_NAV = _OFF_NAK + NA_W
_OFF_GQ = _OFF_NAV + NA_W
_OFF_GK = _OFF_GQ + GQA_Q_W
_OFF_GV = _OFF_GK + GQA_KV_W
_OFF_GATES = _OFF_GV + GQA_KV_W


def _params(n_axes):
    return pltpu.CompilerParams(dimension_semantics=("arbitrary",) * n_axes, vmem_limit_bytes=VMEM_LIMIT)


def _const_spec(shape):
    zeros = (0,) * len(shape)
    return pl.BlockSpec(shape, lambda *_: zeros, pipeline_mode=pl.Buffered(1))


def _norm_modulate(x, g, shift, scale):
    ms = jnp.mean(x * x, axis=-1, keepdims=True)
    h = x * lax.rsqrt(ms + EPS) * g
    return h * (1.0 + scale) + shift


def _mod_kernel(c_ref, w_ref, b_ref, o_ref):
    c = c_ref[...]
    s = c * jax.nn.sigmoid(c)
    o_ref[0] = jnp.dot(s, w_ref[0], preferred_element_type=F32, precision=lax.Precision.HIGHEST) + b_ref[0]


def _modulation(crows, w_mod, b_mod):
    n_layers, d, d6 = w_mod.shape
    r = crows.shape[0]
    nb = d6 // d
    return pl.pallas_call(
        _mod_kernel,
        grid=(n_layers, nb),
        in_specs=[
            pl.BlockSpec((r, d), lambda l, j: (0, 0)),
            pl.BlockSpec((1, d, d), lambda l, j: (l, 0, j)),
            pl.BlockSpec((1, 1, d), lambda l, j: (l, 0, j)),
        ],
        out_specs=pl.BlockSpec((1, r, d), lambda l, j: (l, 0, j)),
        out_shape=jax.ShapeDtypeStruct((n_layers, r, d6), F32),
        compiler_params=_params(2),
        name="modulation",
    )(crows, w_mod, b_mod.reshape(n_layers, 1, d6))


def _head_rms(t, bd):
    ms = jnp.dot((t * t).astype(BF16), bd, preferred_element_type=F32)
    return t * lax.rsqrt(ms + EPS)


def _rope(t, cos, sin_signed, first_half):
    fwd = pltpu.roll(t, LANES - 16, axis=1)
    bwd = pltpu.roll(t, 16, axis=1)
    return t * cos + jnp.where(first_half, fwd, bwd) * sin_signed


def _inproj_kernel(x_ref, sh_ref, sc_ref, g_ref, w_ref, qg_ref, kg_ref, bd_ref, cos_ref, sin_ref,
                   u_ref, naq_ref, nak_ref, nav_ref, gq_ref, gk_ref, gv_ref, *, rope):
    h = _norm_modulate(x_ref[...], g_ref[...], sh_ref[0], sc_ref[0]).astype(BF16)

    def proj(lo, hi):
        return jnp.dot(h, w_ref[:, lo:hi], preferred_element_type=F32)

    a = proj(_OFF_GLU, _OFF_NAQ)
    u_ref[...] = (a[:, :CONV_W] * jax.nn.sigmoid(a[:, CONV_W:])).astype(BF16)
    naq_ref[...] = (proj(_OFF_NAQ, _OFF_NAK) * SCALE).astype(BF16)
    nak_ref[...] = proj(_OFF_NAK, _OFF_NAV).astype(BF16)
    nav_ref[...] = proj(_OFF_NAV, _OFF_GQ).astype(BF16)
    gv_ref[...] = proj(_OFF_GV, _OFF_GATES).astype(BF16)

    bd = bd_ref[...]
    q_scale = SCALE * LOG2E if rope else SCALE
    if rope:
        cos = cos_ref[...]
        sin = sin_ref[...]
        lane = lax.broadcasted_iota(jnp.int32, cos.shape, 1)
        first_half = (lane % 32) < 16

    def finish(t, gain):
        t = _head_rms(t, bd) * gain
        if rope:
            t = _rope(t, cos, sin, first_half)
        return t

    q = proj(_OFF_GQ, _OFF_GK)
    for j in range(GQA_Q_W // LANES):
        qj = finish(q[:, j * LANES:(j + 1) * LANES], qg_ref[...])
        gq_ref[:, j * LANES:(j + 1) * LANES] = (qj * q_scale).astype(BF16)
    gk_ref[...] = finish(proj(_OFF_GK, _OFF_GV), kg_ref[...]).astype(BF16)


def _inproj(x2, mod3, norm_g, w, qg2, kg2, bd, cos_t, sin_t, *, n_batch, seq, tm, rope):
    nt, d = x2.shape
    tps = seq // tm
    tok = lambda s, b: (b * tps + s, 0)
    outs = [CONV_W, NA_W, NA_W, NA_W, GQA_Q_W, GQA_KV_W, GQA_KV_W]
    return pl.pallas_call(
        functools.partial(_inproj_kernel, rope=rope),
        grid=(tps, n_batch),
        in_specs=[
            pl.BlockSpec((tm, d), tok),
            pl.BlockSpec((1, 1, d), lambda s, b: (b * 6 + 0, 0, 0)),
            pl.BlockSpec((1, 1, d), lambda s, b: (b * 6 + 1, 0, 0)),
            _const_spec((1, d)),
            _const_spec(w.shape),
            _const_spec((1, LANES)),
            _const_spec((1, LANES)),
            _const_spec((LANES, LANES)),
            pl.BlockSpec((tm, LANES), lambda s, b: (s, 0)),
            pl.BlockSpec((tm, LANES), lambda s, b: (s, 0)),
        ],
        out_specs=[pl.BlockSpec((tm, n), tok) for n in outs],
        out_shape=[jax.ShapeDtypeStruct((nt, n), BF16) for n in outs],
        compiler_params=_params(2),
        name="inproj_rope" if rope else "inproj_ctx",
    )(x2, mod3, mod3, norm_g, w, qg2, kg2, bd, cos_t, sin_t)


def _conv_kernel(prev_ref, cur_ref, next_ref, w_ref, b_ref, lg_ref, lb_ref, o_ref, win_ref, *, tiles_per_seq):
    i = pl.program_id(1)
    tl = cur_ref.shape[0]
    prev = jnp.where(i == 0, 0.0, prev_ref[...].astype(F32))
    nxt = jnp.where(i == tiles_per_seq - 1, 0.0, next_ref[...].astype(F32))
    win_ref[0:CONV_HALO, :] = prev
    win_ref[CONV_HALO:CONV_HALO + tl, :] = cur_ref[...].astype(F32)
    win_ref[CONV_HALO + tl:, :] = nxt
    base = CONV_HALO - CONV_K // 2
    acc = jnp.zeros((tl, CONV_W), F32)
    for k in range(CONV_K):
        acc = acc + win_ref[base + k:base + k + tl, :] * w_ref[k:k + 1, :]
    y = acc + b_ref[...]
    mu = jnp.mean(y, axis=-1, keepdims=True)
    yc = y - mu
    var = jnp.mean(yc * yc, axis=-1, keepdims=True)
    z = yc * lax.rsqrt(var + EPS) * lg_ref[...] + lb_ref[...]
    o_ref[...] = (z * jax.nn.sigmoid(z)).astype(BF16)


def _conv(u, conv_w, conv_b, ln_g, ln_b, *, n_batch, seq, tl):
    nt = u.shape[0]
    tps = seq // tl
    hb = tl // CONV_HALO
    last_hb = nt // CONV_HALO - 1
    return pl.pallas_call(
        functools.partial(_conv_kernel, tiles_per_seq=tps),
        grid=(n_batch, tps),
        in_specs=[
            pl.BlockSpec((CONV_HALO, CONV_W), lambda b, i: (jnp.maximum((b * tps + i) * hb - 1, 0), 0)),
            pl.BlockSpec((tl, CONV_W), lambda b, i: (b * tps + i, 0)),
            pl.BlockSpec((CONV_HALO, CONV_W), lambda b, i: (jnp.minimum((b * tps + i + 1) * hb, last_hb), 0)),
            _const_spec((CONV_K, CONV_W)),
            _const_spec((1, CONV_W)),
            _const_spec((1, CONV_W)),
            _const_spec((1, CONV_W)),
        ],
        out_specs=pl.BlockSpec((tl, CONV_W), lambda b, i: (b * tps + i, 0)),
        out_shape=jax.ShapeDtypeStruct((nt, CONV_W), BF16),
        scratch_shapes=[pltpu.VMEM((tl + 2 * CONV_HALO, CONV_W), F32)],
        compiler_params=_params(2),
        name="conv_ln_swish",
    )(u, u, u, conv_w, conv_b, ln_g, ln_b)


def _na_bias_tables(rpb, rows):
    h = rpb.shape[0]
    cols = np.arange(GRID_W)
    win0 = np.clip(cols - NA_KW // 2, 0, GRID_W - NA_KW)
    col_ok = (cols[None, :] >= win0[:, None]) & (cols[None, :] < win0[:, None] + NA_KW)
    dc = np.clip(cols[None, :] - cols[:, None] + NA_KW - 1, 0, 2 * NA_KW - 2)
    tiles = jnp.where(jnp.asarray(col_ok)[None, None], rpb[:, :, dc], NEG)
    tiles = jnp.concatenate([tiles, jnp.full((h, 1, GRID_W, GRID_W), NEG, rpb.dtype)], axis=1)
    n_groups = rows // NA_QROWS
    tile_idx = np.zeros((3, NA_QROWS, NA_WROWS), np.int32)
    for v, g in enumerate((0, min(1, n_groups - 1), n_groups - 1)):
        w = int(np.clip(NA_QROWS * g - NA_KH // 2, 0, rows - NA_WROWS))
        for ql in range(NA_QROWS):
            r = NA_QROWS * g + ql
            r0 = int(np.clip(r - NA_KH // 2, 0, rows - NA_KH))
            for kl in range(NA_WROWS):
                kr = w + kl
                tile_idx[v, ql, kl] = kr - r + NA_KH - 1 if r0 <= kr < r0 + NA_KH else 2 * NA_KH - 1
    t = jnp.take(tiles, jnp.asarray(tile_idx.reshape(-1)), axis=1)
    t = t.reshape(h, 3, NA_QROWS, NA_WROWS, GRID_W, GRID_W).transpose(1, 0, 2, 4, 3, 5)
    t = t.reshape(3, h, NA_QROWS * GRID_W, NA_WROWS * GRID_W)
    return t.astype(BF16)


def _na_kernel(q_ref, k_ref, v_ref, kc_ref, vc_ref, bias_ref, o_ref, *, rows):
    g = pl.program_id(2)
    nq = NA_QROWS * GRID_W
    nk = NA_WROWS * GRID_W
    start = jnp.clip(NA_QROWS * g - NA_KH // 2, 0, rows - NA_WROWS) * GRID_W
    start = pl.multiple_of(start, 256)
    kw = k_ref[pl.ds(start, nk), :]
    vw = v_ref[pl.ds(start, nk), :]
    kc = kc_ref[...]
    vall = jnp.concatenate([vw, vc_ref[...]], axis=0)
    lane = lax.broadcasted_iota(jnp.int32, vall.shape, 1)
    res = []
    inv = []
    for j in range(2):
        sl = slice(j * HEAD_DIM, (j + 1) * HEAD_DIM)
        qh = q_ref[:, sl]
        dn = (((1,), (1,)), ((), ()))
        s_loc = lax.dot_general(qh, kw[:, sl], dn, preferred_element_type=F32) + bias_ref[0, j].astype(F32)
        s_ctx = lax.dot_general(qh, kc[:, sl], dn, preferred_element_type=F32)
        m = jnp.maximum(jnp.max(s_loc, axis=-1, keepdims=True), jnp.max(s_ctx, axis=-1, keepdims=True))
        p = jnp.concatenate([jnp.exp(s_loc - m), jnp.exp(s_ctx - m)], axis=-1).astype(BF16)
        own = (lane >= j * HEAD_DIM) & (lane < (j + 1) * HEAD_DIM)
        pv = jnp.dot(p, jnp.where(own, vall, jnp.ones_like(vall)), preferred_element_type=F32)
        den = pv[:, (1 - j) * HEAD_DIM:(1 - j) * HEAD_DIM + 1]
        res.append(pv)
        inv.append(1.0 / den)
    lane_o = lax.broadcasted_iota(jnp.int32, (nq, LANES), 1)
    first = lane_o < HEAD_DIM
    o_ref[...] = (jnp.where(first, res[0], res[1]) * jnp.where(first, inv[0], inv[1])).astype(BF16)


def _na_attention(q, k, v, kc, vc, bias, *, n_batch, seq, ctx_len):
    rows = seq // GRID_W
    n_groups = rows // NA_QROWS
    nq = NA_QROWS * GRID_W
    n_pairs = NA_W // LANES
    variant = lambda g: jnp.where(g == 0, 0, jnp.where(g == n_groups - 1, 2, 1))
    return pl.pallas_call(
        functools.partial(_na_kernel, rows=rows),
        grid=(n_batch, n_pairs, n_groups),
        in_specs=[
            pl.BlockSpec((nq, LANES), lambda b, hp, g: (b * n_groups + g, hp)),
            pl.BlockSpec((seq, LANES), lambda b, hp, g: (b, hp)),
            pl.BlockSpec((seq, LANES), lambda b, hp, g: (b, hp)),
            pl.BlockSpec((ctx_len, LANES), lambda b, hp, g: (b, hp)),
            pl.BlockSpec((ctx_len, LANES), lambda b, hp, g: (b, hp)),
            pl.BlockSpec((1, 2, nq, NA_WROWS * GRID_W), lambda b, hp, g: (variant(g), hp, 0, 0)),
        ],
        out_specs=pl.BlockSpec((nq, LANES), lambda b, hp, g: (b * n_groups + g, hp)),
        out_shape=jax.ShapeDtypeStruct(q.shape, BF16),
        compiler_params=_params(3),
        name="na_attention",
    )(q, k, v, kc, vc, bias)


def _gqa_kernel(q_ref, k_ref, vt_ref, o_ref, qt_ref, s0_ref, s1_ref, m_ref, acc_ref, *, n_chunks):
    tq = q_ref.shape[0]
    group = GQA_Q_HEADS // GQA_KV_HEADS
    q_t = q_ref[...].astype(F32).T
    for hk in range(GQA_KV_HEADS):
        for gi in range(group):
            hq = hk * group + gi
            qt_ref[hk, :, gi * tq:(gi + 1) * tq] = q_t[hq * HEAD_DIM:(hq + 1) * HEAD_DIM, :].astype(BF16)
    m_ref[...] = jnp.full(m_ref.shape, NEG, F32)
    acc_ref[...] = jnp.zeros(acc_ref.shape, F32)

    def scores(c, s_ref):
        ks = pl.multiple_of(c * KEY_CHUNK, KEY_CHUNK)
        for hk in range(GQA_KV_HEADS):
            kch = k_ref[pl.ds(ks, KEY_CHUNK), hk * HEAD_DIM:(hk + 1) * HEAD_DIM]
            s_ref[hk] = jnp.dot(kch, qt_ref[hk], preferred_element_type=F32)

    def softmax_pv(c, s_ref):
        for hk in range(GQA_KV_HEADS):
            s = s_ref[hk]
            m_old = m_ref[hk]
            m_new = jnp.maximum(m_old, jnp.max(s, axis=0, keepdims=True))
            p = jnp.exp2(s - m_new).astype(BF16)
            pv = jnp.dot(vt_ref[0, hk, c], p, preferred_element_type=F32)
            acc_ref[hk] = acc_ref[hk] * jnp.exp2(m_old - m_new) + pv
            m_ref[hk] = m_new

    scores(0, s0_ref)

    def pair(i, carry):
        c = 2 * i
        scores(c + 1, s1_ref)
        softmax_pv(c, s0_ref)
        scores(c + 2, s0_ref)
        softmax_pv(c + 1, s1_ref)
        return carry

    lax.fori_loop(0, (n_chunks - 1) // 2, pair, 0)
    if (n_chunks - 1) % 2:
        scores(n_chunks - 1, s1_ref)
        softmax_pv(n_chunks - 2, s0_ref)
        softmax_pv(n_chunks - 1, s1_ref)
    else:
        softmax_pv(n_chunks - 1, s0_ref)

    heads_t = []
    for hk in range(GQA_KV_HEADS):
        acc = acc_ref[hk]
        out_t = acc[:HEAD_DIM] * (1.0 / acc[HEAD_DIM:HEAD_DIM + 1])
        heads_t += [out_t[:, gi * tq:(gi + 1) * tq] for gi in range(group)]
    o_ref[...] = jnp.concatenate(heads_t, axis=0).T.astype(BF16)


def _gqa_attention(q, k, vt, *, n_batch, sq, sk, tq):
    tpq = sq // tq
    m = (GQA_Q_HEADS // GQA_KV_HEADS) * tq
    n_chunks = sk // KEY_CHUNK
    return pl.pallas_call(
        functools.partial(_gqa_kernel, n_chunks=n_chunks),
        grid=(n_batch, tpq),
        in_specs=[
            pl.BlockSpec((tq, GQA_Q_W), lambda b, i: (b * tpq + i, 0)),
            pl.BlockSpec((sk, GQA_KV_W), lambda b, i: (b, 0)),
            pl.BlockSpec((1,) + vt.shape[1:], lambda b, i: (b, 0, 0, 0, 0)),
        ],
        out_specs=pl.BlockSpec((tq, GQA_Q_W), lambda b, i: (b * tpq + i, 0)),
        out_shape=jax.ShapeDtypeStruct(q.shape, BF16),
        scratch_shapes=[
            pltpu.VMEM((GQA_KV_HEADS, HEAD_DIM, m), BF16),
            pltpu.VMEM((GQA_KV_HEADS, KEY_CHUNK, m), F32),
            pltpu.VMEM((GQA_KV_HEADS, KEY_CHUNK, m), F32),
            pltpu.VMEM((GQA_KV_HEADS, 1, m), F32),
            pltpu.VMEM((GQA_KV_HEADS, VT_ROWS, m), F32),
        ],
        compiler_params=_params(2),
        name="gqa_attention",
    )(q, k, vt)


def _augmented_vt(v_all, n_batch, sk):
    n_chunks = sk // KEY_CHUNK
    vt = v_all.reshape(n_batch, n_chunks, KEY_CHUNK, GQA_KV_HEADS, HEAD_DIM).transpose(0, 3, 1, 4, 2)
    ones = jnp.ones((n_batch, GQA_KV_HEADS, n_chunks, VT_ROWS - HEAD_DIM, KEY_CHUNK), v_all.dtype)
    return jnp.concatenate([vt, ones], axis=3)


def _attn_kernel(q_ref, k_ref, v_ref, o_ref, *, n_kv, group, n_chunks):
    tq = q_ref.shape[0]
    dn = (((1,), (1,)), ((), ()))
    for hk in range(n_kv):
        blk = (hk * HEAD_DIM) // LANES * LANES
        off = hk * HEAD_DIM - blk
        qs = jnp.concatenate(
            [q_ref[:, (hk * group + gi) * HEAD_DIM:(hk * group + gi + 1) * HEAD_DIM] for gi in range(group)], axis=0)
        lane = lax.broadcasted_iota(jnp.int32, (KEY_CHUNK, LANES), 1)
        own = (lane >= off) & (lane < off + HEAD_DIM)

        def body(c, carry):
            m, acc = carry
            ks = pl.multiple_of(c * KEY_CHUNK, KEY_CHUNK)
            kch = k_ref[pl.ds(ks, KEY_CHUNK), blk + off:blk + off + HEAD_DIM]
            vch = v_ref[pl.ds(ks, KEY_CHUNK), blk:blk + LANES]
            s = lax.dot_general(qs, kch, dn, preferred_element_type=F32)
            m_new = jnp.maximum(m, jnp.max(s, axis=-1, keepdims=True))
            p = jnp.exp(s - m_new).astype(BF16)
            pv = jnp.dot(p, jnp.where(own, vch, jnp.ones_like(vch)), preferred_element_type=F32)
            return m_new, acc * jnp.exp(m - m_new) + pv

        m0 = jnp.full((group * tq, 1), NEG, F32)
        acc0 = jnp.zeros((group * tq, LANES), F32)
        _, acc = lax.fori_loop(0, n_chunks, body, (m0, acc0))
        den = acc[:, HEAD_DIM - off:HEAD_DIM - off + 1]
        out = acc[:, off:off + HEAD_DIM] * (1.0 / den)
        for gi in range(group):
            hq = hk * group + gi
            o_ref[:, hq * HEAD_DIM:(hq + 1) * HEAD_DIM] = out[gi * tq:(gi + 1) * tq].astype(BF16)


def _attention(q, k, v, *, n_batch, sq, sk, tq, n_kv, group):
    tpq = sq // tq
    kvw = n_kv * HEAD_DIM
    return pl.pallas_call(
        functools.partial(_attn_kernel, n_kv=n_kv, group=group, n_chunks=sk // KEY_CHUNK),
        grid=(n_batch, tpq),
        in_specs=[
            pl.BlockSpec((tq, q.shape[1]), lambda b, i: (b * tpq + i, 0)),
            pl.BlockSpec((sk, kvw), lambda b, i: (b, 0)),
            pl.BlockSpec((sk, kvw), lambda b, i: (b, 0)),
        ],
        out_specs=pl.BlockSpec((tq, q.shape[1]), lambda b, i: (b * tpq + i, 0)),
        out_shape=jax.ShapeDtypeStruct(q.shape, BF16),
        compiler_params=_params(2),
        name=f"attention_g{group}",
    )(q, k, v)


def _merge_kernel(x_ref, sh_ref, sc_ref, gate_ref, g_ref, wg_ref, a_ref, b_ref, c_ref, wa_ref, wb_ref, wc_ref,
                  wo_ref, o_ref):
    x = x_ref[...]
    d = x.shape[1]
    h = _norm_modulate(x, g_ref[...], sh_ref[0], sc_ref[0]).astype(BF16)
    y = None
    for j, (br_ref, w_ref) in enumerate(((a_ref, wa_ref), (b_ref, wb_ref), (c_ref, wc_ref))):
        gate = jax.nn.sigmoid(jnp.dot(h, wg_ref[:, j * d:(j + 1) * d], preferred_element_type=F32))
        t = gate * jnp.dot(br_ref[...], w_ref[...], preferred_element_type=F32)
        y = t if y is None else y + t
    o_ref[...] = x + gate_ref[0] * jnp.dot(y.astype(BF16), wo_ref[...], preferred_element_type=F32)


def _merge(x2, mod3, norm_g, w_gates, br_a, br_b, br_c, wa, wb, wc, wo, *, n_batch, seq, tm):
    nt, d = x2.shape
    tps = seq // tm
    tok = lambda i: (i, 0)
    mrow = lambda j: (lambda i: ((i // tps) * 6 + j, 0, 0))
    return pl.pallas_call(
        _merge_kernel,
        grid=(nt // tm,),
        in_specs=[
            pl.BlockSpec((tm, d), tok),
            pl.BlockSpec((1, 1, d), mrow(0)),
            pl.BlockSpec((1, 1, d), mrow(1)),
            pl.BlockSpec((1, 1, d), mrow(2)),
            _const_spec((1, d)),
            _const_spec(w_gates.shape),
            pl.BlockSpec((tm, CONV_W), tok),
            pl.BlockSpec((tm, NA_W), tok),
            pl.BlockSpec((tm, GQA_Q_W), tok),
            _const_spec(wa.shape),
            _const_spec(wb.shape),
            _const_spec(wc.shape),
            _const_spec(wo.shape),
        ],
        out_specs=pl.BlockSpec((tm, d), tok),
        out_shape=jax.ShapeDtypeStruct((nt, d), F32),
        compiler_params=_params(1),
        name="merge",
    )(x2, mod3, mod3, mod3, norm_g, w_gates, br_a, br_b, br_c, wa, wb, wc, wo)


def _ffn_kernel(x_ref, sh_ref, sc_ref, gate_ref, g_ref, wi_ref, wo_ref, fg_ref, o_ref, *, n_split, final_norm):
    x = x_ref[...]
    hidden = wo_ref.shape[0]
    h = _norm_modulate(x, g_ref[...], sh_ref[0], sc_ref[0]).astype(BF16)
    step = hidden // n_split
    y = None
    for j in range(n_split):
        gt = jnp.dot(h, wi_ref[:, j * step:(j + 1) * step], preferred_element_type=F32)
        up = jnp.dot(h, wi_ref[:, hidden + j * step:hidden + (j + 1) * step], preferred_element_type=F32)
        act = (gt * jax.nn.sigmoid(gt) * up).astype(BF16)
        t = jnp.dot(act, wo_ref[j * step:(j + 1) * step, :], preferred_element_type=F32)
        y = t if y is None else y + t
    out = x + gate_ref[0] * y
    if final_norm:
        ms = jnp.mean(out * out, axis=-1, keepdims=True)
        out = out * lax.rsqrt(ms + EPS) * fg_ref[...]
    o_ref[...] = out


def _ffn(x2, mod3, norm_g, w_in, w_out, final_g, *, n_batch, seq, tm, final_norm):
    nt, d = x2.shape
    tps = seq // tm
    tok = lambda i: (i, 0)
    mrow = lambda j: (lambda i: ((i // tps) * 6 + j, 0, 0))
    return pl.pallas_call(
        functools.partial(_ffn_kernel, n_split=2, final_norm=final_norm),
        grid=(nt // tm,),
        in_specs=[
            pl.BlockSpec((tm, d), tok),
            pl.BlockSpec((1, 1, d), mrow(3)),
            pl.BlockSpec((1, 1, d), mrow(4)),
            pl.BlockSpec((1, 1, d), mrow(5)),
            _const_spec((1, d)),
            _const_spec(w_in.shape),
            _const_spec(w_out.shape),
            _const_spec((1, d)),
        ],
        out_specs=pl.BlockSpec((tm, d), tok),
        out_shape=jax.ShapeDtypeStruct((nt, d), F32),
        compiler_params=_params(1),
        name="ffn_final" if final_norm else "ffn",
    )(x2, mod3, mod3, mod3, norm_g, w_in, w_out, final_g)


def _rope_tables(seq):
    t = jnp.arange(seq)
    pos_row = (t // GRID_W).astype(F32)
    pos_col = (t % GRID_W).astype(F32)
    half = HEAD_DIM // 2
    freqs = jnp.power(ROPE_THETA, -jnp.arange(0, half, 2, dtype=F32) / half)

    def part(pos):
        ang = pos[:, None] * freqs
        c, s = jnp.cos(ang), jnp.sin(ang)
        return jnp.concatenate([c, c], axis=-1), jnp.concatenate([-s, s], axis=-1)

    cr, sr = part(pos_row)
    cc, sc = part(pos_col)
    cos = jnp.concatenate([cr, cc], axis=-1)
    sin = jnp.concatenate([sr, sc], axis=-1)
    return jnp.tile(cos, (1, LANES // HEAD_DIM)), jnp.tile(sin, (1, LANES // HEAD_DIM))


def kernel(x, c, ctx, c_ctx, w_mod, b_mod, norm1_g, norm2_g, w_in, conv_w, conv_b, conv_ln_g, conv_ln_b,
           w_conv_out, na_rpb, w_na_out, q_norm_g, k_norm_g, w_gqa_out, w_out, w_ffn_in, w_ffn_out, final_g):
    n_batch, seq, d = x.shape
    ctx_len = ctx.shape[1]
    depth = w_mod.shape[0]
    rows = seq // GRID_W
    assert seq % (NA_QROWS * GRID_W) == 0 and rows >= NA_WROWS and ctx_len % KEY_CHUNK == 0
    tm = 512
    tmc = ctx_len

    n_rows = -(-(n_batch + 1) // 8) * 8
    crows = jnp.zeros((n_rows, d), F32).at[:n_batch].set(c).at[n_batch].set(c_ctx)
    mod = _modulation(crows, w_mod, b_mod)

    cos_t, sin_t = _rope_tables(seq)
    bd = jnp.asarray(np.kron(np.eye(LANES // HEAD_DIM), np.full((HEAD_DIM, HEAD_DIM), 1.0 / HEAD_DIM)), BF16)
    tile2 = lambda g: jnp.tile(g, LANES // HEAD_DIM).reshape(1, LANES)
    row = lambda g: g.reshape(1, -1)

    xl = x.reshape(n_batch * seq, d)
    xc = ctx.reshape(n_batch * ctx_len, d)
    for l in range(depth):
        last = l == depth - 1
        w_proj = w_in[l, :, :_OFF_GATES].astype(BF16)
        w_gates = w_in[l, :, _OFF_GATES:].astype(BF16)
        mod_l = mod[l, :n_batch].reshape(n_batch * 6, 1, d)
        mod_c = jnp.broadcast_to(mod[l, n_batch], (n_batch, 6 * d)).reshape(n_batch * 6, 1, d)
        qg2, kg2 = tile2(q_norm_g[l]), tile2(k_norm_g[l])
        g1, g2 = row(norm1_g[l]), row(norm2_g[l])
        wa, wb, wc = w_conv_out[l].astype(BF16), w_na_out[l].astype(BF16), w_gqa_out[l].astype(BF16)
        wo = w_out[l].astype(BF16)
        wfi, wfo = w_ffn_in[l].astype(BF16), w_ffn_out[l].astype(BF16)
        conv_args = (conv_w[l], row(conv_b[l]), row(conv_ln_g[l]), row(conv_ln_b[l]))

        u, naq, nak, nav, gq, gk, gv = _inproj(xl, mod_l, g1, w_proj, qg2, kg2, bd, cos_t, sin_t,
                                               n_batch=n_batch, seq=seq, tm=tm, rope=True)
        uc, naqc, nakc, navc, gqc, gkc, gvc = _inproj(xc, mod_c, g1, w_proj, qg2, kg2, bd,
                                                      cos_t[:ctx_len], sin_t[:ctx_len],
                                                      n_batch=n_batch, seq=ctx_len, tm=tmc, rope=False)

        br_a = _conv(u, *conv_args, n_batch=n_batch, seq=seq, tl=tm)
        bias = _na_bias_tables(na_rpb[l], rows)
        br_b = _na_attention(naq, nak, nav, nakc, navc, bias, n_batch=n_batch, seq=seq, ctx_len=ctx_len)
        cat = lambda a, b, w: jnp.concatenate(
            [a.reshape(n_batch, seq, w), b.reshape(n_batch, ctx_len, w)], axis=1).reshape(-1, w)
        vt = _augmented_vt(cat(gv, gvc, GQA_KV_W), n_batch, seq + ctx_len)
        br_c = _gqa_attention(gq, cat(gk, gkc, GQA_KV_W), vt, n_batch=n_batch, sq=seq, sk=seq + ctx_len, tq=128)
        xl = _merge(xl, mod_l, g1, w_gates, br_a, br_b, br_c, wa, wb, wc, wo, n_batch=n_batch, seq=seq, tm=tm)
        xl = _ffn(xl, mod_l, g2, wfi, wfo, row(final_g), n_batch=n_batch, seq=seq, tm=tm, final_norm=last)

        if not last:
            cbr_a = _conv(uc, *conv_args, n_batch=n_batch, seq=ctx_len, tl=tmc)
            cbr_b = _attention(naqc, nakc, navc, n_batch=n_batch, sq=ctx_len, sk=ctx_len, tq=ctx_len,
                               n_kv=NA_HEADS, group=1)
            cbr_c = _attention(gqc, gkc, gvc, n_batch=n_batch, sq=ctx_len, sk=ctx_len, tq=ctx_len,
                               n_kv=GQA_KV_HEADS, group=GQA_Q_HEADS // GQA_KV_HEADS)
            xc = _merge(xc, mod_c, g1, w_gates, cbr_a, cbr_b, cbr_c, wa, wb, wc, wo,
                        n_batch=n_batch, seq=ctx_len, tm=tmc)
            xc = _ffn(xc, mod_c, g2, wfi, wfo, row(final_g), n_batch=n_batch, seq=ctx_len, tm=tmc,
                      final_norm=False)
    return xl.reshape(n_batch, seq, d)
```

```python
import functools

import numpy as np
import jax
import jax.numpy as jnp
from jax import lax
from jax.experimental import pallas as pl
from jax.experimental.pallas import tpu as pltpu

F32 = jnp.float32
BF16 = jnp.bfloat16

GRID_W = 64
HEAD_DIM = 64
CONV_W = 512
CONV_K = 31
CONV_HALO = 16
NA_HEADS = 8
NA_W = NA_HEADS * HEAD_DIM
NA_KH = 8
NA_KW = 16
NA_QROWS = 4
NA_WROWS = 12
GQA_Q_HEADS = 8
GQA_KV_HEADS = 2
GQA_Q_W = GQA_Q_HEADS * HEAD_DIM
GQA_KV_W = GQA_KV_HEADS * HEAD_DIM
ROPE_THETA = 10000.0
EPS = 1e-6
NEG = -1e30
SCALE = HEAD_DIM ** -0.5
LOG2E = 1.4426950408889634
LANES = 128
KEY_CHUNK = 256
VT_ROWS = HEAD_DIM + 16
VMEM_LIMIT = 56 * 1024 * 1024

_OFF_GLU = 0
_OFF_NAQ = _OFF_GLU + 2 * CONV_W
_OFF_NAK = _OFF_NAQ + NA_W
_OFF_NAV = _OFF_NAK + NA_W
_OFF_GQ = _OFF_NAV + NA_W
_OFF_GK = _OFF_GQ + GQA_Q_W
_OFF_GV = _OFF_GK + GQA_KV_W
_OFF_GATES = _OFF_GV + GQA_KV_W


def _params(n_axes):
    return pltpu.CompilerParams(dimension_semantics=("arbitrary",) * n_axes, vmem_limit_bytes=VMEM_LIMIT)


def _const_spec(shape):
    zeros = (0,) * len(shape)
    return pl.BlockSpec(shape, lambda *_: zeros, pipeline_mode=pl.Buffered(1))


def _norm_modulate(x, g, shift, scale):
    ms = jnp.mean(x * x, axis=-1, keepdims=True)
    h = x * lax.rsqrt(ms + EPS) * g
    return h * (1.0 + scale) + shift


def _mod_kernel(c_ref, w_ref, b_ref, o_ref):
    c = c_ref[...]
    s = c * jax.nn.sigmoid(c)
    o_ref[0] = jnp.dot(s, w_ref[0], preferred_element_type=F32, precision=lax.Precision.HIGHEST) + b_ref[0]


def _modulation(crows, w_mod, b_mod):
    n_layers, d, d6 = w_mod.shape
    r = crows.shape[0]
    nb = d6 // d
    return pl.pallas_call(
        _mod_kernel,
        grid=(n_layers, nb),
        in_specs=[
            pl.BlockSpec((r, d), lambda l, j: (0, 0)),
            pl.BlockSpec((1, d, d), lambda l, j: (l, 0, j)),
            pl.BlockSpec((1, 1, d), lambda l, j: (l, 0, j)),
        ],
        out_specs=pl.BlockSpec((1, r, d), lambda l, j: (l, 0, j)),
        out_shape=jax.ShapeDtypeStruct((n_layers, r, d6), F32),
        compiler_params=_params(2),
        name="modulation",
    )(crows, w_mod, b_mod.reshape(n_layers, 1, d6))


def _head_rms(t, bd):
    ms = jnp.dot((t * t).astype(BF16), bd, preferred_element_type=F32)
    return t * lax.rsqrt(ms + EPS)


def _rope(t, cos, sin_signed, first_half):
    fwd = pltpu.roll(t, LANES - 16, axis=1)
    bwd = pltpu.roll(t, 16, axis=1)
    return t * cos + jnp.where(first_half, fwd, bwd) * sin_signed


def _inproj_kernel(x_ref, sh_ref, sc_ref, g_ref, w_ref, qg_ref, kg_ref, bd_ref, cos_ref, sin_ref,
                   u_ref, naq_ref, nak_ref, nav_ref, gq_ref, gk_ref, gv_ref, *, rope):
    h = _norm_modulate(x_ref[...], g_ref[...], sh_ref[0], sc_ref[0]).astype(BF16)

    def proj(lo, hi):
        return jnp.dot(h, w_ref[:, lo:hi], preferred_element_type=F32)

    q_scale = SCALE * LOG2E if rope else SCALE
    a = proj(_OFF_GLU, _OFF_NAQ)
    u_ref[...] = (a[:, :CONV_W] * jax.nn.sigmoid(a[:, CONV_W:])).astype(BF16)
    naq_ref[...] = (proj(_OFF_NAQ, _OFF_NAK) * q_scale).astype(BF16)
    nak_ref[...] = proj(_OFF_NAK, _OFF_NAV).astype(BF16)
    nav_ref[...] = proj(_OFF_NAV, _OFF_GQ).astype(BF16)
    gv_ref[...] = proj(_OFF_GV, _OFF_GATES).astype(BF16)

    bd = bd_ref[...]
    if rope:
        cos = cos_ref[...]
        sin = sin_ref[...]
        lane = lax.broadcasted_iota(jnp.int32, cos.shape, 1)
        first_half = (lane % 32) < 16

    def finish(t, gain):
        t = _head_rms(t, bd) * gain
        if rope:
            t = _rope(t, cos, sin, first_half)
        return t

    q = proj(_OFF_GQ, _OFF_GK)
    for j in range(GQA_Q_W // LANES):
        qj = finish(q[:, j * LANES:(j + 1) * LANES], qg_ref[...])
        gq_ref[:, j * LANES:(j + 1) * LANES] = (qj * q_scale).astype(BF16)
    gk_ref[...] = finish(proj(_OFF_GK, _OFF_GV), kg_ref[...]).astype(BF16)


def _inproj(x2, mod3, norm_g, w, qg2, kg2, bd, cos_t, sin_t, *, n_batch, seq, tm, rope):
    nt, d = x2.shape
    tps = seq // tm
    tok = lambda s, b: (b * tps + s, 0)
    outs = [CONV_W, NA_W, NA_W, NA_W, GQA_Q_W, GQA_KV_W, GQA_KV_W]
    return pl.pallas_call(
        functools.partial(_inproj_kernel, rope=rope),
        grid=(tps, n_batch),
        in_specs=[
            pl.BlockSpec((tm, d), tok),
            pl.BlockSpec((1, 1, d), lambda s, b: (b * 6 + 0, 0, 0)),
            pl.BlockSpec((1, 1, d), lambda s, b: (b * 6 + 1, 0, 0)),
            _const_spec((1, d)),
            _const_spec(w.shape),
            _const_spec((1, LANES)),
            _const_spec((1, LANES)),
            _const_spec((LANES, LANES)),
            pl.BlockSpec((tm, LANES), lambda s, b: (s, 0)),
            pl.BlockSpec((tm, LANES), lambda s, b: (s, 0)),
        ],
        out_specs=[pl.BlockSpec((tm, n), tok) for n in outs],
        out_shape=[jax.ShapeDtypeStruct((nt, n), BF16) for n in outs],
        compiler_params=_params(2),
        name="inproj_rope" if rope else "inproj_ctx",
    )(x2, mod3, mod3, norm_g, w, qg2, kg2, bd, cos_t, sin_t)


def _conv_kernel(prev_ref, cur_ref, next_ref, w_ref, b_ref, lg_ref, lb_ref, o_ref, win_ref, *, tiles_per_seq):
    i = pl.program_id(1)
    tl = cur_ref.shape[0]
    prev = jnp.where(i == 0, 0.0, prev_ref[...].astype(F32))
    nxt = jnp.where(i == tiles_per_seq - 1, 0.0, next_ref[...].astype(F32))
    win_ref[0:CONV_HALO, :] = prev
    win_ref[CONV_HALO:CONV_HALO + tl, :] = cur_ref[...].astype(F32)
    win_ref[CONV_HALO + tl:, :] = nxt
    base = CONV_HALO - CONV_K // 2
    acc = jnp.zeros((tl, CONV_W), F32)
    for k in range(CONV_K):
        acc = acc + win_ref[base + k:base + k + tl, :] * w_ref[k:k + 1, :]
    y = acc + b_ref[...]
    mu = jnp.mean(y, axis=-1, keepdims=True)
    yc = y - mu
    var = jnp.mean(yc * yc, axis=-1, keepdims=True)
    z = yc * lax.rsqrt(var + EPS) * lg_ref[...] + lb_ref[...]
    o_ref[...] = (z * jax.nn.sigmoid(z)).astype(BF16)


def _conv(u, conv_w, conv_b, ln_g, ln_b, *, n_batch, seq, tl):
    nt = u.shape[0]
    tps = seq // tl
    hb = tl // CONV_HALO
    last_hb = nt // CONV_HALO - 1
    return pl.pallas_call(
        functools.partial(_conv_kernel, tiles_per_seq=tps),
        grid=(n_batch, tps),
        in_specs=[
            pl.BlockSpec((CONV_HALO, CONV_W), lambda b, i: (jnp.maximum((b * tps + i) * hb - 1, 0), 0)),
            pl.BlockSpec((tl, CONV_W), lambda b, i: (b * tps + i, 0)),
            pl.BlockSpec((CONV_HALO, CONV_W), lambda b, i: (jnp.minimum((b * tps + i + 1) * hb, last_hb), 0)),
            _const_spec((CONV_K, CONV_W)),
            _const_spec((1, CONV_W)),
            _const_spec((1, CONV_W)),
            _const_spec((1, CONV_W)),
        ],
        out_specs=pl.BlockSpec((tl, CONV_W), lambda b, i: (b * tps + i, 0)),
        out_shape=jax.ShapeDtypeStruct((nt, CONV_W), BF16),
        scratch_shapes=[pltpu.VMEM((tl + 2 * CONV_HALO, CONV_W), F32)],
        compiler_params=_params(2),
        name="conv_ln_swish",
    )(u, u, u, conv_w, conv_b, ln_g, ln_b)


def _na_bias_tables(rpb, rows):
    h = rpb.shape[0]
    cols = np.arange(GRID_W)
    win0 = np.clip(cols - NA_KW // 2, 0, GRID_W - NA_KW)
    col_ok = (cols[None, :] >= win0[:, None]) & (cols[None, :] < win0[:, None] + NA_KW)
    dc = np.clip(cols[None, :] - cols[:, None] + NA_KW - 1, 0, 2 * NA_KW - 2)
    tiles = jnp.where(jnp.asarray(col_ok)[None, None], rpb[:, :, dc], NEG)
    tiles = jnp.concatenate([tiles, jnp.full((h, 1, GRID_W, GRID_W), NEG, rpb.dtype)], axis=1)
    n_groups = rows // NA_QROWS
    tile_idx = np.zeros((3, NA_QROWS, NA_WROWS), np.int32)
    for v, g in enumerate((0, min(1, n_groups - 1), n_groups - 1)):
        w = int(np.clip(NA_QROWS * g - NA_KH // 2, 0, rows - NA_WROWS))
        for ql in range(NA_QROWS):
            r = NA_QROWS * g + ql
            r0 = int(np.clip(r - NA_KH // 2, 0, rows - NA_KH))
            for kl in range(NA_WROWS):
                kr = w + kl
                tile_idx[v, ql, kl] = kr - r + NA_KH - 1 if r0 <= kr < r0 + NA_KH else 2 * NA_KH - 1
    t = jnp.take(tiles, jnp.asarray(tile_idx.reshape(-1)), axis=1)
    t = t.reshape(h, 3, NA_QROWS, NA_WROWS, GRID_W, GRID_W).transpose(1, 0, 3, 5, 2, 4)
    t = t.reshape(3, h, NA_WROWS * GRID_W, NA_QROWS * GRID_W)
    return t * LOG2E


def _na_kernel(q_ref, k_ref, vt_ref, kc_ref, vct_ref, bias_ref, o_ref, s0_ref, s1_ref, ot_ref, *, n_chunks):
    g = pl.program_id(1)
    nq = NA_QROWS * GRID_W
    win_chunks = NA_WROWS * GRID_W // KEY_CHUNK
    c0 = jnp.clip(g - 1, 0, n_chunks - win_chunks)
    start = pl.multiple_of(c0 * KEY_CHUNK, KEY_CHUNK)
    n_loc = win_chunks * KEY_CHUNK
    q_t = q_ref[...].astype(F32).T.astype(BF16)
    zeros = jnp.zeros((HEAD_DIM, nq), BF16)
    s_refs = (s0_ref, s1_ref)

    def scores(h, s_ref):
        hp, j = divmod(h, LANES // HEAD_DIM)
        qh = q_t[h * HEAD_DIM:(h + 1) * HEAD_DIM]
        qz = jnp.concatenate([qh, zeros] if j == 0 else [zeros, qh], axis=0)
        kwin = k_ref[pl.ds(start, n_loc), hp * LANES:(hp + 1) * LANES]
        s_ref[0:n_loc, :] = jnp.dot(kwin, qz, preferred_element_type=F32) + bias_ref[0, h]
        s_ref[n_loc:, :] = jnp.dot(kc_ref[:, hp * LANES:(hp + 1) * LANES], qz, preferred_element_type=F32)

    def softmax_pv(h, s_ref):
        s = s_ref[...]
        p = jnp.exp2(s - jnp.max(s, axis=0, keepdims=True)).astype(BF16)
        acc = jnp.dot(vct_ref[0, h], p[n_loc:], preferred_element_type=F32)
        for t in range(win_chunks):
            acc = acc + jnp.dot(vt_ref[0, h, c0 + t], p[t * KEY_CHUNK:(t + 1) * KEY_CHUNK],
                                preferred_element_type=F32)
        ot_ref[h * HEAD_DIM:(h + 1) * HEAD_DIM, :] = acc[:HEAD_DIM] * (1.0 / acc[HEAD_DIM:HEAD_DIM + 1])

    scores(0, s_refs[0])
    for h in range(NA_HEADS):
        if h + 1 < NA_HEADS:
            scores(h + 1, s_refs[(h + 1) % 2])
        softmax_pv(h, s_refs[h % 2])
    o_ref[...] = ot_ref[...].T.astype(BF16)


def _na_attention(q, k, vt, kc, vct, bias, *, n_batch, seq, ctx_len):
    rows = seq // GRID_W
    n_groups = rows // NA_QROWS
    nq = NA_QROWS * GRID_W
    variant = lambda g: jnp.where(g == 0, 0, jnp.where(g == n_groups - 1, 2, 1))
    return pl.pallas_call(
        functools.partial(_na_kernel, n_chunks=seq // KEY_CHUNK),
        grid=(n_batch, n_groups),
        in_specs=[
            pl.BlockSpec((nq, NA_W), lambda b, g: (b * n_groups + g, 0)),
            pl.BlockSpec((seq, NA_W), lambda b, g: (b, 0)),
            pl.BlockSpec((1,) + vt.shape[1:], lambda b, g: (b, 0, 0, 0, 0)),
            pl.BlockSpec((ctx_len, NA_W), lambda b, g: (b, 0)),
            pl.BlockSpec((1,) + vct.shape[1:], lambda b, g: (b, 0, 0, 0)),
            pl.BlockSpec((1,) + bias.shape[1:], lambda b, g: (variant(g), 0, 0, 0)),
        ],
        out_specs=pl.BlockSpec((nq, NA_W), lambda b, g: (b * n_groups + g, 0)),
        out_shape=jax.ShapeDtypeStruct(q.shape, BF16),
        scratch_shapes=[
            pltpu.VMEM((NA_WROWS * GRID_W + ctx_len, nq), F32),
            pltpu.VMEM((NA_WROWS * GRID_W + ctx_len, nq), F32),
            pltpu.VMEM((NA_W, nq), F32),
        ],
        compiler_params=_params(2),
        name="na_attention",
    )(q, k, vt, kc, vct, bias)


def _gqa_kernel(q_ref, k_ref, vt_ref, o_ref, qt_ref, s0_ref, s1_ref, m_ref, acc_ref, *, n_chunks):
    tq = q_ref.shape[0]
    group = GQA_Q_HEADS // GQA_KV_HEADS
    q_t = q_ref[...].astype(F32).T
    for hk in range(GQA_KV_HEADS):
        for gi in range(group):
            hq = hk * group + gi
            qt_ref[hk, :, gi * tq:(gi + 1) * tq] = q_t[hq * HEAD_DIM:(hq + 1) * HEAD_DIM, :].astype(BF16)
    m_ref[...] = jnp.full(m_ref.shape, NEG, F32)
    acc_ref[...] = jnp.zeros(acc_ref.shape, F32)

    def scores(c, s_ref):
        ks = pl.multiple_of(c * KEY_CHUNK, KEY_CHUNK)
        for hk in range(GQA_KV_HEADS):
            kch = k_ref[pl.ds(ks, KEY_CHUNK), hk * HEAD_DIM:(hk + 1) * HEAD_DIM]
            s_ref[hk] = jnp.dot(kch, qt_ref[hk], preferred_element_type=F32)

    def softmax_pv(c, s_ref):
        for hk in range(GQA_KV_HEADS):
            s = s_ref[hk]
            m_old = m_ref[hk]
            m_new = jnp.maximum(m_old, jnp.max(s, axis=0, keepdims=True))
            p = jnp.exp2(s - m_new).astype(BF16)
            pv = jnp.dot(vt_ref[0, hk, c], p, preferred_element_type=F32)
            acc_ref[hk] = acc_ref[hk] * jnp.exp2(m_old - m_new) + pv
            m_ref[hk] = m_new

    scores(0, s0_ref)

    def pair(i, carry):
        c = 2 * i
        scores(c + 1, s1_ref)
        softmax_pv(c, s0_ref)
        scores(c + 2, s0_ref)
        softmax_pv(c + 1, s1_ref)
        return carry

    lax.fori_loop(0, (n_chunks - 1) // 2, pair, 0)
    if (n_chunks - 1) % 2:
        scores(n_chunks - 1, s1_ref)
        softmax_pv(n_chunks - 2, s0_ref)
        softmax_pv(n_chunks - 1, s1_ref)
    else:
        softmax_pv(n_chunks - 1, s0_ref)

    heads_t = []
    for hk in range(GQA_KV_HEADS):
        acc = acc_ref[hk]
        out_t = acc[:HEAD_DIM] * (1.0 / acc[HEAD_DIM:HEAD_DIM + 1])
        heads_t += [out_t[:, gi * tq:(gi + 1) * tq] for gi in range(group)]
    o_ref[...] = jnp.concatenate(heads_t, axis=0).T.astype(BF16)


def _gqa_attention(q, k, vt, *, n_batch, sq, sk, tq):
    tpq = sq // tq
    m = (GQA_Q_HEADS // GQA_KV_HEADS) * tq
    n_chunks = sk // KEY_CHUNK
    return pl.pallas_call(
        functools.partial(_gqa_kernel, n_chunks=n_chunks),
        grid=(n_batch, tpq),
        in_specs=[
            pl.BlockSpec((tq, GQA_Q_W), lambda b, i: (b * tpq + i, 0)),
            pl.BlockSpec((sk, GQA_KV_W), lambda b, i: (b, 0)),
            pl.BlockSpec((1,) + vt.shape[1:], lambda b, i: (b, 0, 0, 0, 0)),
        ],
        out_specs=pl.BlockSpec((tq, GQA_Q_W), lambda b, i: (b * tpq + i, 0)),
        out_shape=jax.ShapeDtypeStruct(q.shape, BF16),
        scratch_shapes=[
            pltpu.VMEM((GQA_KV_HEADS, HEAD_DIM, m), BF16),
            pltpu.VMEM((GQA_KV_HEADS, KEY_CHUNK, m), F32),
            pltpu.VMEM((GQA_KV_HEADS, KEY_CHUNK, m), F32),
            pltpu.VMEM((GQA_KV_HEADS, 1, m), F32),
            pltpu.VMEM((GQA_KV_HEADS, VT_ROWS, m), F32),
        ],
        compiler_params=_params(2),
        name="gqa_attention",
    )(q, k, vt)


def _augmented_vt(v, n_batch, sk):
    n_chunks = sk // KEY_CHUNK
    n_heads = v.shape[1] // HEAD_DIM
    vt = v.reshape(n_batch, n_chunks, KEY_CHUNK, n_heads, HEAD_DIM).transpose(0, 3, 1, 4, 2)
    ones = jnp.ones((n_batch, n_heads, n_chunks, VT_ROWS - HEAD_DIM, KEY_CHUNK), v.dtype)
    return jnp.concatenate([vt, ones], axis=3)


def _attn_kernel(q_ref, k_ref, v_ref, o_ref, *, n_kv, group, n_chunks):
    tq = q_ref.shape[0]
    dn = (((1,), (1,)), ((), ()))
    for hk in range(n_kv):
        blk = (hk * HEAD_DIM) // LANES * LANES
        off = hk * HEAD_DIM - blk
        qs = jnp.concatenate(
            [q_ref[:, (hk * group + gi) * HEAD_DIM:(hk * group + gi + 1) * HEAD_DIM] for gi in range(group)], axis=0)
        lane = lax.broadcasted_iota(jnp.int32, (KEY_CHUNK, LANES), 1)
        own = (lane >= off) & (lane < off + HEAD_DIM)

        def body(c, carry):
            m, acc = carry
            ks = pl.multiple_of(c * KEY_CHUNK, KEY_CHUNK)
            kch = k_ref[pl.ds(ks, KEY_CHUNK), blk + off:blk + off + HEAD_DIM]
            vch = v_ref[pl.ds(ks, KEY_CHUNK), blk:blk + LANES]
            s = lax.dot_general(qs, kch, dn, preferred_element_type=F32)
            m_new = jnp.maximum(m, jnp.max(s, axis=-1, keepdims=True))
            p = jnp.exp(s - m_new).astype(BF16)
            pv = jnp.dot(p, jnp.where(own, vch, jnp.ones_like(vch)), preferred_element_type=F32)
            return m_new, acc * jnp.exp(m - m_new) + pv

        m0 = jnp.full((group * tq, 1), NEG, F32)
        acc0 = jnp.zeros((group * tq, LANES), F32)
        _, acc = lax.fori_loop(0, n_chunks, body, (m0, acc0))
        den = acc[:, HEAD_DIM - off:HEAD_DIM - off + 1]
        out = acc[:, off:off + HEAD_DIM] * (1.0 / den)
        for gi in range(group):
            hq = hk * group + gi
            o_ref[:, hq * HEAD_DIM:(hq + 1) * HEAD_DIM] = out[gi * tq:(gi + 1) * tq].astype(BF16)


def _attention(q, k, v, *, n_batch, sq, sk, tq, n_kv, group):
    tpq = sq // tq
    kvw = n_kv * HEAD_DIM
    return pl.pallas_call(
        functools.partial(_attn_kernel, n_kv=n_kv, group=group, n_chunks=sk // KEY_CHUNK),
        grid=(n_batch, tpq),
        in_specs=[
            pl.BlockSpec((tq, q.shape[1]), lambda b, i: (b * tpq + i, 0)),
            pl.BlockSpec((sk, kvw), lambda b, i: (b, 0)),
            pl.BlockSpec((sk, kvw), lambda b, i: (b, 0)),
        ],
        out_specs=pl.BlockSpec((tq, q.shape[1]), lambda b, i: (b * tpq + i, 0)),
        out_shape=jax.ShapeDtypeStruct(q.shape, BF16),
        compiler_params=_params(2),
        name=f"attention_g{group}",
    )(q, k, v)


def _merge_kernel(x_ref, sh_ref, sc_ref, gate_ref, g_ref, wg_ref, a_ref, b_ref, c_ref, wa_ref, wb_ref, wc_ref,
                  wo_ref, o_ref):
    x = x_ref[...]
    d = x.shape[1]
    h = _norm_modulate(x, g_ref[...], sh_ref[0], sc_ref[0]).astype(BF16)
    y = None
    for j, (br_ref, w_ref) in enumerate(((a_ref, wa_ref), (b_ref, wb_ref), (c_ref, wc_ref))):
        gate = jax.nn.sigmoid(jnp.dot(h, wg_ref[:, j * d:(j + 1) * d], preferred_element_type=F32))
        t = gate * jnp.dot(br_ref[...], w_ref[...], preferred_element_type=F32)
        y = t if y is None else y + t
    o_ref[...] = x + gate_ref[0] * jnp.dot(y.astype(BF16), wo_ref[...], preferred_element_type=F32)


def _merge(x2, mod3, norm_g, w_gates, br_a, br_b, br_c, wa, wb, wc, wo, *, n_batch, seq, tm):
    nt, d = x2.shape
    tps = seq // tm
    tok = lambda i: (i, 0)
    mrow = lambda j: (lambda i: ((i // tps) * 6 + j, 0, 0))
    return pl.pallas_call(
        _merge_kernel,
        grid=(nt // tm,),
        in_specs=[
            pl.BlockSpec((tm, d), tok),
            pl.BlockSpec((1, 1, d), mrow(0)),
            pl.BlockSpec((1, 1, d), mrow(1)),
            pl.BlockSpec((1, 1, d), mrow(2)),
            _const_spec((1, d)),
            _const_spec(w_gates.shape),
            pl.BlockSpec((tm, CONV_W), tok),
            pl.BlockSpec((tm, NA_W), tok),
            pl.BlockSpec((tm, GQA_Q_W), tok),
            _const_spec(wa.shape),
            _const_spec(wb.shape),
            _const_spec(wc.shape),
            _const_spec(wo.shape),
        ],
        out_specs=pl.BlockSpec((tm, d), tok),
        out_shape=jax.ShapeDtypeStruct((nt, d), F32),
        compiler_params=_params(1),
        name="merge",
    )(x2, mod3, mod3, mod3, norm_g, w_gates, br_a, br_b, br_c, wa, wb, wc, wo)


def _ffn_kernel(x_ref, sh_ref, sc_ref, gate_ref, g_ref, wi_ref, wo_ref, fg_ref, o_ref, *, n_split, final_norm):
    x = x_ref[...]
    hidden = wo_ref.shape[0]
    h = _norm_modulate(x, g_ref[...], sh_ref[0], sc_ref[0]).astype(BF16)
    step = hidden // n_split
    y = None
    for j in range(n_split):
        gt = jnp.dot(h, wi_ref[:, j * step:(j + 1) * step], preferred_element_type=F32)
        up = jnp.dot(h, wi_ref[:, hidden + j * step:hidden + (j + 1) * step], preferred_element_type=F32)
        act = (gt * jax.nn.sigmoid(gt) * up).astype(BF16)
        t = jnp.dot(act, wo_ref[j * step:(j + 1) * step, :], preferred_element_type=F32)
        y = t if y is None else y + t
    out = x + gate_ref[0] * y
    if final_norm:
        ms = jnp.mean(out * out, axis=-1, keepdims=True)
        out = out * lax.rsqrt(ms + EPS) * fg_ref[...]
    o_ref[...] = out


def _ffn(x2, mod3, norm_g, w_in, w_out, final_g, *, n_batch, seq, tm, final_norm):
    nt, d = x2.shape
    tps = seq // tm
    tok = lambda i: (i, 0)
    mrow = lambda j: (lambda i: ((i // tps) * 6 + j, 0, 0))
    return pl.pallas_call(
        functools.partial(_ffn_kernel, n_split=2, final_norm=final_norm),
        grid=(nt // tm,),
        in_specs=[
            pl.BlockSpec((tm, d), tok),
            pl.BlockSpec((1, 1, d), mrow(3)),
            pl.BlockSpec((1, 1, d), mrow(4)),
            pl.BlockSpec((1, 1, d), mrow(5)),
            _const_spec((1, d)),
            _const_spec(w_in.shape),
            _const_spec(w_out.shape),
            _const_spec((1, d)),
        ],
        out_specs=pl.BlockSpec((tm, d), tok),
        out_shape=jax.ShapeDtypeStruct((nt, d), F32),
        compiler_params=_params(1),
        name="ffn_final" if final_norm else "ffn",
    )(x2, mod3, mod3, mod3, norm_g, w_in, w_out, final_g)


def _rope_tables(seq):
    t = jnp.arange(seq)
    pos_row = (t // GRID_W).astype(F32)
    pos_col = (t % GRID_W).astype(F32)
    half = HEAD_DIM // 2
    freqs = jnp.power(ROPE_THETA, -jnp.arange(0, half, 2, dtype=F32) / half)

    def part(pos):
        ang = pos[:, None] * freqs
        c, s = jnp.cos(ang), jnp.sin(ang)
        return jnp.concatenate([c, c], axis=-1), jnp.concatenate([-s, s], axis=-1)

    cr, sr = part(pos_row)
    cc, sc = part(pos_col)
    cos = jnp.concatenate([cr, cc], axis=-1)
    sin = jnp.concatenate([sr, sc], axis=-1)
    return jnp.tile(cos, (1, LANES // HEAD_DIM)), jnp.tile(sin, (1, LANES // HEAD_DIM))


def kernel(x, c, ctx, c_ctx, w_mod, b_mod, norm1_g, norm2_g, w_in, conv_w, conv_b, conv_ln_g, conv_ln_b,
           w_conv_out, na_rpb, w_na_out, q_norm_g, k_norm_g, w_gqa_out, w_out, w_ffn_in, w_ffn_out, final_g):
    n_batch, seq, d = x.shape
    ctx_len = ctx.shape[1]
    depth = w_mod.shape[0]
    rows = seq // GRID_W
    assert seq % (NA_QROWS * GRID_W) == 0 and rows >= NA_WROWS and ctx_len == KEY_CHUNK
    tm = 512
    tmc = ctx_len

    n_rows = -(-(n_batch + 1) // 8) * 8
    crows = jnp.zeros((n_rows, d), F32).at[:n_batch].set(c).at[n_batch].set(c_ctx)
    mod = _modulation(crows, w_mod, b_mod)

    cos_t, sin_t = _rope_tables(seq)
    bd = jnp.asarray(np.kron(np.eye(LANES // HEAD_DIM), np.full((HEAD_DIM, HEAD_DIM), 1.0 / HEAD_DIM)), BF16)
    tile2 = lambda g: jnp.tile(g, LANES // HEAD_DIM).reshape(1, LANES)
    row = lambda g: g.reshape(1, -1)

    xl = x.reshape(n_batch * seq, d)
    xc = ctx.reshape(n_batch * ctx_len, d)
    for l in range(depth):
        last = l == depth - 1
        w_proj = w_in[l, :, :_OFF_GATES].astype(BF16)
        w_gates = w_in[l, :, _OFF_GATES:].astype(BF16)
        mod_l = mod[l, :n_batch].reshape(n_batch * 6, 1, d)
        mod_c = jnp.broadcast_to(mod[l, n_batch], (n_batch, 6 * d)).reshape(n_batch * 6, 1, d)
        qg2, kg2 = tile2(q_norm_g[l]), tile2(k_norm_g[l])
        g1, g2 = row(norm1_g[l]), row(norm2_g[l])
        wa, wb, wc = w_conv_out[l].astype(BF16), w_na_out[l].astype(BF16), w_gqa_out[l].astype(BF16)
        wo = w_out[l].astype(BF16)
        wfi, wfo = w_ffn_in[l].astype(BF16), w_ffn_out[l].astype(BF16)
        conv_args = (conv_w[l], row(conv_b[l]), row(conv_ln_g[l]), row(conv_ln_b[l]))

        u, naq, nak, nav, gq, gk, gv = _inproj(xl, mod_l, g1, w_proj, qg2, kg2, bd, cos_t, sin_t,
                                               n_batch=n_batch, seq=seq, tm=tm, rope=True)
        uc, naqc, nakc, navc, gqc, gkc, gvc = _inproj(xc, mod_c, g1, w_proj, qg2, kg2, bd,
                                                      cos_t[:ctx_len], sin_t[:ctx_len],
                                                      n_batch=n_batch, seq=ctx_len, tm=tmc, rope=False)

        br_a = _conv(u, *conv_args, n_batch=n_batch, seq=seq, tl=tm)
        bias = _na_bias_tables(na_rpb[l], rows)
        br_b = _na_attention(naq, nak, _augmented_vt(nav, n_batch, seq), nakc,
                             _augmented_vt(navc, n_batch, ctx_len)[:, :, 0], bias,
                             n_batch=n_batch, seq=seq, ctx_len=ctx_len)
        cat = lambda a, b, w: jnp.concatenate(
            [a.reshape(n_batch, seq, w), b.reshape(n_batch, ctx_len, w)], axis=1).reshape(-1, w)
        vt = _augmented_vt(cat(gv, gvc, GQA_KV_W), n_batch, seq + ctx_len)
        br_c = _gqa_attention(gq, cat(gk, gkc, GQA_KV_W), vt, n_batch=n_batch, sq=seq, sk=seq + ctx_len, tq=128)
        xl = _merge(xl, mod_l, g1, w_gates, br_a, br_b, br_c, wa, wb, wc, wo, n_batch=n_batch, seq=seq, tm=tm)
        xl = _ffn(xl, mod_l, g2, wfi, wfo, row(final_g), n_batch=n_batch, seq=seq, tm=tm, final_norm=last)

        if not last:
            cbr_a = _conv(uc, *conv_args, n_batch=n_batch, seq=ctx_len, tl=tmc)
            cbr_b = _attention(naqc, nakc, navc, n_batch=n_batch, sq=ctx_len, sk=ctx_len, tq=ctx_len,
                               n_kv=NA_HEADS, group=1)
            cbr_c = _attention(gqc, gkc, gvc, n_batch=n_batch, sq=ctx_len, sk=ctx_len, tq=ctx_len,
                               n_kv=GQA_KV_HEADS, group=GQA_Q_HEADS // GQA_KV_HEADS)
            xc = _merge(xc, mod_c, g1, w_gates, cbr_a, cbr_b, cbr_c, wa, wb, wc, wo,
                        n_batch=n_batch, seq=ctx_len, tm=tmc)
            xc = _ffn(xc, mod_c, g2, wfi, wfo, row(final_g), n_batch=n_batch, seq=ctx_len, tm=tmc,
                      final_norm=False)
    return xl.reshape(n_batch, seq, d)
```

```python
import functools

import numpy as np
import jax
import jax.numpy as jnp
from jax import lax
from jax.experimental import pallas as pl
from jax.experimental.pallas import tpu as pltpu

F32 = jnp.float32
BF16 = jnp.bfloat16

GRID_W = 64
HEAD_DIM = 64
CONV_W = 512
CONV_K = 31
CONV_HALO = 16
NA_HEADS = 8
NA_W = NA_HEADS * HEAD_DIM
NA_KH = 8
NA_KW = 16
NA_QROWS = 4
NA_WROWS = 12
GQA_Q_HEADS = 8
GQA_KV_HEADS = 2
GQA_GROUP = GQA_Q_HEADS // GQA_KV_HEADS
GQA_Q_W = GQA_Q_HEADS * HEAD_DIM
GQA_KV_W = GQA_KV_HEADS * HEAD_DIM
ROPE_THETA = 10000.0
EPS = 1e-6
NEG = -1e30
SCALE = HEAD_DIM ** -0.5
LOG2E = 1.4426950408889634
LANES = 128
KEY_CHUNK = 256
ONES_ROWS = 16
VMEM_LIMIT = 56 * 1024 * 1024

_IN_GLU = 0
_IN_NAQ = _IN_GLU + 2 * CONV_W
_IN_NAK = _IN_NAQ + NA_W
_IN_NAV = _IN_NAK + NA_W
_IN_GQ = _IN_NAV + NA_W
_IN_GK = _IN_GQ + GQA_Q_W
_IN_GV = _IN_GK + GQA_KV_W
_IN_GATES = _IN_GV + GQA_KV_W
_P_GLU = 0
_P_NAQ = _P_GLU + 2 * CONV_W
_P_NAK = _P_NAQ + NA_W
_P_GQ = _P_NAK + NA_W
_P_GK = _P_GQ + GQA_Q_W
_P_END = _P_GK + GQA_KV_W


def _params(n_axes):
    return pltpu.CompilerParams(dimension_semantics=("arbitrary",) * n_axes, vmem_limit_bytes=VMEM_LIMIT)


def _const_spec(shape):
    zeros = (0,) * len(shape)
    return pl.BlockSpec(shape, lambda *_: zeros, pipeline_mode=pl.Buffered(1))


def _norm_modulate(x, g, shift, scale):
    ms = jnp.mean(x * x, axis=-1, keepdims=True)
    h = x * lax.rsqrt(ms + EPS) * g
    return h * (1.0 + scale) + shift


def _with_ones(v_t):
    return jnp.concatenate([v_t, jnp.ones((ONES_ROWS, v_t.shape[1]), v_t.dtype)], axis=0)


def _mod_kernel(c_ref, w_ref, b_ref, o_ref):
    c = c_ref[...]
    s = c * jax.nn.sigmoid(c)
    o_ref[0] = jnp.dot(s, w_ref[0], preferred_element_type=F32, precision=lax.Precision.HIGHEST) + b_ref[0]


def _modulation(crows, w_mod, b_mod):
    n_layers, d, d6 = w_mod.shape
    r = crows.shape[0]
    nb = d6 // d
    return pl.pallas_call(
        _mod_kernel,
        grid=(n_layers, nb),
        in_specs=[
            pl.BlockSpec((r, d), lambda l, j: (0, 0)),
            pl.BlockSpec((1, d, d), lambda l, j: (l, 0, j)),
            pl.BlockSpec((1, 1, d), lambda l, j: (l, 0, j)),
        ],
        out_specs=pl.BlockSpec((1, r, d), lambda l, j: (l, 0, j)),
        out_shape=jax.ShapeDtypeStruct((n_layers, r, d6), F32),
        compiler_params=_params(2),
        name="modulation",
    )(crows, w_mod, b_mod.reshape(n_layers, 1, d6))


def _head_rms(t, bd):
    ms = jnp.dot((t * t).astype(BF16), bd, preferred_element_type=F32)
    return t * lax.rsqrt(ms + EPS)


def _rope(t, cos, sin_signed, first_half):
    fwd = pltpu.roll(t, LANES - 16, axis=1)
    bwd = pltpu.roll(t, 16, axis=1)
    return t * cos + jnp.where(first_half, fwd, bwd) * sin_signed


def _inproj_kernel(x_ref, sh_ref, sc_ref, g_ref, w_ref, wvt_ref, qg_ref, kg_ref, bd_ref, cos_ref, sin_ref,
                   u_ref, naq_ref, nak_ref, gq_ref, gk_ref, navt_ref, gvt_ref, *, rope):
    h = _norm_modulate(x_ref[...], g_ref[...], sh_ref[0], sc_ref[0]).astype(BF16)

    def proj(lo, hi):
        return jnp.dot(h, w_ref[:, lo:hi], preferred_element_type=F32)

    q_scale = SCALE * LOG2E if rope else SCALE
    a = proj(_P_GLU, _P_NAQ)
    u_ref[...] = (a[:, :CONV_W] * jax.nn.sigmoid(a[:, CONV_W:])).astype(BF16)
    naq_ref[...] = (proj(_P_NAQ, _P_NAK) * q_scale).astype(BF16)
    nak_ref[...] = proj(_P_NAK, _P_GQ).astype(BF16)

    v_t = lax.dot_general(wvt_ref[...], h, (((1,), (1,)), ((), ())), preferred_element_type=F32).astype(BF16)
    for c in range(navt_ref.shape[1]):
        navt_ref[0, c] = v_t[:NA_W, c * KEY_CHUNK:(c + 1) * KEY_CHUNK]
        gvt_ref[0, c] = v_t[NA_W:, c * KEY_CHUNK:(c + 1) * KEY_CHUNK]

    bd = bd_ref[...]
    if rope:
        cos = cos_ref[...]
        sin = sin_ref[...]
        lane = lax.broadcasted_iota(jnp.int32, cos.shape, 1)
        first_half = (lane % 32) < 16

    def finish(t, gain):
        t = _head_rms(t, bd) * gain
        if rope:
            t = _rope(t, cos, sin, first_half)
        return t

    q = proj(_P_GQ, _P_GK)
    for j in range(GQA_Q_W // LANES):
        qj = finish(q[:, j * LANES:(j + 1) * LANES], qg_ref[...])
        gq_ref[:, j * LANES:(j + 1) * LANES] = (qj * q_scale).astype(BF16)
    gk_ref[...] = finish(proj(_P_GK, _P_END), kg_ref[...]).astype(BF16)


def _inproj(x2, mod3, norm_g, w, w_vt, qg2, kg2, bd, cos_t, sin_t, *, n_batch, seq, tm, rope):
    nt, d = x2.shape
    tps = seq // tm
    cpt = tm // KEY_CHUNK
    tok = lambda s, b: (b * tps + s, 0)
    outs = [CONV_W, NA_W, NA_W, GQA_Q_W, GQA_KV_W]
    vt_shape = lambda w_: jax.ShapeDtypeStruct((n_batch, seq // KEY_CHUNK, w_, KEY_CHUNK), BF16)
    vt_spec = lambda w_: pl.BlockSpec((1, cpt, w_, KEY_CHUNK), lambda s, b: (b, s, 0, 0))
    return pl.pallas_call(
        functools.partial(_inproj_kernel, rope=rope),
        grid=(tps, n_batch),
        in_specs=[
            pl.BlockSpec((tm, d), tok),
            pl.BlockSpec((1, 1, d), lambda s, b: (b * 6 + 0, 0, 0)),
            pl.BlockSpec((1, 1, d), lambda s, b: (b * 6 + 1, 0, 0)),
            _const_spec((1, d)),
            _const_spec(w.shape),
            _const_spec(w_vt.shape),
            _const_spec((1, LANES)),
            _const_spec((1, LANES)),
            _const_spec((LANES, LANES)),
            pl.BlockSpec((tm, LANES), lambda s, b: (s, 0)),
            pl.BlockSpec((tm, LANES), lambda s, b: (s, 0)),
        ],
        out_specs=[pl.BlockSpec((tm, n), tok) for n in outs] + [vt_spec(NA_W), vt_spec(GQA_KV_W)],
        out_shape=[jax.ShapeDtypeStruct((nt, n), BF16) for n in outs] + [vt_shape(NA_W), vt_shape(GQA_KV_W)],
        compiler_params=_params(2),
        name="inproj_rope" if rope else "inproj_ctx",
    )(x2, mod3, mod3, norm_g, w, w_vt, qg2, kg2, bd, cos_t, sin_t)


def _conv_kernel(prev_ref, cur_ref, next_ref, w_ref, b_ref, lg_ref, lb_ref, o_ref, win_ref, *, tiles_per_seq):
    i = pl.program_id(1)
    tl = cur_ref.shape[0]
    prev = jnp.where(i == 0, 0.0, prev_ref[...].astype(F32))
    nxt = jnp.where(i == tiles_per_seq - 1, 0.0, next_ref[...].astype(F32))
    win_ref[0:CONV_HALO, :] = prev
    win_ref[CONV_HALO:CONV_HALO + tl, :] = cur_ref[...].astype(F32)
    win_ref[CONV_HALO + tl:, :] = nxt
    base = CONV_HALO - CONV_K // 2
    acc = jnp.zeros((tl, CONV_W), F32)
    for k in range(CONV_K):
        acc = acc + win_ref[base + k:base + k + tl, :] * w_ref[k:k + 1, :]
    y = acc + b_ref[...]
    mu = jnp.mean(y, axis=-1, keepdims=True)
    yc = y - mu
    var = jnp.mean(yc * yc, axis=-1, keepdims=True)
    z = yc * lax.rsqrt(var + EPS) * lg_ref[...] + lb_ref[...]
    o_ref[...] = (z * jax.nn.sigmoid(z)).astype(BF16)


def _conv(u, conv_w, conv_b, ln_g, ln_b, *, n_batch, seq, tl):
    nt = u.shape[0]
    tps = seq // tl
    hb = tl // CONV_HALO
    last_hb = nt // CONV_HALO - 1
    return pl.pallas_call(
        functools.partial(_conv_kernel, tiles_per_seq=tps),
        grid=(n_batch, tps),
        in_specs=[
            pl.BlockSpec((CONV_HALO, CONV_W), lambda b, i: (jnp.maximum((b * tps + i) * hb - 1, 0), 0)),
            pl.BlockSpec((tl, CONV_W), lambda b, i: (b * tps + i, 0)),
            pl.BlockSpec((CONV_HALO, CONV_W), lambda b, i: (jnp.minimum((b * tps + i + 1) * hb, last_hb), 0)),
            _const_spec((CONV_K, CONV_W)),
            _const_spec((1, CONV_W)),
            _const_spec((1, CONV_W)),
            _const_spec((1, CONV_W)),
        ],
        out_specs=pl.BlockSpec((tl, CONV_W), lambda b, i: (b * tps + i, 0)),
        out_shape=jax.ShapeDtypeStruct((nt, CONV_W), BF16),
        scratch_shapes=[pltpu.VMEM((tl + 2 * CONV_HALO, CONV_W), F32)],
        compiler_params=_params(2),
        name="conv_ln_swish",
    )(u, u, u, conv_w, conv_b, ln_g, ln_b)


def _na_bias_tables(rpb, rows):
    cols = np.arange(GRID_W)
    win0 = np.clip(cols - NA_KW // 2, 0, GRID_W - NA_KW)
    col_ok = (cols[:, None] >= win0[None, :]) & (cols[:, None] < win0[None, :] + NA_KW)
    pad = GRID_W - NA_KW
    rp = jnp.pad(rpb * LOG2E, ((0, 0), (0, 0), (pad, pad)))
    tiles = jnp.stack([rp[:, :, GRID_W - 1 - qc:2 * GRID_W - 1 - qc] for qc in range(GRID_W)], axis=-1)
    tiles = jnp.where(jnp.asarray(col_ok)[None, None], tiles, NEG * LOG2E)
    masked = jnp.full(tiles.shape[:1] + tiles.shape[2:], NEG * LOG2E, F32)
    n_groups = rows // NA_QROWS
    variants = []
    for g in (0, min(1, n_groups - 1), n_groups - 1):
        w = int(np.clip(NA_QROWS * g - NA_KH // 2, 0, rows - NA_WROWS))
        blocks = []
        for kl in range(NA_WROWS):
            kr = w + kl
            row = []
            for ql in range(NA_QROWS):
                r = NA_QROWS * g + ql
                r0 = int(np.clip(r - NA_KH // 2, 0, rows - NA_KH))
                row.append(tiles[:, kr - r + NA_KH - 1] if r0 <= kr < r0 + NA_KH else masked)
            blocks.append(jnp.concatenate(row, axis=-1))
        variants.append(jnp.concatenate(blocks, axis=1))
    return jnp.stack(variants)


def _na_kernel(q_ref, k_ref, vt_ref, kc_ref, vct_ref, bias_ref, o_ref, s0_ref, s1_ref, ot_ref, *, n_chunks):
    g = pl.program_id(1)
    nq = NA_QROWS * GRID_W
    win_chunks = NA_WROWS * GRID_W // KEY_CHUNK
    c0 = jnp.clip(g - 1, 0, n_chunks - win_chunks)
    start = pl.multiple_of(c0 * KEY_CHUNK, KEY_CHUNK)
    n_loc = win_chunks * KEY_CHUNK
    q_t = q_ref[...].astype(F32).T.astype(BF16)
    zeros = jnp.zeros((HEAD_DIM, nq), BF16)
    s_refs = (s0_ref, s1_ref)

    def scores(h, s_ref):
        hp, j = divmod(h, LANES // HEAD_DIM)
        qh = q_t[h * HEAD_DIM:(h + 1) * HEAD_DIM]
        qz = jnp.concatenate([qh, zeros] if j == 0 else [zeros, qh], axis=0)
        kwin = k_ref[pl.ds(start, n_loc), hp * LANES:(hp + 1) * LANES]
        s_ref[0:n_loc, :] = jnp.dot(kwin, qz, preferred_element_type=F32) + bias_ref[0, h]
        s_ref[n_loc:, :] = jnp.dot(kc_ref[:, hp * LANES:(hp + 1) * LANES], qz, preferred_element_type=F32)

    def softmax_pv(h, s_ref):
        s = s_ref[...]
        p = jnp.exp2(s - jnp.max(s, axis=0, keepdims=True)).astype(BF16)
        rows_h = slice(h * HEAD_DIM, (h + 1) * HEAD_DIM)
        acc = jnp.dot(_with_ones(vct_ref[0, 0, rows_h, :]), p[n_loc:], preferred_element_type=F32)
        for t in range(win_chunks):
            acc = acc + jnp.dot(_with_ones(vt_ref[0, c0 + t, rows_h, :]), p[t * KEY_CHUNK:(t + 1) * KEY_CHUNK],
                                preferred_element_type=F32)
        ot_ref[rows_h, :] = acc[:HEAD_DIM] * (1.0 / acc[HEAD_DIM:HEAD_DIM + 1])

    scores(0, s_refs[0])
    for h in range(NA_HEADS):
        if h + 1 < NA_HEADS:
            scores(h + 1, s_refs[(h + 1) % 2])
        softmax_pv(h, s_refs[h % 2])
    o_ref[...] = ot_ref[...].T.astype(BF16)


def _na_attention(q, k, vt, kc, vct, bias, *, n_batch, seq, ctx_len):
    rows = seq // GRID_W
    n_groups = rows // NA_QROWS
    nq = NA_QROWS * GRID_W
    variant = lambda g: jnp.where(g == 0, 0, jnp.where(g == n_groups - 1, 2, 1))
    return pl.pallas_call(
        functools.partial(_na_kernel, n_chunks=seq // KEY_CHUNK),
        grid=(n_batch, n_groups),
        in_specs=[
            pl.BlockSpec((nq, NA_W), lambda b, g: (b * n_groups + g, 0)),
            pl.BlockSpec((seq, NA_W), lambda b, g: (b, 0)),
            pl.BlockSpec((1,) + vt.shape[1:], lambda b, g: (b, 0, 0, 0)),
            pl.BlockSpec((ctx_len, NA_W), lambda b, g: (b, 0)),
            pl.BlockSpec((1,) + vct.shape[1:], lambda b, g: (b, 0, 0, 0)),
            pl.BlockSpec((1,) + bias.shape[1:], lambda b, g: (variant(g), 0, 0, 0)),
        ],
        out_specs=pl.BlockSpec((nq, NA_W), lambda b, g: (b * n_groups + g, 0)),
        out_shape=jax.ShapeDtypeStruct(q.shape, BF16),
        scratch_shapes=[
            pltpu.VMEM((NA_WROWS * GRID_W + ctx_len, nq), F32),
            pltpu.VMEM((NA_WROWS * GRID_W + ctx_len, nq), F32),
            pltpu.VMEM((NA_W, nq), F32),
        ],
        compiler_params=_params(2),
        name="na_attention",
    )(q, k, vt, kc, vct, bias)


def _gqa_kernel(q_ref, k_ref, kc_ref, vt_ref, vct_ref, o_ref, qt_ref, s0_ref, s1_ref, m_ref, acc_ref, *, n_chunks):
    tq = q_ref.shape[0]
    q_t = q_ref[...].astype(F32).T
    for hk in range(GQA_KV_HEADS):
        for gi in range(GQA_GROUP):
            hq = hk * GQA_GROUP + gi
            qt_ref[hk, :, gi * tq:(gi + 1) * tq] = q_t[hq * HEAD_DIM:(hq + 1) * HEAD_DIM, :].astype(BF16)
    m_ref[...] = jnp.full(m_ref.shape, NEG, F32)
    acc_ref[...] = jnp.zeros(acc_ref.shape, F32)

    def scores(k_rows, s_ref):
        for hk in range(GQA_KV_HEADS):
            kch = k_rows[:, hk * HEAD_DIM:(hk + 1) * HEAD_DIM]
            s_ref[hk] = jnp.dot(kch, qt_ref[hk], preferred_element_type=F32)

    def softmax_pv(v_t, s_ref):
        for hk in range(GQA_KV_HEADS):
            s = s_ref[hk]
            m_old = m_ref[hk]
            m_new = jnp.maximum(m_old, jnp.max(s, axis=0, keepdims=True))
            p = jnp.exp2(s - m_new).astype(BF16)
            pv = jnp.dot(_with_ones(v_t[hk * HEAD_DIM:(hk + 1) * HEAD_DIM]), p, preferred_element_type=F32)
            acc_ref[hk] = acc_ref[hk] * jnp.exp2(m_old - m_new) + pv
            m_ref[hk] = m_new

    k_lat = lambda c: k_ref[pl.ds(pl.multiple_of(c * KEY_CHUNK, KEY_CHUNK), KEY_CHUNK), :]
    v_lat = lambda c: vt_ref[0, c]

    scores(k_lat(0), s0_ref)

    def pair(i, carry):
        c = 2 * i
        scores(k_lat(c + 1), s1_ref)
        softmax_pv(v_lat(c), s0_ref)
        scores(k_lat(c + 2), s0_ref)
        softmax_pv(v_lat(c + 1), s1_ref)
        return carry

    lax.fori_loop(0, n_chunks // 2 - 1, pair, 0)
    scores(k_lat(n_chunks - 1), s1_ref)
    softmax_pv(v_lat(n_chunks - 2), s0_ref)
    scores(kc_ref[...], s0_ref)
    softmax_pv(v_lat(n_chunks - 1), s1_ref)
    softmax_pv(vct_ref[0, 0], s0_ref)

    heads_t = []
    for hk in range(GQA_KV_HEADS):
        acc = acc_ref[hk]
        out_t = acc[:HEAD_DIM] * (1.0 / acc[HEAD_DIM:HEAD_DIM + 1])
        heads_t += [out_t[:, gi * tq:(gi + 1) * tq] for gi in range(GQA_GROUP)]
    o_ref[...] = jnp.concatenate(heads_t, axis=0).T.astype(BF16)


def _gqa_attention(q, k, kc, vt, vct, *, n_batch, sq, ctx_len, tq):
    tpq = sq // tq
    m = GQA_GROUP * tq
    n_chunks = sq // KEY_CHUNK
    assert n_chunks % 2 == 0 and n_chunks >= 2 and ctx_len == KEY_CHUNK
    return pl.pallas_call(
        functools.partial(_gqa_kernel, n_chunks=n_chunks),
        grid=(n_batch, tpq),
        in_specs=[
            pl.BlockSpec((tq, GQA_Q_W), lambda b, i: (b * tpq + i, 0)),
            pl.BlockSpec((sq, GQA_KV_W), lambda b, i: (b, 0)),
            pl.BlockSpec((ctx_len, GQA_KV_W), lambda b, i: (b, 0)),
            pl.BlockSpec((1,) + vt.shape[1:], lambda b, i: (b, 0, 0, 0)),
            pl.BlockSpec((1,) + vct.shape[1:], lambda b, i: (b, 0, 0, 0)),
        ],
        out_specs=pl.BlockSpec((tq, GQA_Q_W), lambda b, i: (b * tpq + i, 0)),
        out_shape=jax.ShapeDtypeStruct(q.shape, BF16),
        scratch_shapes=[
            pltpu.VMEM((GQA_KV_HEADS, HEAD_DIM, m), BF16),
            pltpu.VMEM((GQA_KV_HEADS, KEY_CHUNK, m), F32),
            pltpu.VMEM((GQA_KV_HEADS, KEY_CHUNK, m), F32),
            pltpu.VMEM((GQA_KV_HEADS, 1, m), F32),
            pltpu.VMEM((GQA_KV_HEADS, HEAD_DIM + ONES_ROWS, m), F32),
        ],
        compiler_params=_params(2),
        name="gqa_attention",
    )(q, k, kc, vt, vct)


def _ctx_attn_kernel(q_ref, k_ref, vt_ref, o_ref, ot_ref, *, n_kv, group):
    tq = q_ref.shape[0]
    q_t = q_ref[...].astype(F32).T.astype(BF16)
    for hk in range(n_kv):
        rows_k = slice(hk * HEAD_DIM, (hk + 1) * HEAD_DIM)
        qs = jnp.concatenate([q_t[(hk * group + gi) * HEAD_DIM:(hk * group + gi + 1) * HEAD_DIM]
                              for gi in range(group)], axis=1)
        s = jnp.dot(k_ref[:, rows_k], qs, preferred_element_type=F32)
        p = jnp.exp(s - jnp.max(s, axis=0, keepdims=True)).astype(BF16)
        acc = jnp.dot(_with_ones(vt_ref[0, 0, rows_k, :]), p, preferred_element_type=F32)
        out_t = acc[:HEAD_DIM] * (1.0 / acc[HEAD_DIM:HEAD_DIM + 1])
        for gi in range(group):
            hq = hk * group + gi
            ot_ref[hq * HEAD_DIM:(hq + 1) * HEAD_DIM, :] = out_t[:, gi * tq:(gi + 1) * tq]
    o_ref[...] = ot_ref[...].T.astype(BF16)


def _ctx_attention(q, k, vt, *, n_batch, ctx_len, n_kv, group):
    return pl.pallas_call(
        functools.partial(_ctx_attn_kernel, n_kv=n_kv, group=group),
        grid=(n_batch,),
        in_specs=[
            pl.BlockSpec((ctx_len, q.shape[1]), lambda b: (b, 0)),
            pl.BlockSpec((ctx_len, k.shape[1]), lambda b: (b, 0)),
            pl.BlockSpec((1,) + vt.shape[1:], lambda b: (b, 0, 0, 0)),
        ],
        out_specs=pl.BlockSpec((ctx_len, q.shape[1]), lambda b: (b, 0)),
        out_shape=jax.ShapeDtypeStruct(q.shape, BF16),
        scratch_shapes=[pltpu.VMEM((q.shape[1], ctx_len), F32)],
        compiler_params=_params(1),
        name=f"ctx_attention_g{group}",
    )(q, k, vt)


def _merge_kernel(x_ref, sh_ref, sc_ref, gate_ref, g_ref, wg_ref, a_ref, b_ref, c_ref, wa_ref, wb_ref, wc_ref,
                  wo_ref, o_ref):
    x = x_ref[...]
    d = x.shape[1]
    h = _norm_modulate(x, g_ref[...], sh_ref[0], sc_ref[0]).astype(BF16)
    y = None
    for j, (br_ref, w_ref) in enumerate(((a_ref, wa_ref), (b_ref, wb_ref), (c_ref, wc_ref))):
        gate = jax.nn.sigmoid(jnp.dot(h, wg_ref[:, j * d:(j + 1) * d], preferred_element_type=F32))
        t = gate * jnp.dot(br_ref[...], w_ref[...], preferred_element_type=F32)
        y = t if y is None else y + t
    o_ref[...] = x + gate_ref[0] * jnp.dot(y.astype(BF16), wo_ref[...], preferred_element_type=F32)


def _merge(x2, mod3, norm_g, w_gates, br_a, br_b, br_c, wa, wb, wc, wo, *, n_batch, seq, tm):
    nt, d = x2.shape
    tps = seq // tm
    tok = lambda i: (i, 0)
    mrow = lambda j: (lambda i: ((i // tps) * 6 + j, 0, 0))
    return pl.pallas_call(
        _merge_kernel,
        grid=(nt // tm,),
        in_specs=[
            pl.BlockSpec((tm, d), tok),
            pl.BlockSpec((1, 1, d), mrow(0)),
            pl.BlockSpec((1, 1, d), mrow(1)),
            pl.BlockSpec((1, 1, d), mrow(2)),
            _const_spec((1, d)),
            _const_spec(w_gates.shape),
            pl.BlockSpec((tm, CONV_W), tok),
            pl.BlockSpec((tm, NA_W), tok),
            pl.BlockSpec((tm, GQA_Q_W), tok),
            _const_spec(wa.shape),
            _const_spec(wb.shape),
            _const_spec(wc.shape),
            _const_spec(wo.shape),
        ],
        out_specs=pl.BlockSpec((tm, d), tok),
        out_shape=jax.ShapeDtypeStruct((nt, d), F32),
        compiler_params=_params(1),
        name="merge",
    )(x2, mod3, mod3, mod3, norm_g, w_gates, br_a, br_b, br_c, wa, wb, wc, wo)


def _ffn_kernel(x_ref, sh_ref, sc_ref, gate_ref, g_ref, wi_ref, wo_ref, fg_ref, o_ref, *, n_split, final_norm):
    x = x_ref[...]
    hidden = wo_ref.shape[0]
    h = _norm_modulate(x, g_ref[...], sh_ref[0], sc_ref[0]).astype(BF16)
    step = hidden // n_split
    y = None
    for j in range(n_split):
        gt = jnp.dot(h, wi_ref[:, j * step:(j + 1) * step], preferred_element_type=F32)
        up = jnp.dot(h, wi_ref[:, hidden + j * step:hidden + (j + 1) * step], preferred_element_type=F32)
        act = (gt * jax.nn.sigmoid(gt) * up).astype(BF16)
        t = jnp.dot(act, wo_ref[j * step:(j + 1) * step, :], preferred_element_type=F32)
        y = t if y is None else y + t
    out = x + gate_ref[0] * y
    if final_norm:
        ms = jnp.mean(out * out, axis=-1, keepdims=True)
        out = out * lax.rsqrt(ms + EPS) * fg_ref[...]
    o_ref[...] = out


def _ffn(x2, mod3, norm_g, w_in, w_out, final_g, *, n_batch, seq, tm, final_norm):
    nt, d = x2.shape
    tps = seq // tm
    tok = lambda i: (i, 0)
    mrow = lambda j: (lambda i: ((i // tps) * 6 + j, 0, 0))
    return pl.pallas_call(
        functools.partial(_ffn_kernel, n_split=2, final_norm=final_norm),
        grid=(nt // tm,),
        in_specs=[
            pl.BlockSpec((tm, d), tok),
            pl.BlockSpec((1, 1, d), mrow(3)),
            pl.BlockSpec((1, 1, d), mrow(4)),
            pl.BlockSpec((1, 1, d), mrow(5)),
            _const_spec((1, d)),
            _const_spec(w_in.shape),
            _const_spec(w_out.shape),
            _const_spec((1, d)),
        ],
        out_specs=pl.BlockSpec((tm, d), tok),
        out_shape=jax.ShapeDtypeStruct((nt, d), F32),
        compiler_params=_params(1),
        name="ffn_final" if final_norm else "ffn",
    )(x2, mod3, mod3, mod3, norm_g, w_in, w_out, final_g)


def _rope_tables(seq):
    t = jnp.arange(seq)
    pos_row = (t // GRID_W).astype(F32)
    pos_col = (t % GRID_W).astype(F32)
    half = HEAD_DIM // 2
    freqs = jnp.power(ROPE_THETA, -jnp.arange(0, half, 2, dtype=F32) / half)

    def part(pos):
        ang = pos[:, None] * freqs
        c, s = jnp.cos(ang), jnp.sin(ang)
        return jnp.concatenate([c, c], axis=-1), jnp.concatenate([-s, s], axis=-1)

    cr, sr = part(pos_row)
    cc, sc = part(pos_col)
    cos = jnp.concatenate([cr, cc], axis=-1)
    sin = jnp.concatenate([sr, sc], axis=-1)
    return jnp.tile(cos, (1, LANES // HEAD_DIM)), jnp.tile(sin, (1, LANES // HEAD_DIM))


def kernel(x, c, ctx, c_ctx, w_mod, b_mod, norm1_g, norm2_g, w_in, conv_w, conv_b, conv_ln_g, conv_ln_b,
           w_conv_out, na_rpb, w_na_out, q_norm_g, k_norm_g, w_gqa_out, w_out, w_ffn_in, w_ffn_out, final_g):
    n_batch, seq, d = x.shape
    ctx_len = ctx.shape[1]
    depth = w_mod.shape[0]
    rows = seq // GRID_W
    assert seq % (NA_QROWS * GRID_W) == 0 and rows >= NA_WROWS and ctx_len == KEY_CHUNK
    tm = 512
    tmc = ctx_len

    n_rows = -(-(n_batch + 1) // 8) * 8
    crows = jnp.zeros((n_rows, d), F32).at[:n_batch].set(c).at[n_batch].set(c_ctx)
    mod = _modulation(crows, w_mod, b_mod)

    cos_t, sin_t = _rope_tables(seq)
    bd = jnp.asarray(np.kron(np.eye(LANES // HEAD_DIM), np.full((HEAD_DIM, HEAD_DIM), 1.0 / HEAD_DIM)), BF16)
    tile2 = lambda g: jnp.tile(g, LANES // HEAD_DIM).reshape(1, LANES)
    row = lambda g: g.reshape(1, -1)

    xl = x.reshape(n_batch * seq, d)
    xc = ctx.reshape(n_batch * ctx_len, d)
    for l in range(depth):
        last = l == depth - 1
        wl = w_in[l]
        w_proj = jnp.concatenate([wl[:, _IN_GLU:_IN_NAV], wl[:, _IN_GQ:_IN_GV]], axis=1).astype(BF16)
        w_vt = jnp.concatenate([wl[:, _IN_NAV:_IN_GQ], wl[:, _IN_GV:_IN_GATES]], axis=1).T.astype(BF16)
        w_gates = wl[:, _IN_GATES:].astype(BF16)
        mod_l = mod[l, :n_batch].reshape(n_batch * 6, 1, d)
        mod_c = jnp.broadcast_to(mod[l, n_batch], (n_batch, 6 * d)).reshape(n_batch * 6, 1, d)
        qg2, kg2 = tile2(q_norm_g[l]), tile2(k_norm_g[l])
        g1, g2 = row(norm1_g[l]), row(norm2_g[l])
        wa, wb, wc = w_conv_out[l].astype(BF16), w_na_out[l].astype(BF16), w_gqa_out[l].astype(BF16)
        wo = w_out[l].astype(BF16)
        wfi, wfo = w_ffn_in[l].astype(BF16), w_ffn_out[l].astype(BF16)
        conv_args = (conv_w[l], row(conv_b[l]), row(conv_ln_g[l]), row(conv_ln_b[l]))

        u, naq, nak, gq, gk, navt, gvt = _inproj(xl, mod_l, g1, w_proj, w_vt, qg2, kg2, bd, cos_t, sin_t,
                                                 n_batch=n_batch, seq=seq, tm=tm, rope=True)
        uc, naqc, nakc, gqc, gkc, navct, gvct = _inproj(xc, mod_c, g1, w_proj, w_vt, qg2, kg2, bd,
                                                        cos_t[:ctx_len], sin_t[:ctx_len],
                                                        n_batch=n_batch, seq=ctx_len, tm=tmc, rope=False)

        br_a = _conv(u, *conv_args, n_batch=n_batch, seq=seq, tl=tm)
        bias = _na_bias_tables(na_rpb[l], rows)
        br_b = _na_attention(naq, nak, navt, nakc, navct, bias, n_batch=n_batch, seq=seq, ctx_len=ctx_len)
        br_c = _gqa_attention(gq, gk, gkc, gvt, gvct, n_batch=n_batch, sq=seq, ctx_len=ctx_len, tq=128)
        xl = _merge(xl, mod_l, g1, w_gates, br_a, br_b, br_c, wa, wb, wc, wo, n_batch=n_batch, seq=seq, tm=tm)
        xl = _ffn(xl, mod_l, g2, wfi, wfo, row(final_g), n_batch=n_batch, seq=seq, tm=tm, final_norm=last)

        if not last:
            cbr_a = _conv(uc, *conv_args, n_batch=n_batch, seq=ctx_len, tl=tmc)
            cbr_b = _ctx_attention(naqc, nakc, navct, n_batch=n_batch, ctx_len=ctx_len, n_kv=NA_HEADS, group=1)
            cbr_c = _ctx_attention(gqc, gkc, gvct, n_batch=n_batch, ctx_len=ctx_len, n_kv=GQA_KV_HEADS,
                                   group=GQA_GROUP)
            xc = _merge(xc, mod_c, g1, w_gates, cbr_a, cbr_b, cbr_c, wa, wb, wc, wo,
                        n_batch=n_batch, seq=ctx_len, tm=tmc)
            xc = _ffn(xc, mod_c, g2, wfi, wfo, row(final_g), n_batch=n_batch, seq=ctx_len, tm=tmc,
                      final_norm=False)
    return xl.reshape(n_batch, seq, d)
```

```python
import functools

import numpy as np
import jax
import jax.numpy as jnp
from jax import lax
from jax.experimental import pallas as pl
from jax.experimental.pallas import tpu as pltpu

F32 = jnp.float32
BF16 = jnp.bfloat16

GRID_W = 64
HEAD_DIM = 64
CONV_W = 512
CONV_K = 31
CONV_HALO = 16
NA_HEADS = 8
NA_W = NA_HEADS * HEAD_DIM
NA_KH = 8
NA_KW = 16
NA_QROWS = 4
NA_WROWS = 12
GQA_Q_HEADS = 8
GQA_KV_HEADS = 2
GQA_GROUP = GQA_Q_HEADS // GQA_KV_HEADS
GQA_Q_W = GQA_Q_HEADS * HEAD_DIM
GQA_KV_W = GQA_KV_HEADS * HEAD_DIM
GQA_UNROLL = 4
ROPE_THETA = 10000.0
EPS = 1e-6
NEG = -1e30
SCALE = HEAD_DIM ** -0.5
LOG2E = 1.4426950408889634
LANES = 128
SUBLANES = 8
KEY_CHUNK = 256
ONES_ROWS = 16
VMEM_LIMIT = 56 * 1024 * 1024

_IN_GLU = 0
_IN_NAQ = _IN_GLU + 2 * CONV_W
_IN_NAK = _IN_NAQ + NA_W
_IN_NAV = _IN_NAK + NA_W
_IN_GQ = _IN_NAV + NA_W
_IN_GK = _IN_GQ + GQA_Q_W
_IN_GV = _IN_GK + GQA_KV_W
_IN_GATES = _IN_GV + GQA_KV_W
_P_GLU = 0
_P_NAQ = _P_GLU + 2 * CONV_W
_P_NAK = _P_NAQ + NA_W
_P_GQ = _P_NAK + NA_W
_P_GK = _P_GQ + GQA_Q_W
_P_END = _P_GK + GQA_KV_W


def _params(n_axes):
    return pltpu.CompilerParams(dimension_semantics=("arbitrary",) * n_axes, vmem_limit_bytes=VMEM_LIMIT)


def _const_spec(shape):
    zeros = (0,) * len(shape)
    return pl.BlockSpec(shape, lambda *_: zeros, pipeline_mode=pl.Buffered(1))


def _norm_modulate(x, g, shift, scale):
    ms = jnp.mean(x * x, axis=-1, keepdims=True)
    h = x * lax.rsqrt(ms + EPS) * g
    return h * (1.0 + scale) + shift


def _with_ones(v_t):
    return jnp.concatenate([v_t, jnp.ones((ONES_ROWS, v_t.shape[1]), v_t.dtype)], axis=0)


def _mod_kernel(c_ref, w_ref, b_ref, o_ref):
    c = c_ref[...]
    s = c * jax.nn.sigmoid(c)
    o_ref[0] = jnp.dot(s, w_ref[0], preferred_element_type=F32, precision=lax.Precision.HIGHEST) + b_ref[0]


def _modulation(crows, w_mod, b_mod):
    n_layers, d, d6 = w_mod.shape
    r = crows.shape[0]
    nb = d6 // d
    return pl.pallas_call(
        _mod_kernel,
        grid=(n_layers, nb),
        in_specs=[
            pl.BlockSpec((r, d), lambda l, j: (0, 0)),
            pl.BlockSpec((1, d, d), lambda l, j: (l, 0, j)),
            pl.BlockSpec((1, 1, d), lambda l, j: (l, 0, j)),
        ],
        out_specs=pl.BlockSpec((1, r, d), lambda l, j: (l, 0, j)),
        out_shape=jax.ShapeDtypeStruct((n_layers, r, d6), F32),
        compiler_params=_params(2),
        name="modulation",
    )(crows, w_mod, b_mod.reshape(n_layers, 1, d6))


def _head_rms(t, bd):
    ms = jnp.dot((t * t).astype(BF16), bd, preferred_element_type=F32)
    return t * lax.rsqrt(ms + EPS)


def _rope(t, cos, sin_signed, first_half):
    fwd = pltpu.roll(t, LANES - 16, axis=1)
    bwd = pltpu.roll(t, 16, axis=1)
    return t * cos + jnp.where(first_half, fwd, bwd) * sin_signed


def _inproj_kernel(x_ref, sh_ref, sc_ref, g_ref, w_ref, wvt_ref, qg_ref, kg_ref, bd_ref, cos_ref, sin_ref,
                   u_ref, naq_ref, nak_ref, gq_ref, gk_ref, navt_ref, gvt_ref, *, rope):
    h = _norm_modulate(x_ref[...], g_ref[...], sh_ref[0], sc_ref[0]).astype(BF16)

    def proj(lo, hi):
        return jnp.dot(h, w_ref[:, lo:hi], preferred_element_type=F32)

    q_scale = SCALE * LOG2E if rope else SCALE
    a = proj(_P_GLU, _P_NAQ)
    u_ref[...] = (a[:, :CONV_W] * jax.nn.sigmoid(a[:, CONV_W:])).astype(BF16)
    naq_ref[...] = (proj(_P_NAQ, _P_NAK) * q_scale).astype(BF16)
    nak_ref[...] = proj(_P_NAK, _P_GQ).astype(BF16)

    v_t = lax.dot_general(wvt_ref[...], h, (((1,), (1,)), ((), ())), preferred_element_type=F32).astype(BF16)
    for c in range(navt_ref.shape[1]):
        navt_ref[0, c] = v_t[:NA_W, c * KEY_CHUNK:(c + 1) * KEY_CHUNK]
        gvt_ref[0, c] = v_t[NA_W:, c * KEY_CHUNK:(c + 1) * KEY_CHUNK]

    bd = bd_ref[...]
    if rope:
        cos = cos_ref[...]
        sin = sin_ref[...]
        lane = lax.broadcasted_iota(jnp.int32, cos.shape, 1)
        first_half = (lane % 32) < 16

    def finish(t, gain):
        t = _head_rms(t, bd) * gain
        if rope:
            t = _rope(t, cos, sin, first_half)
        return t

    q = proj(_P_GQ, _P_GK)
    for j in range(GQA_Q_W // LANES):
        qj = finish(q[:, j * LANES:(j + 1) * LANES], qg_ref[...])
        gq_ref[:, j * LANES:(j + 1) * LANES] = (qj * q_scale).astype(BF16)
    gk_ref[...] = finish(proj(_P_GK, _P_END), kg_ref[...]).astype(BF16)


def _inproj(x2, mod3, norm_g, w, w_vt, qg2, kg2, bd, cos_t, sin_t, *, n_batch, seq, tm, rope):
    nt, d = x2.shape
    tps = seq // tm
    cpt = tm // KEY_CHUNK
    tok = lambda s, b: (b * tps + s, 0)
    outs = [CONV_W, NA_W, NA_W, GQA_Q_W, GQA_KV_W]
    vt_shape = lambda w_: jax.ShapeDtypeStruct((n_batch, seq // KEY_CHUNK, w_, KEY_CHUNK), BF16)
    vt_spec = lambda w_: pl.BlockSpec((1, cpt, w_, KEY_CHUNK), lambda s, b: (b, s, 0, 0))
    return pl.pallas_call(
        functools.partial(_inproj_kernel, rope=rope),
        grid=(tps, n_batch),
        in_specs=[
            pl.BlockSpec((tm, d), tok),
            pl.BlockSpec((1, 1, d), lambda s, b: (b * 6 + 0, 0, 0)),
            pl.BlockSpec((1, 1, d), lambda s, b: (b * 6 + 1, 0, 0)),
            _const_spec((1, d)),
            _const_spec(w.shape),
            _const_spec(w_vt.shape),
            _const_spec((1, LANES)),
            _const_spec((1, LANES)),
            _const_spec((LANES, LANES)),
            pl.BlockSpec((tm, LANES), lambda s, b: (s, 0)),
            pl.BlockSpec((tm, LANES), lambda s, b: (s, 0)),
        ],
        out_specs=[pl.BlockSpec((tm, n), tok) for n in outs] + [vt_spec(NA_W), vt_spec(GQA_KV_W)],
        out_shape=[jax.ShapeDtypeStruct((nt, n), BF16) for n in outs] + [vt_shape(NA_W), vt_shape(GQA_KV_W)],
        compiler_params=_params(2),
        name="inproj_rope" if rope else "inproj_ctx",
    )(x2, mod3, mod3, norm_g, w, w_vt, qg2, kg2, bd, cos_t, sin_t)


def _conv_kernel(prev_ref, cur_ref, next_ref, w_ref, b_ref, lg_ref, lb_ref, o_ref, win_ref, shift_ref, *,
                 tiles_per_seq):
    i = pl.program_id(1)
    tl = cur_ref.shape[0]
    prev = jnp.where(i == 0, 0.0, prev_ref[...].astype(F32))
    nxt = jnp.where(i == tiles_per_seq - 1, 0.0, next_ref[...].astype(F32))
    win_ref[0:CONV_HALO, :] = prev
    win_ref[CONV_HALO:CONV_HALO + tl, :] = cur_ref[...].astype(F32)
    win_ref[CONV_HALO + tl:, :] = nxt
    base = CONV_HALO - CONV_K // 2
    acc = jnp.zeros((tl, CONV_W), F32)
    for r in range(SUBLANES):
        taps = range(r, CONV_K, SUBLANES)
        n_shift = tl + SUBLANES * (len(taps) - 1)
        shift_ref[r % 2, 0:n_shift, :] = win_ref[base + r:base + r + n_shift, :]
        for i, k in enumerate(taps):
            acc = acc + shift_ref[r % 2, SUBLANES * i:SUBLANES * i + tl, :] * w_ref[k:k + 1, :]
    y = acc + b_ref[...]
    mu = jnp.mean(y, axis=-1, keepdims=True)
    yc = y - mu
    var = jnp.mean(yc * yc, axis=-1, keepdims=True)
    z = yc * lax.rsqrt(var + EPS) * lg_ref[...] + lb_ref[...]
    o_ref[...] = (z * jax.nn.sigmoid(z)).astype(BF16)


def _conv(u, conv_w, conv_b, ln_g, ln_b, *, n_batch, seq, tl):
    nt = u.shape[0]
    tps = seq // tl
    hb = tl // CONV_HALO
    last_hb = nt // CONV_HALO - 1
    return pl.pallas_call(
        functools.partial(_conv_kernel, tiles_per_seq=tps),
        grid=(n_batch, tps),
        in_specs=[
            pl.BlockSpec((CONV_HALO, CONV_W), lambda b, i: (jnp.maximum((b * tps + i) * hb - 1, 0), 0)),
            pl.BlockSpec((tl, CONV_W), lambda b, i: (b * tps + i, 0)),
            pl.BlockSpec((CONV_HALO, CONV_W), lambda b, i: (jnp.minimum((b * tps + i + 1) * hb, last_hb), 0)),
            _const_spec((CONV_K, CONV_W)),
            _const_spec((1, CONV_W)),
            _const_spec((1, CONV_W)),
            _const_spec((1, CONV_W)),
        ],
        out_specs=pl.BlockSpec((tl, CONV_W), lambda b, i: (b * tps + i, 0)),
        out_shape=jax.ShapeDtypeStruct((nt, CONV_W), BF16),
        scratch_shapes=[pltpu.VMEM((tl + 2 * CONV_HALO, CONV_W), F32),
                        pltpu.VMEM((2, tl + 2 * CONV_HALO, CONV_W), F32)],
        compiler_params=_params(2),
        name="conv_ln_swish",
    )(u, u, u, conv_w, conv_b, ln_g, ln_b)


def _na_bias_tables(rpb, rows):
    cols = np.arange(GRID_W)
    win0 = np.clip(cols - NA_KW // 2, 0, GRID_W - NA_KW)
    col_ok = (cols[:, None] >= win0[None, :]) & (cols[:, None] < win0[None, :] + NA_KW)
    pad = GRID_W - NA_KW
    rp = jnp.pad(rpb * LOG2E, ((0, 0), (0, 0), (pad, pad)))
    tiles = jnp.stack([rp[:, :, GRID_W - 1 - qc:2 * GRID_W - 1 - qc] for qc in range(GRID_W)], axis=-1)
    tiles = jnp.where(jnp.asarray(col_ok)[None, None], tiles, NEG * LOG2E)
    masked = jnp.full(tiles.shape[:1] + tiles.shape[2:], NEG * LOG2E, F32)
    n_groups = rows // NA_QROWS
    variants = []
    for g in (0, min(1, n_groups - 1), n_groups - 1):
        w = int(np.clip(NA_QROWS * g - NA_KH // 2, 0, rows - NA_WROWS))
        blocks = []
        for kl in range(NA_WROWS):
            kr = w + kl
            row = []
            for ql in range(NA_QROWS):
                r = NA_QROWS * g + ql
                r0 = int(np.clip(r - NA_KH // 2, 0, rows - NA_KH))
                row.append(tiles[:, kr - r + NA_KH - 1] if r0 <= kr < r0 + NA_KH else masked)
            blocks.append(jnp.concatenate(row, axis=-1))
        variants.append(jnp.concatenate(blocks, axis=1))
    return jnp.stack(variants)


def _na_kernel(q_ref, k_ref, vt_ref, kc_ref, vct_ref, bias_ref, o_ref, s0_ref, s1_ref, ot_ref, *, n_chunks):
    g = pl.program_id(1)
    nq = NA_QROWS * GRID_W
    win_chunks = NA_WROWS * GRID_W // KEY_CHUNK
    c0 = jnp.clip(g - 1, 0, n_chunks - win_chunks)
    start = pl.multiple_of(c0 * KEY_CHUNK, KEY_CHUNK)
    n_loc = win_chunks * KEY_CHUNK
    q_t = q_ref[...].astype(F32).T.astype(BF16)
    zeros = jnp.zeros((HEAD_DIM, nq), BF16)
    s_refs = (s0_ref, s1_ref)

    def scores(h, s_ref):
        hp, j = divmod(h, LANES // HEAD_DIM)
        qh = q_t[h * HEAD_DIM:(h + 1) * HEAD_DIM]
        qz = jnp.concatenate([qh, zeros] if j == 0 else [zeros, qh], axis=0)
        kwin = k_ref[pl.ds(start, n_loc), hp * LANES:(hp + 1) * LANES]
        s_ref[0:n_loc, :] = jnp.dot(kwin, qz, preferred_element_type=F32) + bias_ref[0, h]
        s_ref[n_loc:, :] = jnp.dot(kc_ref[:, hp * LANES:(hp + 1) * LANES], qz, preferred_element_type=F32)

    def softmax_pv(h, s_ref):
        s = s_ref[...]
        p = jnp.exp2(s - jnp.max(s, axis=0, keepdims=True)).astype(BF16)
        rows_h = slice(h * HEAD_DIM, (h + 1) * HEAD_DIM)
        acc = jnp.dot(_with_ones(vct_ref[0, 0, rows_h, :]), p[n_loc:], preferred_element_type=F32)
        for t in range(win_chunks):
            acc = acc + jnp.dot(_with_ones(vt_ref[0, c0 + t, rows_h, :]), p[t * KEY_CHUNK:(t + 1) * KEY_CHUNK],
                                preferred_element_type=F32)
        ot_ref[rows_h, :] = acc[:HEAD_DIM] * (1.0 / acc[HEAD_DIM:HEAD_DIM + 1])

    scores(0, s_refs[0])
    for h in range(NA_HEADS):
        if h + 1 < NA_HEADS:
            scores(h + 1, s_refs[(h + 1) % 2])
        softmax_pv(h, s_refs[h % 2])
    o_ref[...] = ot_ref[...].T.astype(BF16)


def _na_attention(q, k, vt, kc, vct, bias, *, n_batch, seq, ctx_len):
    rows = seq // GRID_W
    n_groups = rows // NA_QROWS
    nq = NA_QROWS * GRID_W
    variant = lambda g: jnp.where(g == 0, 0, jnp.where(g == n_groups - 1, 2, 1))
    return pl.pallas_call(
        functools.partial(_na_kernel, n_chunks=seq // KEY_CHUNK),
        grid=(n_batch, n_groups),
        in_specs=[
            pl.BlockSpec((nq, NA_W), lambda b, g: (b * n_groups + g, 0)),
            pl.BlockSpec((seq, NA_W), lambda b, g: (b, 0)),
            pl.BlockSpec((1,) + vt.shape[1:], lambda b, g: (b, 0, 0, 0)),
            pl.BlockSpec((ctx_len, NA_W), lambda b, g: (b, 0)),
            pl.BlockSpec((1,) + vct.shape[1:], lambda b, g: (b, 0, 0, 0)),
            pl.BlockSpec((1,) + bias.shape[1:], lambda b, g: (variant(g), 0, 0, 0)),
        ],
        out_specs=pl.BlockSpec((nq, NA_W), lambda b, g: (b * n_groups + g, 0)),
        out_shape=jax.ShapeDtypeStruct(q.shape, BF16),
        scratch_shapes=[
            pltpu.VMEM((NA_WROWS * GRID_W + ctx_len, nq), F32),
            pltpu.VMEM((NA_WROWS * GRID_W + ctx_len, nq), F32),
            pltpu.VMEM((NA_W, nq), F32),
        ],
        compiler_params=_params(2),
        name="na_attention",
    )(q, k, vt, kc, vct, bias)


def _gqa_kernel(q_ref, k_ref, kc_ref, vt_ref, vct_ref, o_ref, qt_ref, s0_ref, s1_ref, m_ref, acc_ref, *, n_chunks):
    tq = q_ref.shape[0]
    q_t = q_ref[...].astype(F32).T
    for hk in range(GQA_KV_HEADS):
        for gi in range(GQA_GROUP):
            hq = hk * GQA_GROUP + gi
            qt_ref[hk, :, gi * tq:(gi + 1) * tq] = q_t[hq * HEAD_DIM:(hq + 1) * HEAD_DIM, :].astype(BF16)
    m_ref[...] = jnp.full(m_ref.shape, NEG, F32)
    acc_ref[...] = jnp.zeros(acc_ref.shape, F32)

    def scores(k_rows, s_ref):
        for hk in range(GQA_KV_HEADS):
            kch = k_rows[:, hk * HEAD_DIM:(hk + 1) * HEAD_DIM]
            s_ref[hk] = jnp.dot(kch, qt_ref[hk], preferred_element_type=F32)

    def softmax_pv(v_t, s_ref):
        for hk in range(GQA_KV_HEADS):
            s = s_ref[hk]
            m_old = m_ref[hk]
            m_new = jnp.maximum(m_old, jnp.max(s, axis=0, keepdims=True))
            p = jnp.exp2(s - m_new).astype(BF16)
            pv = jnp.dot(_with_ones(v_t[hk * HEAD_DIM:(hk + 1) * HEAD_DIM]), p, preferred_element_type=F32)
            acc_ref[hk] = acc_ref[hk] * jnp.exp2(m_old - m_new) + pv
            m_ref[hk] = m_new

    k_lat = lambda c: k_ref[pl.ds(pl.multiple_of(c * KEY_CHUNK, KEY_CHUNK), KEY_CHUNK), :]
    v_lat = lambda c: vt_ref[0, c]

    s_refs = (s0_ref, s1_ref)
    scores(k_lat(0), s_refs[0])

    def steps(c0, n):
        for j in range(n):
            c = c0 + j
            nxt_is_ctx = isinstance(c, int) and c + 1 == n_chunks
            scores(kc_ref[...] if nxt_is_ctx else k_lat(c + 1), s_refs[(j + 1) % 2])
            softmax_pv(v_lat(c), s_refs[j % 2])

    trips = (n_chunks - 1) // GQA_UNROLL

    def body(i, carry):
        steps(i * GQA_UNROLL, GQA_UNROLL)
        return carry

    lax.fori_loop(0, trips, body, 0)
    steps(trips * GQA_UNROLL, n_chunks - trips * GQA_UNROLL)
    softmax_pv(vct_ref[0, 0], s_refs[n_chunks % 2])

    heads_t = []
    for hk in range(GQA_KV_HEADS):
        acc = acc_ref[hk]
        out_t = acc[:HEAD_DIM] * (1.0 / acc[HEAD_DIM:HEAD_DIM + 1])
        heads_t += [out_t[:, gi * tq:(gi + 1) * tq] for gi in range(GQA_GROUP)]
    o_ref[...] = jnp.concatenate(heads_t, axis=0).T.astype(BF16)


def _gqa_attention(q, k, kc, vt, vct, *, n_batch, sq, ctx_len, tq):
    tpq = sq // tq
    m = GQA_GROUP * tq
    n_chunks = sq // KEY_CHUNK
    assert n_chunks % 2 == 0 and n_chunks >= 2 and ctx_len == KEY_CHUNK
    return pl.pallas_call(
        functools.partial(_gqa_kernel, n_chunks=n_chunks),
        grid=(n_batch, tpq),
        in_specs=[
            pl.BlockSpec((tq, GQA_Q_W), lambda b, i: (b * tpq + i, 0)),
            pl.BlockSpec((sq, GQA_KV_W), lambda b, i: (b, 0)),
            pl.BlockSpec((ctx_len, GQA_KV_W), lambda b, i: (b, 0)),
            pl.BlockSpec((1,) + vt.shape[1:], lambda b, i: (b, 0, 0, 0)),
            pl.BlockSpec((1,) + vct.shape[1:], lambda b, i: (b, 0, 0, 0)),
        ],
        out_specs=pl.BlockSpec((tq, GQA_Q_W), lambda b, i: (b * tpq + i, 0)),
        out_shape=jax.ShapeDtypeStruct(q.shape, BF16),
        scratch_shapes=[
            pltpu.VMEM((GQA_KV_HEADS, HEAD_DIM, m), BF16),
            pltpu.VMEM((GQA_KV_HEADS, KEY_CHUNK, m), F32),
            pltpu.VMEM((GQA_KV_HEADS, KEY_CHUNK, m), F32),
            pltpu.VMEM((GQA_KV_HEADS, 1, m), F32),
            pltpu.VMEM((GQA_KV_HEADS, HEAD_DIM + ONES_ROWS, m), F32),
        ],
        compiler_params=_params(2),
        name="gqa_attention",
    )(q, k, kc, vt, vct)


def _ctx_attn_kernel(q_ref, k_ref, vt_ref, o_ref, ot_ref, *, n_kv, group):
    tq = q_ref.shape[0]
    q_t = q_ref[...].astype(F32).T.astype(BF16)
    for hk in range(n_kv):
        rows_k = slice(hk * HEAD_DIM, (hk + 1) * HEAD_DIM)
        qs = jnp.concatenate([q_t[(hk * group + gi) * HEAD_DIM:(hk * group + gi + 1) * HEAD_DIM]
                              for gi in range(group)], axis=1)
        s = jnp.dot(k_ref[:, rows_k], qs, preferred_element_type=F32)
        p = jnp.exp(s - jnp.max(s, axis=0, keepdims=True)).astype(BF16)
        acc = jnp.dot(_with_ones(vt_ref[0, 0, rows_k, :]), p, preferred_element_type=F32)
        out_t = acc[:HEAD_DIM] * (1.0 / acc[HEAD_DIM:HEAD_DIM + 1])
        for gi in range(group):
            hq = hk * group + gi
            ot_ref[hq * HEAD_DIM:(hq + 1) * HEAD_DIM, :] = out_t[:, gi * tq:(gi + 1) * tq]
    o_ref[...] = ot_ref[...].T.astype(BF16)


def _ctx_attention(q, k, vt, *, n_batch, ctx_len, n_kv, group):
    return pl.pallas_call(
        functools.partial(_ctx_attn_kernel, n_kv=n_kv, group=group),
        grid=(n_batch,),
        in_specs=[
            pl.BlockSpec((ctx_len, q.shape[1]), lambda b: (b, 0)),
            pl.BlockSpec((ctx_len, k.shape[1]), lambda b: (b, 0)),
            pl.BlockSpec((1,) + vt.shape[1:], lambda b: (b, 0, 0, 0)),
        ],
        out_specs=pl.BlockSpec((ctx_len, q.shape[1]), lambda b: (b, 0)),
        out_shape=jax.ShapeDtypeStruct(q.shape, BF16),
        scratch_shapes=[pltpu.VMEM((q.shape[1], ctx_len), F32)],
        compiler_params=_params(1),
        name=f"ctx_attention_g{group}",
    )(q, k, vt)


def _merge_kernel(x_ref, sh_ref, sc_ref, gate_ref, g_ref, wg_ref, a_ref, b_ref, c_ref, wa_ref, wb_ref, wc_ref,
                  wo_ref, o_ref):
    x = x_ref[...]
    d = x.shape[1]
    h = _norm_modulate(x, g_ref[...], sh_ref[0], sc_ref[0]).astype(BF16)
    y = None
    for j, (br_ref, w_ref) in enumerate(((a_ref, wa_ref), (b_ref, wb_ref), (c_ref, wc_ref))):
        gate = jax.nn.sigmoid(jnp.dot(h, wg_ref[:, j * d:(j + 1) * d], preferred_element_type=F32))
        t = gate * jnp.dot(br_ref[...], w_ref[...], preferred_element_type=F32)
        y = t if y is None else y + t
    o_ref[...] = x + gate_ref[0] * jnp.dot(y.astype(BF16), wo_ref[...], preferred_element_type=F32)


def _merge(x2, mod3, norm_g, w_gates, br_a, br_b, br_c, wa, wb, wc, wo, *, n_batch, seq, tm):
    nt, d = x2.shape
    tps = seq // tm
    tok = lambda i: (i, 0)
    mrow = lambda j: (lambda i: ((i // tps) * 6 + j, 0, 0))
    return pl.pallas_call(
        _merge_kernel,
        grid=(nt // tm,),
        in_specs=[
            pl.BlockSpec((tm, d), tok),
            pl.BlockSpec((1, 1, d), mrow(0)),
            pl.BlockSpec((1, 1, d), mrow(1)),
            pl.BlockSpec((1, 1, d), mrow(2)),
            _const_spec((1, d)),
            _const_spec(w_gates.shape),
            pl.BlockSpec((tm, CONV_W), tok),
            pl.BlockSpec((tm, NA_W), tok),
            pl.BlockSpec((tm, GQA_Q_W), tok),
            _const_spec(wa.shape),
            _const_spec(wb.shape),
            _const_spec(wc.shape),
            _const_spec(wo.shape),
        ],
        out_specs=pl.BlockSpec((tm, d), tok),
        out_shape=jax.ShapeDtypeStruct((nt, d), F32),
        compiler_params=_params(1),
        name="merge",
    )(x2, mod3, mod3, mod3, norm_g, w_gates, br_a, br_b, br_c, wa, wb, wc, wo)


def _ffn_kernel(x_ref, sh_ref, sc_ref, gate_ref, g_ref, wi_ref, wo_ref, fg_ref, o_ref, *, n_split, final_norm):
    x = x_ref[...]
    hidden = wo_ref.shape[0]
    h = _norm_modulate(x, g_ref[...], sh_ref[0], sc_ref[0]).astype(BF16)
    step = hidden // n_split
    y = None
    for j in range(n_split):
        gt = jnp.dot(h, wi_ref[:, j * step:(j + 1) * step], preferred_element_type=F32)
        up = jnp.dot(h, wi_ref[:, hidden + j * step:hidden + (j + 1) * step], preferred_element_type=F32)
        act = (gt * jax.nn.sigmoid(gt) * up).astype(BF16)
        t = jnp.dot(act, wo_ref[j * step:(j + 1) * step, :], preferred_element_type=F32)
        y = t if y is None else y + t
    out = x + gate_ref[0] * y
    if final_norm:
        ms = jnp.mean(out * out, axis=-1, keepdims=True)
        out = out * lax.rsqrt(ms + EPS) * fg_ref[...]
    o_ref[...] = out


def _ffn(x2, mod3, norm_g, w_in, w_out, final_g, *, n_batch, seq, tm, final_norm):
    nt, d = x2.shape
    tps = seq // tm
    tok = lambda i: (i, 0)
    mrow = lambda j: (lambda i: ((i // tps) * 6 + j, 0, 0))
    return pl.pallas_call(
        functools.partial(_ffn_kernel, n_split=2, final_norm=final_norm),
        grid=(nt // tm,),
        in_specs=[
            pl.BlockSpec((tm, d), tok),
            pl.BlockSpec((1, 1, d), mrow(3)),
            pl.BlockSpec((1, 1, d), mrow(4)),
            pl.BlockSpec((1, 1, d), mrow(5)),
            _const_spec((1, d)),
            _const_spec(w_in.shape),
            _const_spec(w_out.shape),
            _const_spec((1, d)),
        ],
        out_specs=pl.BlockSpec((tm, d), tok),
        out_shape=jax.ShapeDtypeStruct((nt, d), F32),
        compiler_params=_params(1),
        name="ffn_final" if final_norm else "ffn",
    )(x2, mod3, mod3, mod3, norm_g, w_in, w_out, final_g)


def _rope_tables(seq):
    t = jnp.arange(seq)
    pos_row = (t // GRID_W).astype(F32)
    pos_col = (t % GRID_W).astype(F32)
    half = HEAD_DIM // 2
    freqs = jnp.power(ROPE_THETA, -jnp.arange(0, half, 2, dtype=F32) / half)

    def part(pos):
        ang = pos[:, None] * freqs
        c, s = jnp.cos(ang), jnp.sin(ang)
        return jnp.concatenate([c, c], axis=-1), jnp.concatenate([-s, s], axis=-1)

    cr, sr = part(pos_row)
    cc, sc = part(pos_col)
    cos = jnp.concatenate([cr, cc], axis=-1)
    sin = jnp.concatenate([sr, sc], axis=-1)
    return jnp.tile(cos, (1, LANES // HEAD_DIM)), jnp.tile(sin, (1, LANES // HEAD_DIM))


def kernel(x, c, ctx, c_ctx, w_mod, b_mod, norm1_g, norm2_g, w_in, conv_w, conv_b, conv_ln_g, conv_ln_b,
           w_conv_out, na_rpb, w_na_out, q_norm_g, k_norm_g, w_gqa_out, w_out, w_ffn_in, w_ffn_out, final_g):
    n_batch, seq, d = x.shape
    ctx_len = ctx.shape[1]
    depth = w_mod.shape[0]
    rows = seq // GRID_W
    assert seq % (NA_QROWS * GRID_W) == 0 and rows >= NA_WROWS and ctx_len == KEY_CHUNK
    tm = 512
    tmc = ctx_len

    n_rows = -(-(n_batch + 1) // 8) * 8
    crows = jnp.zeros((n_rows, d), F32).at[:n_batch].set(c).at[n_batch].set(c_ctx)
    mod = _modulation(crows, w_mod, b_mod)

    cos_t, sin_t = _rope_tables(seq)
    bd = jnp.asarray(np.kron(np.eye(LANES // HEAD_DIM), np.full((HEAD_DIM, HEAD_DIM), 1.0 / HEAD_DIM)), BF16)
    tile2 = lambda g: jnp.tile(g, LANES // HEAD_DIM).reshape(1, LANES)
    row = lambda g: g.reshape(1, -1)

    xl = x.reshape(n_batch * seq, d)
    xc = ctx.reshape(n_batch * ctx_len, d)
    for l in range(depth):
        last = l == depth - 1
        wl = w_in[l]
        w_proj = jnp.concatenate([wl[:, _IN_GLU:_IN_NAV], wl[:, _IN_GQ:_IN_GV]], axis=1).astype(BF16)
        w_vt = jnp.concatenate([wl[:, _IN_NAV:_IN_GQ], wl[:, _IN_GV:_IN_GATES]], axis=1).T.astype(BF16)
        w_gates = wl[:, _IN_GATES:].astype(BF16)
        mod_l = mod[l, :n_batch].reshape(n_batch * 6, 1, d)
        mod_c = jnp.broadcast_to(mod[l, n_batch], (n_batch, 6 * d)).reshape(n_batch * 6, 1, d)
        qg2, kg2 = tile2(q_norm_g[l]), tile2(k_norm_g[l])
        g1, g2 = row(norm1_g[l]), row(norm2_g[l])
        wa, wb, wc = w_conv_out[l].astype(BF16), w_na_out[l].astype(BF16), w_gqa_out[l].astype(BF16)
        wo = w_out[l].astype(BF16)
        wfi, wfo = w_ffn_in[l].astype(BF16), w_ffn_out[l].astype(BF16)
        conv_args = (conv_w[l], row(conv_b[l]), row(conv_ln_g[l]), row(conv_ln_b[l]))

        u, naq, nak, gq, gk, navt, gvt = _inproj(xl, mod_l, g1, w_proj, w_vt, qg2, kg2, bd, cos_t, sin_t,
                                                 n_batch=n_batch, seq=seq, tm=tm, rope=True)
        uc, naqc, nakc, gqc, gkc, navct, gvct = _inproj(xc, mod_c, g1, w_proj, w_vt, qg2, kg2, bd,
                                                        cos_t[:ctx_len], sin_t[:ctx_len],
                                                        n_batch=n_batch, seq=ctx_len, tm=tmc, rope=False)

        br_a = _conv(u, *conv_args, n_batch=n_batch, seq=seq, tl=tm)
        bias = _na_bias_tables(na_rpb[l], rows)
        br_b = _na_attention(naq, nak, navt, nakc, navct, bias, n_batch=n_batch, seq=seq, ctx_len=ctx_len)
        br_c = _gqa_attention(gq, gk, gkc, gvt, gvct, n_batch=n_batch, sq=seq, ctx_len=ctx_len, tq=128)
        xl = _merge(xl, mod_l, g1, w_gates, br_a, br_b, br_c, wa, wb, wc, wo, n_batch=n_batch, seq=seq, tm=tm)
        xl = _ffn(xl, mod_l, g2, wfi, wfo, row(final_g), n_batch=n_batch, seq=seq, tm=tm, final_norm=last)

        if not last:
            cbr_a = _conv(uc, *conv_args, n_batch=n_batch, seq=ctx_len, tl=tmc)
            cbr_b = _ctx_attention(naqc, nakc, navct, n_batch=n_batch, ctx_len=ctx_len, n_kv=NA_HEADS, group=1)
            cbr_c = _ctx_attention(gqc, gkc, gvct, n_batch=n_batch, ctx_len=ctx_len, n_kv=GQA_KV_HEADS,
                                   group=GQA_GROUP)
            xc = _merge(xc, mod_c, g1, w_gates, cbr_a, cbr_b, cbr_c, wa, wb, wc, wo,
                        n_batch=n_batch, seq=ctx_len, tm=tmc)
            xc = _ffn(xc, mod_c, g2, wfi, wfo, row(final_g), n_batch=n_batch, seq=ctx_len, tm=tmc,
                      final_norm=False)
    return xl.reshape(n_batch, seq, d)
```

```python
import functools

import numpy as np
import jax
import jax.numpy as jnp
from jax import lax
from jax.experimental import pallas as pl
from jax.experimental.pallas import tpu as pltpu

F32 = jnp.float32
BF16 = jnp.bfloat16

GRID_W = 64
HEAD_DIM = 64
CONV_W = 512
CONV_K = 31
CONV_HALO = 16
CONV_ROWS = 64
NA_HEADS = 8
NA_W = NA_HEADS * HEAD_DIM
NA_KH = 8
NA_KW = 16
NA_QROWS = 4
NA_WROWS = 12
GQA_Q_HEADS = 8
GQA_KV_HEADS = 2
GQA_GROUP = GQA_Q_HEADS // GQA_KV_HEADS
GQA_Q_W = GQA_Q_HEADS * HEAD_DIM
GQA_KV_W = GQA_KV_HEADS * HEAD_DIM
GQA_UNROLL = 4
ROPE_THETA = 10000.0
EPS = 1e-6
NEG = -1e30
SCALE = HEAD_DIM ** -0.5
LOG2E = 1.4426950408889634
LANES = 128
SUBLANES = 8
KEY_CHUNK = 256
ONES_ROWS = 16
VMEM_LIMIT = 56 * 1024 * 1024

_IN_GLU = 0
_IN_NAQ = _IN_GLU + 2 * CONV_W
_IN_NAK = _IN_NAQ + NA_W
_IN_NAV = _IN_NAK + NA_W
_IN_GQ = _IN_NAV + NA_W
_IN_GK = _IN_GQ + GQA_Q_W
_IN_GV = _IN_GK + GQA_KV_W
_IN_GATES = _IN_GV + GQA_KV_W
_P_GLU = 0
_P_NAQ = _P_GLU + 2 * CONV_W
_P_NAK = _P_NAQ + NA_W
_P_GQ = _P_NAK + NA_W
_P_GK = _P_GQ + GQA_Q_W
_P_END = _P_GK + GQA_KV_W


def _params(n_axes):
    return pltpu.CompilerParams(dimension_semantics=("arbitrary",) * n_axes, vmem_limit_bytes=VMEM_LIMIT)


def _const_spec(shape):
    zeros = (0,) * len(shape)
    return pl.BlockSpec(shape, lambda *_: zeros, pipeline_mode=pl.Buffered(1))


def _norm_modulate(x, g, shift, scale):
    ms = jnp.mean(x * x, axis=-1, keepdims=True)
    h = x * lax.rsqrt(ms + EPS) * g
    return h * (1.0 + scale) + shift


def _with_ones(v_t):
    return jnp.concatenate([v_t, jnp.ones((ONES_ROWS, v_t.shape[1]), v_t.dtype)], axis=0)


def _mod_kernel(c_ref, w_ref, b_ref, o_ref):
    c = c_ref[...]
    s = c * jax.nn.sigmoid(c)
    o_ref[0] = jnp.dot(s, w_ref[0], preferred_element_type=F32, precision=lax.Precision.HIGHEST) + b_ref[0]


def _modulation(crows, w_mod, b_mod):
    n_layers, d, d6 = w_mod.shape
    r = crows.shape[0]
    nb = d6 // d
    return pl.pallas_call(
        _mod_kernel,
        grid=(n_layers, nb),
        in_specs=[
            pl.BlockSpec((r, d), lambda l, j: (0, 0)),
            pl.BlockSpec((1, d, d), lambda l, j: (l, 0, j)),
            pl.BlockSpec((1, 1, d), lambda l, j: (l, 0, j)),
        ],
        out_specs=pl.BlockSpec((1, r, d), lambda l, j: (l, 0, j)),
        out_shape=jax.ShapeDtypeStruct((n_layers, r, d6), F32),
        compiler_params=_params(2),
        name="modulation",
    )(crows, w_mod, b_mod.reshape(n_layers, 1, d6))


def _head_rms(t, bd):
    ms = jnp.dot((t * t).astype(BF16), bd, preferred_element_type=F32)
    return t * lax.rsqrt(ms + EPS)


def _rope(t, cos, sin_signed, first_half):
    fwd = pltpu.roll(t, LANES - 16, axis=1)
    bwd = pltpu.roll(t, 16, axis=1)
    return t * cos + jnp.where(first_half, fwd, bwd) * sin_signed


def _inproj_kernel(x_ref, sh_ref, sc_ref, g_ref, w_ref, wvt_ref, qg_ref, kg_ref, bd_ref, cos_ref, sin_ref,
                   u_ref, naq_ref, nak_ref, gq_ref, gk_ref, navt_ref, gvt_ref, *, rope):
    h = _norm_modulate(x_ref[...], g_ref[...], sh_ref[0], sc_ref[0]).astype(BF16)

    def proj(lo, hi):
        return jnp.dot(h, w_ref[:, lo:hi], preferred_element_type=F32)

    q_scale = SCALE * LOG2E if rope else SCALE
    bd = bd_ref[...]
    if rope:
        cos = cos_ref[...]
        sin = sin_ref[...]
        lane = lax.broadcasted_iota(jnp.int32, cos.shape, 1)
        first_half = (lane % 32) < 16

    def finish(t, gain):
        t = _head_rms(t, bd) * gain
        if rope:
            t = _rope(t, cos, sin, first_half)
        return t

    q = proj(_P_GQ, _P_GK)
    for j in range(GQA_Q_W // LANES):
        qj = finish(q[:, j * LANES:(j + 1) * LANES], qg_ref[...])
        gq_ref[:, j * LANES:(j + 1) * LANES] = (qj * q_scale).astype(BF16)
    gk_ref[...] = finish(proj(_P_GK, _P_END), kg_ref[...]).astype(BF16)

    a = proj(_P_GLU, _P_NAQ)
    u_ref[...] = (a[:, :CONV_W] * jax.nn.sigmoid(a[:, CONV_W:])).astype(BF16)
    naq_ref[...] = (proj(_P_NAQ, _P_NAK) * q_scale).astype(BF16)
    nak_ref[...] = proj(_P_NAK, _P_GQ).astype(BF16)

    v_t = lax.dot_general(wvt_ref[...], h, (((1,), (1,)), ((), ())), preferred_element_type=F32).astype(BF16)
    for c in range(navt_ref.shape[1]):
        navt_ref[0, c] = v_t[:NA_W, c * KEY_CHUNK:(c + 1) * KEY_CHUNK]
        gvt_ref[0, c] = v_t[NA_W:, c * KEY_CHUNK:(c + 1) * KEY_CHUNK]


def _inproj(x2, mod3, norm_g, w, w_vt, qg2, kg2, bd, cos_t, sin_t, *, n_batch, seq, tm, rope):
    nt, d = x2.shape
    tps = seq // tm
    cpt = tm // KEY_CHUNK
    tok = lambda s, b: (b * tps + s, 0)
    outs = [CONV_W, NA_W, NA_W, GQA_Q_W, GQA_KV_W]
    vt_shape = lambda w_: jax.ShapeDtypeStruct((n_batch, seq // KEY_CHUNK, w_, KEY_CHUNK), BF16)
    vt_spec = lambda w_: pl.BlockSpec((1, cpt, w_, KEY_CHUNK), lambda s, b: (b, s, 0, 0))
    return pl.pallas_call(
        functools.partial(_inproj_kernel, rope=rope),
        grid=(tps, n_batch),
        in_specs=[
            pl.BlockSpec((tm, d), tok),
            pl.BlockSpec((1, 1, d), lambda s, b: (b * 6 + 0, 0, 0)),
            pl.BlockSpec((1, 1, d), lambda s, b: (b * 6 + 1, 0, 0)),
            _const_spec((1, d)),
            _const_spec(w.shape),
            _const_spec(w_vt.shape),
            _const_spec((1, LANES)),
            _const_spec((1, LANES)),
            _const_spec((LANES, LANES)),
            pl.BlockSpec((tm, LANES), lambda s, b: (s, 0)),
            pl.BlockSpec((tm, LANES), lambda s, b: (s, 0)),
        ],
        out_specs=[pl.BlockSpec((tm, n), tok) for n in outs] + [vt_spec(NA_W), vt_spec(GQA_KV_W)],
        out_shape=[jax.ShapeDtypeStruct((nt, n), BF16) for n in outs] + [vt_shape(NA_W), vt_shape(GQA_KV_W)],
        compiler_params=_params(2),
        name="inproj_rope" if rope else "inproj_ctx",
    )(x2, mod3, mod3, norm_g, w, w_vt, qg2, kg2, bd, cos_t, sin_t)


def _conv_ln_swish(prev_ref, cur_ref, next_ref, w_ref, b_ref, lg_ref, lb_ref, win_ref, shift_ref, out_ref, first,
                   last):
    tl = cur_ref.shape[0]
    win_ref[0:CONV_HALO, :] = jnp.where(first, 0.0, prev_ref[...].astype(F32))
    win_ref[CONV_HALO:CONV_HALO + tl, :] = cur_ref[...].astype(F32)
    win_ref[CONV_HALO + tl:, :] = jnp.where(last, 0.0, next_ref[...].astype(F32))
    base = CONV_HALO - CONV_K // 2
    for r in range(SUBLANES):
        n_shift = tl + SUBLANES * (len(range(r, CONV_K, SUBLANES)) - 1)
        shift_ref[r, 0:n_shift, :] = win_ref[base + r:base + r + n_shift, :]
    for rb in range(0, tl, CONV_ROWS):
        acc = jnp.zeros((CONV_ROWS, CONV_W), F32)
        for k in range(CONV_K):
            off = rb + SUBLANES * (k // SUBLANES)
            acc = acc + shift_ref[k % SUBLANES, off:off + CONV_ROWS, :] * w_ref[k:k + 1, :]
        y = acc + b_ref[...]
        mu = jnp.mean(y, axis=-1, keepdims=True)
        yc = y - mu
        var = jnp.mean(yc * yc, axis=-1, keepdims=True)
        z = yc * lax.rsqrt(var + EPS) * lg_ref[...] + lb_ref[...]
        out_ref[rb:rb + CONV_ROWS, :] = (z * jax.nn.sigmoid(z)).astype(out_ref.dtype)


def _conv_halo_specs(n_tokens, tl):
    hb = tl // CONV_HALO
    last_hb = n_tokens // CONV_HALO - 1
    return [
        pl.BlockSpec((CONV_HALO, CONV_W), lambda i: (jnp.maximum(i * hb - 1, 0), 0)),
        pl.BlockSpec((tl, CONV_W), lambda i: (i, 0)),
        pl.BlockSpec((CONV_HALO, CONV_W), lambda i: (jnp.minimum((i + 1) * hb, last_hb), 0)),
    ]


def _na_bias_tables(rpb, rows):
    cols = np.arange(GRID_W)
    win0 = np.clip(cols - NA_KW // 2, 0, GRID_W - NA_KW)
    col_ok = (cols[:, None] >= win0[None, :]) & (cols[:, None] < win0[None, :] + NA_KW)
    pad = GRID_W - NA_KW
    rp = jnp.pad(rpb * LOG2E, ((0, 0), (0, 0), (pad, pad)))
    tiles = jnp.stack([rp[:, :, GRID_W - 1 - qc:2 * GRID_W - 1 - qc] for qc in range(GRID_W)], axis=-1)
    tiles = jnp.where(jnp.asarray(col_ok)[None, None], tiles, NEG * LOG2E)
    masked = jnp.full(tiles.shape[:1] + tiles.shape[2:], NEG * LOG2E, F32)
    n_groups = rows // NA_QROWS
    variants = []
    for g in (0, min(1, n_groups - 1), n_groups - 1):
        w = int(np.clip(NA_QROWS * g - NA_KH // 2, 0, rows - NA_WROWS))
        blocks = []
        for kl in range(NA_WROWS):
            kr = w + kl
            row = []
            for ql in range(NA_QROWS):
                r = NA_QROWS * g + ql
                r0 = int(np.clip(r - NA_KH // 2, 0, rows - NA_KH))
                row.append(tiles[:, kr - r + NA_KH - 1] if r0 <= kr < r0 + NA_KH else masked)
            blocks.append(jnp.concatenate(row, axis=-1))
        variants.append(jnp.concatenate(blocks, axis=1))
    return jnp.stack(variants)


def _na_kernel(q_ref, k_ref, vt_ref, kc_ref, vct_ref, bias_ref, o_ref, s0_ref, s1_ref, ot_ref, *, n_chunks):
    g = pl.program_id(1)
    nq = NA_QROWS * GRID_W
    win_chunks = NA_WROWS * GRID_W // KEY_CHUNK
    c0 = jnp.clip(g - 1, 0, n_chunks - win_chunks)
    start = pl.multiple_of(c0 * KEY_CHUNK, KEY_CHUNK)
    n_loc = win_chunks * KEY_CHUNK
    q_t = q_ref[...].astype(F32).T.astype(BF16)
    zeros = jnp.zeros((HEAD_DIM, nq), BF16)
    s_refs = (s0_ref, s1_ref)

    def scores(h, s_ref):
        hp, j = divmod(h, LANES // HEAD_DIM)
        qh = q_t[h * HEAD_DIM:(h + 1) * HEAD_DIM]
        qz = jnp.concatenate([qh, zeros] if j == 0 else [zeros, qh], axis=0)
        kwin = k_ref[pl.ds(start, n_loc), hp * LANES:(hp + 1) * LANES]
        s_ref[0:n_loc, :] = jnp.dot(kwin, qz, preferred_element_type=F32) + bias_ref[0, h]
        s_ref[n_loc:, :] = jnp.dot(kc_ref[:, hp * LANES:(hp + 1) * LANES], qz, preferred_element_type=F32)

    def softmax_pv(h, s_ref):
        s = s_ref[...]
        p = jnp.exp2(s - jnp.max(s, axis=0, keepdims=True)).astype(BF16)
        rows_h = slice(h * HEAD_DIM, (h + 1) * HEAD_DIM)
        acc = jnp.dot(_with_ones(vct_ref[0, 0, rows_h, :]), p[n_loc:], preferred_element_type=F32)
        for t in range(win_chunks):
            acc = acc + jnp.dot(_with_ones(vt_ref[0, c0 + t, rows_h, :]), p[t * KEY_CHUNK:(t + 1) * KEY_CHUNK],
                                preferred_element_type=F32)
        ot_ref[rows_h, :] = acc[:HEAD_DIM] * (1.0 / acc[HEAD_DIM:HEAD_DIM + 1])

    scores(0, s_refs[0])
    for h in range(NA_HEADS):
        if h + 1 < NA_HEADS:
            scores(h + 1, s_refs[(h + 1) % 2])
        softmax_pv(h, s_refs[h % 2])
    o_ref[...] = ot_ref[...].T.astype(BF16)


def _na_attention(q, k, vt, kc, vct, bias, *, n_batch, seq, ctx_len):
    rows = seq // GRID_W
    n_groups = rows // NA_QROWS
    nq = NA_QROWS * GRID_W
    variant = lambda g: jnp.where(g == 0, 0, jnp.where(g == n_groups - 1, 2, 1))
    return pl.pallas_call(
        functools.partial(_na_kernel, n_chunks=seq // KEY_CHUNK),
        grid=(n_batch, n_groups),
        in_specs=[
            pl.BlockSpec((nq, NA_W), lambda b, g: (b * n_groups + g, 0)),
            pl.BlockSpec((seq, NA_W), lambda b, g: (b, 0)),
            pl.BlockSpec((1,) + vt.shape[1:], lambda b, g: (b, 0, 0, 0)),
            pl.BlockSpec((ctx_len, NA_W), lambda b, g: (b, 0)),
            pl.BlockSpec((1,) + vct.shape[1:], lambda b, g: (b, 0, 0, 0)),
            pl.BlockSpec((1,) + bias.shape[1:], lambda b, g: (variant(g), 0, 0, 0)),
        ],
        out_specs=pl.BlockSpec((nq, NA_W), lambda b, g: (b * n_groups + g, 0)),
        out_shape=jax.ShapeDtypeStruct(q.shape, BF16),
        scratch_shapes=[
            pltpu.VMEM((NA_WROWS * GRID_W + ctx_len, nq), F32),
            pltpu.VMEM((NA_WROWS * GRID_W + ctx_len, nq), F32),
            pltpu.VMEM((NA_W, nq), F32),
        ],
        compiler_params=_params(2),
        name="na_attention",
    )(q, k, vt, kc, vct, bias)


def _gqa_kernel(q_ref, k_ref, kc_ref, vt_ref, vct_ref, o_ref, qt_ref, s0_ref, s1_ref, m_ref, acc_ref, *, n_chunks):
    tq = q_ref.shape[0]
    q_t = q_ref[...].astype(F32).T
    for hk in range(GQA_KV_HEADS):
        for gi in range(GQA_GROUP):
            hq = hk * GQA_GROUP + gi
            qt_ref[hk, :, gi * tq:(gi + 1) * tq] = q_t[hq * HEAD_DIM:(hq + 1) * HEAD_DIM, :].astype(BF16)
    m_ref[...] = jnp.full(m_ref.shape, NEG, F32)
    acc_ref[...] = jnp.zeros(acc_ref.shape, F32)

    def scores(k_rows, s_ref):
        for hk in range(GQA_KV_HEADS):
            kch = k_rows[:, hk * HEAD_DIM:(hk + 1) * HEAD_DIM]
            s_ref[hk] = jnp.dot(kch, qt_ref[hk], preferred_element_type=F32)

    def softmax_pv(v_t, s_ref):
        for hk in range(GQA_KV_HEADS):
            s = s_ref[hk]
            m_old = m_ref[hk]
            m_new = jnp.maximum(m_old, jnp.max(s, axis=0, keepdims=True))
            p = jnp.exp2(s - m_new).astype(BF16)
            pv = jnp.dot(_with_ones(v_t[hk * HEAD_DIM:(hk + 1) * HEAD_DIM]), p, preferred_element_type=F32)
            acc_ref[hk] = acc_ref[hk] * jnp.exp2(m_old - m_new) + pv
            m_ref[hk] = m_new

    k_lat = lambda c: k_ref[pl.ds(pl.multiple_of(c * KEY_CHUNK, KEY_CHUNK), KEY_CHUNK), :]
    v_lat = lambda c: vt_ref[0, c]

    s_refs = (s0_ref, s1_ref)
    scores(k_lat(0), s_refs[0])

    def steps(c0, n):
        for j in range(n):
            c = c0 + j
            nxt_is_ctx = isinstance(c, int) and c + 1 == n_chunks
            scores(kc_ref[...] if nxt_is_ctx else k_lat(c + 1), s_refs[(j + 1) % 2])
            softmax_pv(v_lat(c), s_refs[j % 2])

    trips = (n_chunks - 1) // GQA_UNROLL

    def body(i, carry):
        steps(i * GQA_UNROLL, GQA_UNROLL)
        return carry

    lax.fori_loop(0, trips, body, 0)
    steps(trips * GQA_UNROLL, n_chunks - trips * GQA_UNROLL)
    softmax_pv(vct_ref[0, 0], s_refs[n_chunks % 2])

    heads_t = []
    for hk in range(GQA_KV_HEADS):
        acc = acc_ref[hk]
        out_t = acc[:HEAD_DIM] * (1.0 / acc[HEAD_DIM:HEAD_DIM + 1])
        heads_t += [out_t[:, gi * tq:(gi + 1) * tq] for gi in range(GQA_GROUP)]
    o_ref[...] = jnp.concatenate(heads_t, axis=0).T.astype(BF16)


def _gqa_attention(q, k, kc, vt, vct, *, n_batch, sq, ctx_len, tq):
    tpq = sq // tq
    m = GQA_GROUP * tq
    n_chunks = sq // KEY_CHUNK
    assert n_chunks % 2 == 0 and n_chunks >= 2 and ctx_len == KEY_CHUNK
    return pl.pallas_call(
        functools.partial(_gqa_kernel, n_chunks=n_chunks),
        grid=(n_batch, tpq),
        in_specs=[
            pl.BlockSpec((tq, GQA_Q_W), lambda b, i: (b * tpq + i, 0)),
            pl.BlockSpec((sq, GQA_KV_W), lambda b, i: (b, 0)),
            pl.BlockSpec((ctx_len, GQA_KV_W), lambda b, i: (b, 0)),
            pl.BlockSpec((1,) + vt.shape[1:], lambda b, i: (b, 0, 0, 0)),
            pl.BlockSpec((1,) + vct.shape[1:], lambda b, i: (b, 0, 0, 0)),
        ],
        out_specs=pl.BlockSpec((tq, GQA_Q_W), lambda b, i: (b * tpq + i, 0)),
        out_shape=jax.ShapeDtypeStruct(q.shape, BF16),
        scratch_shapes=[
            pltpu.VMEM((GQA_KV_HEADS, HEAD_DIM, m), BF16),
            pltpu.VMEM((GQA_KV_HEADS, KEY_CHUNK, m), F32),
            pltpu.VMEM((GQA_KV_HEADS, KEY_CHUNK, m), F32),
            pltpu.VMEM((GQA_KV_HEADS, 1, m), F32),
            pltpu.VMEM((GQA_KV_HEADS, HEAD_DIM + ONES_ROWS, m), F32),
        ],
        compiler_params=_params(2),
        name="gqa_attention",
    )(q, k, kc, vt, vct)


def _ctx_attn_kernel(q_ref, k_ref, vt_ref, o_ref, ot_ref, *, n_kv, group):
    tq = q_ref.shape[0]
    q_t = q_ref[...].astype(F32).T.astype(BF16)
    for hk in range(n_kv):
        rows_k = slice(hk * HEAD_DIM, (hk + 1) * HEAD_DIM)
        qs = jnp.concatenate([q_t[(hk * group + gi) * HEAD_DIM:(hk * group + gi + 1) * HEAD_DIM]
                              for gi in range(group)], axis=1)
        s = jnp.dot(k_ref[:, rows_k], qs, preferred_element_type=F32)
        p = jnp.exp(s - jnp.max(s, axis=0, keepdims=True)).astype(BF16)
        acc = jnp.dot(_with_ones(vt_ref[0, 0, rows_k, :]), p, preferred_element_type=F32)
        out_t = acc[:HEAD_DIM] * (1.0 / acc[HEAD_DIM:HEAD_DIM + 1])
        for gi in range(group):
            hq = hk * group + gi
            ot_ref[hq * HEAD_DIM:(hq + 1) * HEAD_DIM, :] = out_t[:, gi * tq:(gi + 1) * tq]
    o_ref[...] = ot_ref[...].T.astype(BF16)


def _ctx_attention(q, k, vt, *, n_batch, ctx_len, n_kv, group):
    return pl.pallas_call(
        functools.partial(_ctx_attn_kernel, n_kv=n_kv, group=group),
        grid=(n_batch,),
        in_specs=[
            pl.BlockSpec((ctx_len, q.shape[1]), lambda b: (b, 0)),
            pl.BlockSpec((ctx_len, k.shape[1]), lambda b: (b, 0)),
            pl.BlockSpec((1,) + vt.shape[1:], lambda b: (b, 0, 0, 0)),
        ],
        out_specs=pl.BlockSpec((ctx_len, q.shape[1]), lambda b: (b, 0)),
        out_shape=jax.ShapeDtypeStruct(q.shape, BF16),
        scratch_shapes=[pltpu.VMEM((q.shape[1], ctx_len), F32)],
        compiler_params=_params(1),
        name=f"ctx_attention_g{group}",
    )(q, k, vt)


def _merge_kernel(x_ref, sh_ref, sc_ref, gate_ref, g_ref, wg_ref, uprev_ref, u_ref, unext_ref, cw_ref, cb_ref,
                  lg_ref, lb_ref, b_ref, c_ref, wa_ref, wb_ref, wc_ref, wo_ref, o_ref, win_ref, shift_ref, bra_ref,
                  *, tiles_per_seq):
    pos = pl.program_id(0) % tiles_per_seq
    x = x_ref[...]
    d = x.shape[1]
    h = _norm_modulate(x, g_ref[...], sh_ref[0], sc_ref[0]).astype(BF16)
    def gated(j, br, w_ref):
        gate = jax.nn.sigmoid(jnp.dot(h, wg_ref[:, j * d:(j + 1) * d], preferred_element_type=F32))
        return gate * jnp.dot(br, w_ref[...], preferred_element_type=F32)

    y = gated(1, b_ref[...], wb_ref) + gated(2, c_ref[...], wc_ref)
    gate_a = jax.nn.sigmoid(jnp.dot(h, wg_ref[:, 0:d], preferred_element_type=F32))
    _conv_ln_swish(uprev_ref, u_ref, unext_ref, cw_ref, cb_ref, lg_ref, lb_ref, win_ref, shift_ref, bra_ref,
                   pos == 0, pos == tiles_per_seq - 1)
    y = y + gate_a * jnp.dot(bra_ref[...], wa_ref[...], preferred_element_type=F32)
    o_ref[...] = x + gate_ref[0] * jnp.dot(y.astype(BF16), wo_ref[...], preferred_element_type=F32)


def _merge(x2, mod3, norm_g, w_gates, u, conv_w, conv_b, ln_g, ln_b, br_b, br_c, wa, wb, wc, wo, *, n_batch, seq,
           tm):
    nt, d = x2.shape
    tps = seq // tm
    tok = lambda i: (i, 0)
    mrow = lambda j: (lambda i: ((i // tps) * 6 + j, 0, 0))
    return pl.pallas_call(
        functools.partial(_merge_kernel, tiles_per_seq=tps),
        grid=(nt // tm,),
        in_specs=[
            pl.BlockSpec((tm, d), tok),
            pl.BlockSpec((1, 1, d), mrow(0)),
            pl.BlockSpec((1, 1, d), mrow(1)),
            pl.BlockSpec((1, 1, d), mrow(2)),
            _const_spec((1, d)),
            _const_spec(w_gates.shape),
            *_conv_halo_specs(nt, tm),
            _const_spec((CONV_K, CONV_W)),
            _const_spec((1, CONV_W)),
            _const_spec((1, CONV_W)),
            _const_spec((1, CONV_W)),
            pl.BlockSpec((tm, NA_W), tok),
            pl.BlockSpec((tm, GQA_Q_W), tok),
            _const_spec(wa.shape),
            _const_spec(wb.shape),
            _const_spec(wc.shape),
            _const_spec(wo.shape),
        ],
        out_specs=pl.BlockSpec((tm, d), tok),
        out_shape=jax.ShapeDtypeStruct((nt, d), F32),
        scratch_shapes=[pltpu.VMEM((tm + 2 * CONV_HALO, CONV_W), F32),
                        pltpu.VMEM((SUBLANES, tm + 2 * CONV_HALO, CONV_W), F32),
                        pltpu.VMEM((tm, CONV_W), BF16)],
        compiler_params=_params(1),
        name="merge",
    )(x2, mod3, mod3, mod3, norm_g, w_gates, u, u, u, conv_w, conv_b, ln_g, ln_b, br_b, br_c, wa, wb, wc, wo)


def _ffn_kernel(x_ref, sh_ref, sc_ref, gate_ref, g_ref, wi_ref, wo_ref, fg_ref, o_ref, *, n_split, final_norm):
    x = x_ref[...]
    hidden = wo_ref.shape[0]
    h = _norm_modulate(x, g_ref[...], sh_ref[0], sc_ref[0]).astype(BF16)
    step = hidden // n_split
    y = None
    for j in range(n_split):
        gt = jnp.dot(h, wi_ref[:, j * step:(j + 1) * step], preferred_element_type=F32)
        up = jnp.dot(h, wi_ref[:, hidden + j * step:hidden + (j + 1) * step], preferred_element_type=F32)
        act = (gt * jax.nn.sigmoid(gt) * up).astype(BF16)
        t = jnp.dot(act, wo_ref[j * step:(j + 1) * step, :], preferred_element_type=F32)
        y = t if y is None else y + t
    out = x + gate_ref[0] * y
    if final_norm:
        ms = jnp.mean(out * out, axis=-1, keepdims=True)
        out = out * lax.rsqrt(ms + EPS) * fg_ref[...]
    o_ref[...] = out


def _ffn(x2, mod3, norm_g, w_in, w_out, final_g, *, n_batch, seq, tm, final_norm):
    nt, d = x2.shape
    tps = seq // tm
    tok = lambda i: (i, 0)
    mrow = lambda j: (lambda i: ((i // tps) * 6 + j, 0, 0))
    return pl.pallas_call(
        functools.partial(_ffn_kernel, n_split=2, final_norm=final_norm),
        grid=(nt // tm,),
        in_specs=[
            pl.BlockSpec((tm, d), tok),
            pl.BlockSpec((1, 1, d), mrow(3)),
            pl.BlockSpec((1, 1, d), mrow(4)),
            pl.BlockSpec((1, 1, d), mrow(5)),
            _const_spec((1, d)),
            _const_spec(w_in.shape),
            _const_spec(w_out.shape),
            _const_spec((1, d)),
        ],
        out_specs=pl.BlockSpec((tm, d), tok),
        out_shape=jax.ShapeDtypeStruct((nt, d), F32),
        compiler_params=_params(1),
        name="ffn_final" if final_norm else "ffn",
    )(x2, mod3, mod3, mod3, norm_g, w_in, w_out, final_g)


def _rope_tables(seq):
    t = jnp.arange(seq)
    pos_row = (t // GRID_W).astype(F32)
    pos_col = (t % GRID_W).astype(F32)
    half = HEAD_DIM // 2
    freqs = jnp.power(ROPE_THETA, -jnp.arange(0, half, 2, dtype=F32) / half)

    def part(pos):
        ang = pos[:, None] * freqs
        c, s = jnp.cos(ang), jnp.sin(ang)
        return jnp.concatenate([c, c], axis=-1), jnp.concatenate([-s, s], axis=-1)

    cr, sr = part(pos_row)
    cc, sc = part(pos_col)
    cos = jnp.concatenate([cr, cc], axis=-1)
    sin = jnp.concatenate([sr, sc], axis=-1)
    return jnp.tile(cos, (1, LANES // HEAD_DIM)), jnp.tile(sin, (1, LANES // HEAD_DIM))


def kernel(x, c, ctx, c_ctx, w_mod, b_mod, norm1_g, norm2_g, w_in, conv_w, conv_b, conv_ln_g, conv_ln_b,
           w_conv_out, na_rpb, w_na_out, q_norm_g, k_norm_g, w_gqa_out, w_out, w_ffn_in, w_ffn_out, final_g):
    n_batch, seq, d = x.shape
    ctx_len = ctx.shape[1]
    depth = w_mod.shape[0]
    rows = seq // GRID_W
    assert seq % (NA_QROWS * GRID_W) == 0 and rows >= NA_WROWS and ctx_len == KEY_CHUNK
    tm = 512
    tmc = ctx_len

    n_rows = -(-(n_batch + 1) // 8) * 8
    crows = jnp.zeros((n_rows, d), F32).at[:n_batch].set(c).at[n_batch].set(c_ctx)
    mod = _modulation(crows, w_mod, b_mod)

    cos_t, sin_t = _rope_tables(seq)
    bd = jnp.asarray(np.kron(np.eye(LANES // HEAD_DIM), np.full((HEAD_DIM, HEAD_DIM), 1.0 / HEAD_DIM)), BF16)
    tile2 = lambda g: jnp.tile(g, LANES // HEAD_DIM).reshape(1, LANES)
    row = lambda g: g.reshape(1, -1)

    xl = x.reshape(n_batch * seq, d)
    xc = ctx.reshape(n_batch * ctx_len, d)
    for l in range(depth):
        last = l == depth - 1
        wl = w_in[l]
        w_proj = jnp.concatenate([wl[:, _IN_GLU:_IN_NAV], wl[:, _IN_GQ:_IN_GV]], axis=1).astype(BF16)
        w_vt = jnp.concatenate([wl[:, _IN_NAV:_IN_GQ], wl[:, _IN_GV:_IN_GATES]], axis=1).T.astype(BF16)
        w_gates = wl[:, _IN_GATES:].astype(BF16)
        mod_l = mod[l, :n_batch].reshape(n_batch * 6, 1, d)
        mod_c = jnp.broadcast_to(mod[l, n_batch], (n_batch, 6 * d)).reshape(n_batch * 6, 1, d)
        qg2, kg2 = tile2(q_norm_g[l]), tile2(k_norm_g[l])
        g1, g2 = row(norm1_g[l]), row(norm2_g[l])
        wa, wb, wc = w_conv_out[l].astype(BF16), w_na_out[l].astype(BF16), w_gqa_out[l].astype(BF16)
        wo = w_out[l].astype(BF16)
        wfi, wfo = w_ffn_in[l].astype(BF16), w_ffn_out[l].astype(BF16)
        conv_args = (conv_w[l], row(conv_b[l]), row(conv_ln_g[l]), row(conv_ln_b[l]))

        u, naq, nak, gq, gk, navt, gvt = _inproj(xl, mod_l, g1, w_proj, w_vt, qg2, kg2, bd, cos_t, sin_t,
                                                 n_batch=n_batch, seq=seq, tm=tm, rope=True)
        uc, naqc, nakc, gqc, gkc, navct, gvct = _inproj(xc, mod_c, g1, w_proj, w_vt, qg2, kg2, bd,
                                                        cos_t[:ctx_len], sin_t[:ctx_len],
                                                        n_batch=n_batch, seq=ctx_len, tm=tmc, rope=False)

        bias = _na_bias_tables(na_rpb[l], rows)
        br_b = _na_attention(naq, nak, navt, nakc, navct, bias, n_batch=n_batch, seq=seq, ctx_len=ctx_len)
        br_c = _gqa_attention(gq, gk, gkc, gvt, gvct, n_batch=n_batch, sq=seq, ctx_len=ctx_len, tq=128)
        xl = _merge(xl, mod_l, g1, w_gates, u, *conv_args, br_b, br_c, wa, wb, wc, wo,
                    n_batch=n_batch, seq=seq, tm=tm)
        xl = _ffn(xl, mod_l, g2, wfi, wfo, row(final_g), n_batch=n_batch, seq=seq, tm=tm, final_norm=last)

        if not last:
            cbr_b = _ctx_attention(naqc, nakc, navct, n_batch=n_batch, ctx_len=ctx_len, n_kv=NA_HEADS, group=1)
            cbr_c = _ctx_attention(gqc, gkc, gvct, n_batch=n_batch, ctx_len=ctx_len, n_kv=GQA_KV_HEADS,
                                   group=GQA_GROUP)
            xc = _merge(xc, mod_c, g1, w_gates, uc, *conv_args, cbr_b, cbr_c, wa, wb, wc, wo,
                        n_batch=n_batch, seq=ctx_len, tm=tmc)
            xc = _ffn(xc, mod_c, g2, wfi, wfo, row(final_g), n_batch=n_batch, seq=ctx_len, tm=tmc,
                      final_norm=False)
    return xl.reshape(n_batch, seq, d)
```

```python
import functools

import numpy as np
import jax
import jax.numpy as jnp
from jax import lax
from jax.experimental import pallas as pl
from jax.experimental.pallas import tpu as pltpu

F32 = jnp.float32
BF16 = jnp.bfloat16

GRID_W = 64
HEAD_DIM = 64
CONV_W = 512
CONV_K = 31
CONV_HALO = 16
CONV_ROWS = 64
NA_HEADS = 8
NA_W = NA_HEADS * HEAD_DIM
NA_KH = 8
NA_KW = 16
NA_QROWS = 4
NA_WROWS = 12
NA_AHEAD = 8
GQA_Q_HEADS = 8
GQA_KV_HEADS = 2
GQA_GROUP = GQA_Q_HEADS // GQA_KV_HEADS
GQA_Q_W = GQA_Q_HEADS * HEAD_DIM
GQA_KV_W = GQA_KV_HEADS * HEAD_DIM
GQA_AHEAD = 2
GQA_UNROLL = 6
ROPE_THETA = 10000.0
EPS = 1e-6
NEG = -1e30
SCALE = HEAD_DIM ** -0.5
LOG2E = 1.4426950408889634
LANES = 128
SUBLANES = 8
KEY_CHUNK = 256
ONES_ROWS = 16
VMEM_LIMIT = 56 * 1024 * 1024

_IN_GLU = 0
_IN_NAQ = _IN_GLU + 2 * CONV_W
_IN_NAK = _IN_NAQ + NA_W
_IN_NAV = _IN_NAK + NA_W
_IN_GQ = _IN_NAV + NA_W
_IN_GK = _IN_GQ + GQA_Q_W
_IN_GV = _IN_GK + GQA_KV_W
_IN_GATES = _IN_GV + GQA_KV_W
_P_GLU = 0
_P_NAQ = _P_GLU + 2 * CONV_W
_P_NAK = _P_NAQ + NA_W
_P_GQ = _P_NAK + NA_W
_P_GK = _P_GQ + GQA_Q_W
_P_END = _P_GK + GQA_KV_W


def _params(n_axes):
    return pltpu.CompilerParams(dimension_semantics=("arbitrary",) * n_axes, vmem_limit_bytes=VMEM_LIMIT)


def _const_spec(shape):
    zeros = (0,) * len(shape)
    return pl.BlockSpec(shape, lambda *_: zeros, pipeline_mode=pl.Buffered(1))


def _norm_modulate(x, g, shift, scale):
    ms = jnp.mean(x * x, axis=-1, keepdims=True)
    h = x * lax.rsqrt(ms + EPS) * g
    return h * (1.0 + scale) + shift


def _with_ones(v_t):
    return jnp.concatenate([v_t, jnp.ones((ONES_ROWS, v_t.shape[1]), v_t.dtype)], axis=0)


def _mod_kernel(c_ref, w_ref, b_ref, o_ref):
    c = c_ref[...]
    s = c * jax.nn.sigmoid(c)
    o_ref[0] = jnp.dot(s, w_ref[0], preferred_element_type=F32, precision=lax.Precision.HIGHEST) + b_ref[0]


def _modulation(crows, w_mod, b_mod):
    n_layers, d, d6 = w_mod.shape
    r = crows.shape[0]
    nb = d6 // d
    return pl.pallas_call(
        _mod_kernel,
        grid=(n_layers, nb),
        in_specs=[
            pl.BlockSpec((r, d), lambda l, j: (0, 0)),
            pl.BlockSpec((1, d, d), lambda l, j: (l, 0, j)),
            pl.BlockSpec((1, 1, d), lambda l, j: (l, 0, j)),
        ],
        out_specs=pl.BlockSpec((1, r, d), lambda l, j: (l, 0, j)),
        out_shape=jax.ShapeDtypeStruct((n_layers, r, d6), F32),
        compiler_params=_params(2),
        name="modulation",
    )(crows, w_mod, b_mod.reshape(n_layers, 1, d6))


def _head_rms(t, bd):
    ms = jnp.dot((t * t).astype(BF16), bd, preferred_element_type=F32)
    return t * lax.rsqrt(ms + EPS)


def _rope(t, cos, sin_signed, first_half):
    fwd = pltpu.roll(t, LANES - 16, axis=1)
    bwd = pltpu.roll(t, 16, axis=1)
    return t * cos + jnp.where(first_half, fwd, bwd) * sin_signed


def _inproj_kernel(x_ref, sh_ref, sc_ref, g_ref, w_ref, wvt_ref, qg_ref, kg_ref, bd_ref, cos_ref, sin_ref,
                   u_ref, naq_ref, nak_ref, gq_ref, gk_ref, navt_ref, gvt_ref, *, rope):
    h = _norm_modulate(x_ref[...], g_ref[...], sh_ref[0], sc_ref[0]).astype(BF16)

    def proj(lo, hi):
        return jnp.dot(h, w_ref[:, lo:hi], preferred_element_type=F32)

    q_scale = SCALE * LOG2E if rope else SCALE
    bd = bd_ref[...]
    if rope:
        cos = cos_ref[...]
        sin = sin_ref[...]
        lane = lax.broadcasted_iota(jnp.int32, cos.shape, 1)
        first_half = (lane % 32) < 16

    def finish(t, gain):
        t = _head_rms(t, bd) * gain
        if rope:
            t = _rope(t, cos, sin, first_half)
        return t

    q = proj(_P_GQ, _P_GK)
    for j in range(GQA_Q_W // LANES):
        qj = finish(q[:, j * LANES:(j + 1) * LANES], qg_ref[...])
        gq_ref[:, j * LANES:(j + 1) * LANES] = (qj * q_scale).astype(BF16)
    gk_ref[...] = finish(proj(_P_GK, _P_END), kg_ref[...]).astype(BF16)

    a = proj(_P_GLU, _P_NAQ)
    u_ref[...] = (a[:, :CONV_W] * jax.nn.sigmoid(a[:, CONV_W:])).astype(BF16)
    naq_ref[...] = (proj(_P_NAQ, _P_NAK) * q_scale).astype(BF16)
    nak_ref[...] = proj(_P_NAK, _P_GQ).astype(BF16)

    v_t = lax.dot_general(wvt_ref[...], h, (((1,), (1,)), ((), ())), preferred_element_type=F32).astype(BF16)
    for c in range(navt_ref.shape[1]):
        navt_ref[0, c] = v_t[:NA_W, c * KEY_CHUNK:(c + 1) * KEY_CHUNK]
        gvt_ref[0, c] = v_t[NA_W:, c * KEY_CHUNK:(c + 1) * KEY_CHUNK]


def _inproj(x2, mod3, norm_g, w, w_vt, qg2, kg2, bd, cos_t, sin_t, *, n_batch, seq, tm, rope):
    nt, d = x2.shape
    tps = seq // tm
    cpt = tm // KEY_CHUNK
    tok = lambda s, b: (b * tps + s, 0)
    outs = [CONV_W, NA_W, NA_W, GQA_Q_W, GQA_KV_W]
    vt_shape = lambda w_: jax.ShapeDtypeStruct((n_batch, seq // KEY_CHUNK, w_, KEY_CHUNK), BF16)
    vt_spec = lambda w_: pl.BlockSpec((1, cpt, w_, KEY_CHUNK), lambda s, b: (b, s, 0, 0))
    return pl.pallas_call(
        functools.partial(_inproj_kernel, rope=rope),
        grid=(tps, n_batch),
        in_specs=[
            pl.BlockSpec((tm, d), tok),
            pl.BlockSpec((1, 1, d), lambda s, b: (b * 6 + 0, 0, 0)),
            pl.BlockSpec((1, 1, d), lambda s, b: (b * 6 + 1, 0, 0)),
            _const_spec((1, d)),
            _const_spec(w.shape),
            _const_spec(w_vt.shape),
            _const_spec((1, LANES)),
            _const_spec((1, LANES)),
            _const_spec((LANES, LANES)),
            pl.BlockSpec((tm, LANES), lambda s, b: (s, 0)),
            pl.BlockSpec((tm, LANES), lambda s, b: (s, 0)),
        ],
        out_specs=[pl.BlockSpec((tm, n), tok) for n in outs] + [vt_spec(NA_W), vt_spec(GQA_KV_W)],
        out_shape=[jax.ShapeDtypeStruct((nt, n), BF16) for n in outs] + [vt_shape(NA_W), vt_shape(GQA_KV_W)],
        compiler_params=_params(2),
        name="inproj_rope" if rope else "inproj_ctx",
    )(x2, mod3, mod3, norm_g, w, w_vt, qg2, kg2, bd, cos_t, sin_t)


def _conv_ln_swish(prev_ref, cur_ref, next_ref, w_ref, b_ref, lg_ref, lb_ref, win_ref, shift_ref, out_ref, first,
                   last):
    tl = cur_ref.shape[0]
    win_ref[0:CONV_HALO, :] = jnp.where(first, 0.0, prev_ref[...].astype(F32))
    win_ref[CONV_HALO:CONV_HALO + tl, :] = cur_ref[...].astype(F32)
    win_ref[CONV_HALO + tl:, :] = jnp.where(last, 0.0, next_ref[...].astype(F32))
    base = CONV_HALO - CONV_K // 2
    for r in range(SUBLANES):
        n_shift = tl + SUBLANES * (len(range(r, CONV_K, SUBLANES)) - 1)
        shift_ref[r, 0:n_shift, :] = win_ref[base + r:base + r + n_shift, :]
    for rb in range(0, tl, CONV_ROWS):
        acc = jnp.zeros((CONV_ROWS, CONV_W), F32)
        for k in range(CONV_K):
            off = rb + SUBLANES * (k // SUBLANES)
            acc = acc + shift_ref[k % SUBLANES, off:off + CONV_ROWS, :] * w_ref[k:k + 1, :]
        y = acc + b_ref[...]
        mu = jnp.mean(y, axis=-1, keepdims=True)
        yc = y - mu
        var = jnp.mean(yc * yc, axis=-1, keepdims=True)
        z = yc * lax.rsqrt(var + EPS) * lg_ref[...] + lb_ref[...]
        out_ref[rb:rb + CONV_ROWS, :] = (z * jax.nn.sigmoid(z)).astype(out_ref.dtype)


def _conv_halo_specs(n_tokens, tl):
    hb = tl // CONV_HALO
    last_hb = n_tokens // CONV_HALO - 1
    return [
        pl.BlockSpec((CONV_HALO, CONV_W), lambda i: (jnp.maximum(i * hb - 1, 0), 0)),
        pl.BlockSpec((tl, CONV_W), lambda i: (i, 0)),
        pl.BlockSpec((CONV_HALO, CONV_W), lambda i: (jnp.minimum((i + 1) * hb, last_hb), 0)),
    ]


def _na_bias_tables(rpb, rows):
    cols = np.arange(GRID_W)
    win0 = np.clip(cols - NA_KW // 2, 0, GRID_W - NA_KW)
    col_ok = (cols[:, None] >= win0[None, :]) & (cols[:, None] < win0[None, :] + NA_KW)
    pad = GRID_W - NA_KW
    rp = jnp.pad(rpb * LOG2E, ((0, 0), (0, 0), (pad, pad)))
    tiles = jnp.stack([rp[:, :, GRID_W - 1 - qc:2 * GRID_W - 1 - qc] for qc in range(GRID_W)], axis=-1)
    tiles = jnp.where(jnp.asarray(col_ok)[None, None], tiles, NEG * LOG2E)
    masked = jnp.full(tiles.shape[:1] + tiles.shape[2:], NEG * LOG2E, F32)
    n_groups = rows // NA_QROWS
    variants = []
    for g in (0, min(1, n_groups - 1), n_groups - 1):
        w = int(np.clip(NA_QROWS * g - NA_KH // 2, 0, rows - NA_WROWS))
        blocks = []
        for kl in range(NA_WROWS):
            kr = w + kl
            row = []
            for ql in range(NA_QROWS):
                r = NA_QROWS * g + ql
                r0 = int(np.clip(r - NA_KH // 2, 0, rows - NA_KH))
                row.append(tiles[:, kr - r + NA_KH - 1] if r0 <= kr < r0 + NA_KH else masked)
            blocks.append(jnp.concatenate(row, axis=-1))
        variants.append(jnp.concatenate(blocks, axis=1))
    return jnp.stack(variants)


def _na_kernel(q_ref, k_ref, vt_ref, kc_ref, vct_ref, bias_ref, o_ref, s_ref, ot_ref, *, n_chunks):
    g = pl.program_id(1)
    nq = NA_QROWS * GRID_W
    win_chunks = NA_WROWS * GRID_W // KEY_CHUNK
    c0 = jnp.clip(g - 1, 0, n_chunks - win_chunks)
    start = pl.multiple_of(c0 * KEY_CHUNK, KEY_CHUNK)
    q_t = q_ref[...].astype(F32).T.astype(BF16)
    zeros = jnp.zeros((HEAD_DIM, nq), BF16)
    units = [(h, t) for t in range(win_chunks + 1) for h in range(NA_HEADS)]

    def scores(h, t, s_ref):
        hp, j = divmod(h, LANES // HEAD_DIM)
        qh = q_t[h * HEAD_DIM:(h + 1) * HEAD_DIM]
        qz = jnp.concatenate([qh, zeros] if j == 0 else [zeros, qh], axis=0)
        pair = slice(hp * LANES, (hp + 1) * LANES)
        if t < win_chunks:
            rows_k = pl.ds(pl.multiple_of(start + t * KEY_CHUNK, KEY_CHUNK), KEY_CHUNK)
            s_ref[...] = (jnp.dot(k_ref[rows_k, pair], qz, preferred_element_type=F32)
                          + bias_ref[0, h, t * KEY_CHUNK:(t + 1) * KEY_CHUNK, :])
        else:
            s_ref[...] = jnp.dot(kc_ref[:, pair], qz, preferred_element_type=F32)

    def softmax_pv(h, t, s_ref, state):
        rows_h = slice(h * HEAD_DIM, (h + 1) * HEAD_DIM)
        v_t = vt_ref[0, c0 + t, rows_h, :] if t < win_chunks else vct_ref[0, 0, rows_h, :]
        s = s_ref[...]
        m_new = jnp.max(s, axis=0, keepdims=True)
        if state is not None:
            m_old, acc_old = state
            m_new = jnp.maximum(m_old, m_new)
        p = jnp.exp2(s - m_new).astype(BF16)
        acc = jnp.dot(_with_ones(v_t), p, preferred_element_type=F32)
        if state is not None:
            acc = acc + acc_old * jnp.exp2(m_old - m_new)
        return m_new, acc

    n_buf = NA_AHEAD + 1
    for u in range(NA_AHEAD):
        scores(*units[u], s_ref.at[u % n_buf])
    state = [None] * NA_HEADS
    for u, (h, t) in enumerate(units):
        if u + NA_AHEAD < len(units):
            scores(*units[u + NA_AHEAD], s_ref.at[(u + NA_AHEAD) % n_buf])
        state[h] = softmax_pv(h, t, s_ref.at[u % n_buf], state[h])
        if t == win_chunks:
            acc = state[h][1]
            ot_ref[h * HEAD_DIM:(h + 1) * HEAD_DIM, :] = acc[:HEAD_DIM] * (1.0 / acc[HEAD_DIM:HEAD_DIM + 1])
    o_ref[...] = ot_ref[...].T.astype(BF16)


def _na_attention(q, k, vt, kc, vct, bias, *, n_batch, seq, ctx_len):
    rows = seq // GRID_W
    n_groups = rows // NA_QROWS
    nq = NA_QROWS * GRID_W
    variant = lambda g: jnp.where(g == 0, 0, jnp.where(g == n_groups - 1, 2, 1))
    return pl.pallas_call(
        functools.partial(_na_kernel, n_chunks=seq // KEY_CHUNK),
        grid=(n_batch, n_groups),
        in_specs=[
            pl.BlockSpec((nq, NA_W), lambda b, g: (b * n_groups + g, 0)),
            pl.BlockSpec((seq, NA_W), lambda b, g: (b, 0)),
            pl.BlockSpec((1,) + vt.shape[1:], lambda b, g: (b, 0, 0, 0)),
            pl.BlockSpec((ctx_len, NA_W), lambda b, g: (b, 0)),
            pl.BlockSpec((1,) + vct.shape[1:], lambda b, g: (b, 0, 0, 0)),
            pl.BlockSpec((1,) + bias.shape[1:], lambda b, g: (variant(g), 0, 0, 0)),
        ],
        out_specs=pl.BlockSpec((nq, NA_W), lambda b, g: (b * n_groups + g, 0)),
        out_shape=jax.ShapeDtypeStruct(q.shape, BF16),
        scratch_shapes=[
            pltpu.VMEM((NA_AHEAD + 1, KEY_CHUNK, nq), F32),
            pltpu.VMEM((NA_W, nq), F32),
        ],
        compiler_params=_params(2),
        name="na_attention",
    )(q, k, vt, kc, vct, bias)


def _gqa_kernel(q_ref, k_ref, kc_ref, vt_ref, vct_ref, o_ref, qt_ref, s_ref, m_ref, acc_ref, *, n_chunks):
    tq = q_ref.shape[0]
    q_t = q_ref[...].astype(F32).T
    for hk in range(GQA_KV_HEADS):
        for gi in range(GQA_GROUP):
            hq = hk * GQA_GROUP + gi
            qt_ref[hk, :, gi * tq:(gi + 1) * tq] = q_t[hq * HEAD_DIM:(hq + 1) * HEAD_DIM, :].astype(BF16)
    m_ref[...] = jnp.full(m_ref.shape, NEG, F32)
    acc_ref[...] = jnp.zeros(acc_ref.shape, F32)

    def scores(k_blk, s_buf):
        for hk in range(GQA_KV_HEADS):
            kch = k_blk[:, hk * HEAD_DIM:(hk + 1) * HEAD_DIM]
            s_buf[hk] = jnp.dot(kch, qt_ref[hk], preferred_element_type=F32)

    def softmax_pv(v_t, s_ref):
        for hk in range(GQA_KV_HEADS):
            s = s_ref[hk]
            m_old = m_ref[hk]
            m_new = jnp.maximum(m_old, jnp.max(s, axis=0, keepdims=True))
            p = jnp.exp2(s - m_new).astype(BF16)
            pv = jnp.dot(_with_ones(v_t[hk * HEAD_DIM:(hk + 1) * HEAD_DIM]), p, preferred_element_type=F32)
            acc_ref[hk] = acc_ref[hk] * jnp.exp2(m_old - m_new) + pv
            m_ref[hk] = m_new

    def k_rows(c):
        if isinstance(c, int) and c == n_chunks:
            return kc_ref[...]
        return k_ref[pl.ds(pl.multiple_of(c * KEY_CHUNK, KEY_CHUNK), KEY_CHUNK), :]

    def v_cols(c):
        return vct_ref[0, 0] if isinstance(c, int) and c == n_chunks else vt_ref[0, c]

    n_buf = GQA_AHEAD + 1
    for c in range(GQA_AHEAD):
        scores(k_rows(c), s_ref.at[c % n_buf])

    def steps(c0, n):
        for j in range(n):
            c = c0 + j
            if not (isinstance(c, int) and c + GQA_AHEAD > n_chunks):
                scores(k_rows(c + GQA_AHEAD), s_ref.at[(j + GQA_AHEAD) % n_buf])
            softmax_pv(v_cols(c), s_ref.at[j % n_buf])

    trips = (n_chunks - GQA_AHEAD) // GQA_UNROLL

    def body(i, carry):
        steps(i * GQA_UNROLL, GQA_UNROLL)
        return carry

    lax.fori_loop(0, trips, body, 0)
    steps(trips * GQA_UNROLL, n_chunks + 1 - trips * GQA_UNROLL)

    heads_t = []
    for hk in range(GQA_KV_HEADS):
        acc = acc_ref[hk]
        out_t = acc[:HEAD_DIM] * (1.0 / acc[HEAD_DIM:HEAD_DIM + 1])
        heads_t += [out_t[:, gi * tq:(gi + 1) * tq] for gi in range(GQA_GROUP)]
    o_ref[...] = jnp.concatenate(heads_t, axis=0).T.astype(BF16)


def _gqa_attention(q, k, kc, vt, vct, *, n_batch, sq, ctx_len, tq):
    tpq = sq // tq
    m = GQA_GROUP * tq
    n_chunks = sq // KEY_CHUNK
    assert n_chunks >= GQA_AHEAD and GQA_UNROLL % (GQA_AHEAD + 1) == 0 and ctx_len == KEY_CHUNK
    return pl.pallas_call(
        functools.partial(_gqa_kernel, n_chunks=n_chunks),
        grid=(n_batch, tpq),
        in_specs=[
            pl.BlockSpec((tq, GQA_Q_W), lambda b, i: (b * tpq + i, 0)),
            pl.BlockSpec((sq, GQA_KV_W), lambda b, i: (b, 0)),
            pl.BlockSpec((ctx_len, GQA_KV_W), lambda b, i: (b, 0)),
            pl.BlockSpec((1,) + vt.shape[1:], lambda b, i: (b, 0, 0, 0)),
            pl.BlockSpec((1,) + vct.shape[1:], lambda b, i: (b, 0, 0, 0)),
        ],
        out_specs=pl.BlockSpec((tq, GQA_Q_W), lambda b, i: (b * tpq + i, 0)),
        out_shape=jax.ShapeDtypeStruct(q.shape, BF16),
        scratch_shapes=[
            pltpu.VMEM((GQA_KV_HEADS, HEAD_DIM, m), BF16),
            pltpu.VMEM((GQA_AHEAD + 1, GQA_KV_HEADS, KEY_CHUNK, m), F32),
            pltpu.VMEM((GQA_KV_HEADS, 1, m), F32),
            pltpu.VMEM((GQA_KV_HEADS, HEAD_DIM + ONES_ROWS, m), F32),
        ],
        compiler_params=_params(2),
        name="gqa_attention",
    )(q, k, kc, vt, vct)


def _ctx_attn_kernel(q_ref, k_ref, vt_ref, o_ref, ot_ref, *, n_kv, group):
    tq = q_ref.shape[0]
    q_t = q_ref[...].astype(F32).T.astype(BF16)
    for hk in range(n_kv):
        rows_k = slice(hk * HEAD_DIM, (hk + 1) * HEAD_DIM)
        qs = jnp.concatenate([q_t[(hk * group + gi) * HEAD_DIM:(hk * group + gi + 1) * HEAD_DIM]
                              for gi in range(group)], axis=1)
        s = jnp.dot(k_ref[:, rows_k], qs, preferred_element_type=F32)
        p = jnp.exp(s - jnp.max(s, axis=0, keepdims=True)).astype(BF16)
        acc = jnp.dot(_with_ones(vt_ref[0, 0, rows_k, :]), p, preferred_element_type=F32)
        out_t = acc[:HEAD_DIM] * (1.0 / acc[HEAD_DIM:HEAD_DIM + 1])
        for gi in range(group):
            hq = hk * group + gi
            ot_ref[hq * HEAD_DIM:(hq + 1) * HEAD_DIM, :] = out_t[:, gi * tq:(gi + 1) * tq]
    o_ref[...] = ot_ref[...].T.astype(BF16)


def _ctx_attention(q, k, vt, *, n_batch, ctx_len, n_kv, group):
    return pl.pallas_call(
        functools.partial(_ctx_attn_kernel, n_kv=n_kv, group=group),
        grid=(n_batch,),
        in_specs=[
            pl.BlockSpec((ctx_len, q.shape[1]), lambda b: (b, 0)),
            pl.BlockSpec((ctx_len, k.shape[1]), lambda b: (b, 0)),
            pl.BlockSpec((1,) + vt.shape[1:], lambda b: (b, 0, 0, 0)),
        ],
        out_specs=pl.BlockSpec((ctx_len, q.shape[1]), lambda b: (b, 0)),
        out_shape=jax.ShapeDtypeStruct(q.shape, BF16),
        scratch_shapes=[pltpu.VMEM((q.shape[1], ctx_len), F32)],
        compiler_params=_params(1),
        name=f"ctx_attention_g{group}",
    )(q, k, vt)


def _merge_kernel(x_ref, sh_ref, sc_ref, gate_ref, g_ref, wg_ref, uprev_ref, u_ref, unext_ref, cw_ref, cb_ref,
                  lg_ref, lb_ref, b_ref, c_ref, wa_ref, wb_ref, wc_ref, wo_ref, o_ref, win_ref, shift_ref, bra_ref,
                  *, tiles_per_seq):
    pos = pl.program_id(0) % tiles_per_seq
    x = x_ref[...]
    d = x.shape[1]
    h = _norm_modulate(x, g_ref[...], sh_ref[0], sc_ref[0]).astype(BF16)
    def gated(j, br, w_ref):
        gate = jax.nn.sigmoid(jnp.dot(h, wg_ref[:, j * d:(j + 1) * d], preferred_element_type=F32))
        return gate * jnp.dot(br, w_ref[...], preferred_element_type=F32)

    y = gated(1, b_ref[...], wb_ref) + gated(2, c_ref[...], wc_ref)
    gate_a = jax.nn.sigmoid(jnp.dot(h, wg_ref[:, 0:d], preferred_element_type=F32))
    _conv_ln_swish(uprev_ref, u_ref, unext_ref, cw_ref, cb_ref, lg_ref, lb_ref, win_ref, shift_ref, bra_ref,
                   pos == 0, pos == tiles_per_seq - 1)
    y = y + gate_a * jnp.dot(bra_ref[...], wa_ref[...], preferred_element_type=F32)
    o_ref[...] = x + gate_ref[0] * jnp.dot(y.astype(BF16), wo_ref[...], preferred_element_type=F32)


def _merge(x2, mod3, norm_g, w_gates, u, conv_w, conv_b, ln_g, ln_b, br_b, br_c, wa, wb, wc, wo, *, n_batch, seq,
           tm):
    nt, d = x2.shape
    tps = seq // tm
    tok = lambda i: (i, 0)
    mrow = lambda j: (lambda i: ((i // tps) * 6 + j, 0, 0))
    return pl.pallas_call(
        functools.partial(_merge_kernel, tiles_per_seq=tps),
        grid=(nt // tm,),
        in_specs=[
            pl.BlockSpec((tm, d), tok),
            pl.BlockSpec((1, 1, d), mrow(0)),
            pl.BlockSpec((1, 1, d), mrow(1)),
            pl.BlockSpec((1, 1, d), mrow(2)),
            _const_spec((1, d)),
            _const_spec(w_gates.shape),
            *_conv_halo_specs(nt, tm),
            _const_spec((CONV_K, CONV_W)),
            _const_spec((1, CONV_W)),
            _const_spec((1, CONV_W)),
            _const_spec((1, CONV_W)),
            pl.BlockSpec((tm, NA_W), tok),
            pl.BlockSpec((tm, GQA_Q_W), tok),
            _const_spec(wa.shape),
            _const_spec(wb.shape),
            _const_spec(wc.shape),
            _const_spec(wo.shape),
        ],
        out_specs=pl.BlockSpec((tm, d), tok),
        out_shape=jax.ShapeDtypeStruct((nt, d), F32),
        scratch_shapes=[pltpu.VMEM((tm + 2 * CONV_HALO, CONV_W), F32),
                        pltpu.VMEM((SUBLANES, tm + 2 * CONV_HALO, CONV_W), F32),
                        pltpu.VMEM((tm, CONV_W), BF16)],
        compiler_params=_params(1),
        name="merge",
    )(x2, mod3, mod3, mod3, norm_g, w_gates, u, u, u, conv_w, conv_b, ln_g, ln_b, br_b, br_c, wa, wb, wc, wo)


def _ffn_kernel(x_ref, sh_ref, sc_ref, gate_ref, g_ref, wi_ref, wo_ref, fg_ref, o_ref, *, n_split, final_norm):
    x = x_ref[...]
    hidden = wo_ref.shape[0]
    h = _norm_modulate(x, g_ref[...], sh_ref[0], sc_ref[0]).astype(BF16)
    step = hidden // n_split
    y = None
    for j in range(n_split):
        gt = jnp.dot(h, wi_ref[:, j * step:(j + 1) * step], preferred_element_type=F32)
        up = jnp.dot(h, wi_ref[:, hidden + j * step:hidden + (j + 1) * step], preferred_element_type=F32)
        act = (gt * jax.nn.sigmoid(gt) * up).astype(BF16)
        t = jnp.dot(act, wo_ref[j * step:(j + 1) * step, :], preferred_element_type=F32)
        y = t if y is None else y + t
    out = x + gate_ref[0] * y
    if final_norm:
        ms = jnp.mean(out * out, axis=-1, keepdims=True)
        out = out * lax.rsqrt(ms + EPS) * fg_ref[...]
    o_ref[...] = out


def _ffn(x2, mod3, norm_g, w_in, w_out, final_g, *, n_batch, seq, tm, final_norm):
    nt, d = x2.shape
    tps = seq // tm
    tok = lambda i: (i, 0)
    mrow = lambda j: (lambda i: ((i // tps) * 6 + j, 0, 0))
    return pl.pallas_call(
        functools.partial(_ffn_kernel, n_split=1, final_norm=final_norm),
        grid=(nt // tm,),
        in_specs=[
            pl.BlockSpec((tm, d), tok),
            pl.BlockSpec((1, 1, d), mrow(3)),
            pl.BlockSpec((1, 1, d), mrow(4)),
            pl.BlockSpec((1, 1, d), mrow(5)),
            _const_spec((1, d)),
            _const_spec(w_in.shape),
            _const_spec(w_out.shape),
            _const_spec((1, d)),
        ],
        out_specs=pl.BlockSpec((tm, d), tok),
        out_shape=jax.ShapeDtypeStruct((nt, d), F32),
        compiler_params=_params(1),
        name="ffn_final" if final_norm else "ffn",
    )(x2, mod3, mod3, mod3, norm_g, w_in, w_out, final_g)


def _rope_tables(seq):
    t = jnp.arange(seq)
    pos_row = (t // GRID_W).astype(F32)
    pos_col = (t % GRID_W).astype(F32)
    half = HEAD_DIM // 2
    freqs = jnp.power(ROPE_THETA, -jnp.arange(0, half, 2, dtype=F32) / half)

    def part(pos):
        ang = pos[:, None] * freqs
        c, s = jnp.cos(ang), jnp.sin(ang)
        return jnp.concatenate([c, c], axis=-1), jnp.concatenate([-s, s], axis=-1)

    cr, sr = part(pos_row)
    cc, sc = part(pos_col)
    cos = jnp.concatenate([cr, cc], axis=-1)
    sin = jnp.concatenate([sr, sc], axis=-1)
    return jnp.tile(cos, (1, LANES // HEAD_DIM)), jnp.tile(sin, (1, LANES // HEAD_DIM))


def kernel(x, c, ctx, c_ctx, w_mod, b_mod, norm1_g, norm2_g, w_in, conv_w, conv_b, conv_ln_g, conv_ln_b,
           w_conv_out, na_rpb, w_na_out, q_norm_g, k_norm_g, w_gqa_out, w_out, w_ffn_in, w_ffn_out, final_g):
    n_batch, seq, d = x.shape
    ctx_len = ctx.shape[1]
    depth = w_mod.shape[0]
    rows = seq // GRID_W
    assert seq % (NA_QROWS * GRID_W) == 0 and rows >= NA_WROWS and ctx_len == KEY_CHUNK
    tm = 512
    tmc = ctx_len

    n_rows = -(-(n_batch + 1) // 8) * 8
    crows = jnp.zeros((n_rows, d), F32).at[:n_batch].set(c).at[n_batch].set(c_ctx)
    mod = _modulation(crows, w_mod, b_mod)

    cos_t, sin_t = _rope_tables(seq)
    bd = jnp.asarray(np.kron(np.eye(LANES // HEAD_DIM), np.full((HEAD_DIM, HEAD_DIM), 1.0 / HEAD_DIM)), BF16)
    tile2 = lambda g: jnp.tile(g, LANES // HEAD_DIM).reshape(1, LANES)
    row = lambda g: g.reshape(1, -1)

    xl = x.reshape(n_batch * seq, d)
    xc = ctx.reshape(n_batch * ctx_len, d)
    for l in range(depth):
        last = l == depth - 1
        wl = w_in[l]
        w_proj = jnp.concatenate([wl[:, _IN_GLU:_IN_NAV], wl[:, _IN_GQ:_IN_GV]], axis=1).astype(BF16)
        w_vt = jnp.concatenate([wl[:, _IN_NAV:_IN_GQ], wl[:, _IN_GV:_IN_GATES]], axis=1).T.astype(BF16)
        w_gates = wl[:, _IN_GATES:].astype(BF16)
        mod_l = mod[l, :n_batch].reshape(n_batch * 6, 1, d)
        mod_c = jnp.broadcast_to(mod[l, n_batch], (n_batch, 6 * d)).reshape(n_batch * 6, 1, d)
        qg2, kg2 = tile2(q_norm_g[l]), tile2(k_norm_g[l])
        g1, g2 = row(norm1_g[l]), row(norm2_g[l])
        wa, wb, wc = w_conv_out[l].astype(BF16), w_na_out[l].astype(BF16), w_gqa_out[l].astype(BF16)
        wo = w_out[l].astype(BF16)
        wfi, wfo = w_ffn_in[l].astype(BF16), w_ffn_out[l].astype(BF16)
        conv_args = (conv_w[l], row(conv_b[l]), row(conv_ln_g[l]), row(conv_ln_b[l]))

        u, naq, nak, gq, gk, navt, gvt = _inproj(xl, mod_l, g1, w_proj, w_vt, qg2, kg2, bd, cos_t, sin_t,
                                                 n_batch=n_batch, seq=seq, tm=tm, rope=True)
        uc, naqc, nakc, gqc, gkc, navct, gvct = _inproj(xc, mod_c, g1, w_proj, w_vt, qg2, kg2, bd,
                                                        cos_t[:ctx_len], sin_t[:ctx_len],
                                                        n_batch=n_batch, seq=ctx_len, tm=tmc, rope=False)

        bias = _na_bias_tables(na_rpb[l], rows)
        br_b = _na_attention(naq, nak, navt, nakc, navct, bias, n_batch=n_batch, seq=seq, ctx_len=ctx_len)
        br_c = _gqa_attention(gq, gk, gkc, gvt, gvct, n_batch=n_batch, sq=seq, ctx_len=ctx_len, tq=128)
        xl = _merge(xl, mod_l, g1, w_gates, u, *conv_args, br_b, br_c, wa, wb, wc, wo,
                    n_batch=n_batch, seq=seq, tm=tm)
        xl = _ffn(xl, mod_l, g2, wfi, wfo, row(final_g), n_batch=n_batch, seq=seq, tm=tm, final_norm=last)

        if not last:
            cbr_b = _ctx_attention(naqc, nakc, navct, n_batch=n_batch, ctx_len=ctx_len, n_kv=NA_HEADS, group=1)
            cbr_c = _ctx_attention(gqc, gkc, gvct, n_batch=n_batch, ctx_len=ctx_len, n_kv=GQA_KV_HEADS,
                                   group=GQA_GROUP)
            xc = _merge(xc, mod_c, g1, w_gates, uc, *conv_args, cbr_b, cbr_c, wa, wb, wc, wo,
                        n_batch=n_batch, seq=ctx_len, tm=tmc)
            xc = _ffn(xc, mod_c, g2, wfi, wfo, row(final_g), n_batch=n_batch, seq=ctx_len, tm=tmc,
                      final_norm=False)
    return xl.reshape(n_batch, seq, d)
```

```python
import functools

import numpy as np
import jax
import jax.numpy as jnp
from jax import lax
from jax.experimental import pallas as pl
from jax.experimental.pallas import tpu as pltpu

F32 = jnp.float32
BF16 = jnp.bfloat16

GRID_W = 64
HEAD_DIM = 64
CONV_W = 512
CONV_K = 31
CONV_HALO = 16
CONV_ROWS = 64
NA_HEADS = 8
NA_W = NA_HEADS * HEAD_DIM
NA_KH = 8
NA_KW = 16
NA_QROWS = 4
NA_WROWS = 12
NA_AHEAD = 8
GQA_Q_HEADS = 8
GQA_KV_HEADS = 2
GQA_GROUP = GQA_Q_HEADS // GQA_KV_HEADS
GQA_Q_W = GQA_Q_HEADS * HEAD_DIM
GQA_KV_W = GQA_KV_HEADS * HEAD_DIM
GQA_AHEAD = 2
GQA_UNROLL = 6
ROPE_THETA = 10000.0
EPS = 1e-6
NEG = -1e30
SCALE = HEAD_DIM ** -0.5
LOG2E = 1.4426950408889634
LANES = 128
SUBLANES = 8
KEY_CHUNK = 256
ONES_ROWS = 16
VMEM_LIMIT = 56 * 1024 * 1024

_IN_GLU = 0
_IN_NAQ = _IN_GLU + 2 * CONV_W
_IN_NAK = _IN_NAQ + NA_W
_IN_NAV = _IN_NAK + NA_W
_IN_GQ = _IN_NAV + NA_W
_IN_GK = _IN_GQ + GQA_Q_W
_IN_GV = _IN_GK + GQA_KV_W
_IN_GATES = _IN_GV + GQA_KV_W
_P_GLU = 0
_P_NAQ = _P_GLU + 2 * CONV_W
_P_NAK = _P_NAQ + NA_W
_P_GQ = _P_NAK + NA_W
_P_GK = _P_GQ + GQA_Q_W
_P_END = _P_GK + GQA_KV_W


def _params(n_axes):
    return pltpu.CompilerParams(dimension_semantics=("arbitrary",) * n_axes, vmem_limit_bytes=VMEM_LIMIT)


def _const_spec(shape):
    zeros = (0,) * len(shape)
    return pl.BlockSpec(shape, lambda *_: zeros, pipeline_mode=pl.Buffered(1))


def _norm_modulate(x, g, shift, scale):
    ms = jnp.mean(x * x, axis=-1, keepdims=True)
    h = x * lax.rsqrt(ms + EPS) * g
    return h * (1.0 + scale) + shift


def _with_ones(v_t):
    return jnp.concatenate([v_t, jnp.ones((ONES_ROWS, v_t.shape[1]), v_t.dtype)], axis=0)


def _mod_kernel(c_ref, w_ref, b_ref, o_ref):
    c = c_ref[...]
    s = c * jax.nn.sigmoid(c)
    o_ref[0] = jnp.dot(s, w_ref[0], preferred_element_type=F32, precision=lax.Precision.HIGHEST) + b_ref[0]


def _modulation(crows, w_mod, b_mod):
    n_layers, d, d6 = w_mod.shape
    r = crows.shape[0]
    nb = d6 // d
    return pl.pallas_call(
        _mod_kernel,
        grid=(n_layers, nb),
        in_specs=[
            pl.BlockSpec((r, d), lambda l, j: (0, 0)),
            pl.BlockSpec((1, d, d), lambda l, j: (l, 0, j)),
            pl.BlockSpec((1, 1, d), lambda l, j: (l, 0, j)),
        ],
        out_specs=pl.BlockSpec((1, r, d), lambda l, j: (l, 0, j)),
        out_shape=jax.ShapeDtypeStruct((n_layers, r, d6), F32),
        compiler_params=_params(2),
        name="modulation",
    )(crows, w_mod, b_mod.reshape(n_layers, 1, d6))


def _head_rms(t, bd):
    ms = jnp.dot((t * t).astype(BF16), bd, preferred_element_type=F32)
    return t * lax.rsqrt(ms + EPS)


def _rope(t, cos, sin_signed, first_half):
    fwd = pltpu.roll(t, LANES - 16, axis=1)
    bwd = pltpu.roll(t, 16, axis=1)
    return t * cos + jnp.where(first_half, fwd, bwd) * sin_signed


def _inproj_kernel(x_ref, sh_ref, sc_ref, g_ref, w_ref, wvt_ref, qg_ref, kg_ref, bd_ref, cos_ref, sin_ref,
                   u_ref, naq_ref, nak_ref, gq_ref, gk_ref, navt_ref, gvt_ref, *, rope):
    h = _norm_modulate(x_ref[...], g_ref[...], sh_ref[0], sc_ref[0]).astype(BF16)

    def proj(lo, hi):
        return jnp.dot(h, w_ref[:, lo:hi], preferred_element_type=F32)

    q_scale = SCALE * LOG2E if rope else SCALE
    bd = bd_ref[...]
    if rope:
        cos = cos_ref[...]
        sin = sin_ref[...]
        lane = lax.broadcasted_iota(jnp.int32, cos.shape, 1)
        first_half = (lane % 32) < 16

    def finish(t, gain):
        t = _head_rms(t, bd) * gain
        if rope:
            t = _rope(t, cos, sin, first_half)
        return t

    q = proj(_P_GQ, _P_GK)
    for j in range(GQA_Q_W // LANES):
        qj = finish(q[:, j * LANES:(j + 1) * LANES], qg_ref[...])
        gq_ref[:, j * LANES:(j + 1) * LANES] = (qj * q_scale).astype(BF16)
    gk_ref[...] = finish(proj(_P_GK, _P_END), kg_ref[...]).astype(BF16)

    a = proj(_P_GLU, _P_NAQ)
    u_ref[...] = (a[:, :CONV_W] * jax.nn.sigmoid(a[:, CONV_W:])).astype(BF16)
    naq_ref[...] = (proj(_P_NAQ, _P_NAK) * q_scale).astype(BF16)
    nak_ref[...] = proj(_P_NAK, _P_GQ).astype(BF16)

    v_t = lax.dot_general(wvt_ref[...], h, (((1,), (1,)), ((), ())), preferred_element_type=F32).astype(BF16)
    for c in range(navt_ref.shape[1]):
        navt_ref[0, c] = v_t[:NA_W, c * KEY_CHUNK:(c + 1) * KEY_CHUNK]
        gvt_ref[0, c] = v_t[NA_W:, c * KEY_CHUNK:(c + 1) * KEY_CHUNK]


def _inproj(x2, mod3, norm_g, w, w_vt, qg2, kg2, bd, cos_t, sin_t, *, n_batch, seq, tm, rope):
    nt, d = x2.shape
    tps = seq // tm
    cpt = tm // KEY_CHUNK
    tok = lambda s, b: (b * tps + s, 0)
    outs = [CONV_W, NA_W, NA_W, GQA_Q_W, GQA_KV_W]
    vt_shape = lambda w_: jax.ShapeDtypeStruct((n_batch, seq // KEY_CHUNK, w_, KEY_CHUNK), BF16)
    vt_spec = lambda w_: pl.BlockSpec((1, cpt, w_, KEY_CHUNK), lambda s, b: (b, s, 0, 0))
    return pl.pallas_call(
        functools.partial(_inproj_kernel, rope=rope),
        grid=(tps, n_batch),
        in_specs=[
            pl.BlockSpec((tm, d), tok),
            pl.BlockSpec((1, 1, d), lambda s, b: (b * 6 + 0, 0, 0)),
            pl.BlockSpec((1, 1, d), lambda s, b: (b * 6 + 1, 0, 0)),
            _const_spec((1, d)),
            _const_spec(w.shape),
            _const_spec(w_vt.shape),
            _const_spec((1, LANES)),
            _const_spec((1, LANES)),
            _const_spec((LANES, LANES)),
            pl.BlockSpec((tm, LANES), lambda s, b: (s, 0)),
            pl.BlockSpec((tm, LANES), lambda s, b: (s, 0)),
        ],
        out_specs=[pl.BlockSpec((tm, n), tok) for n in outs] + [vt_spec(NA_W), vt_spec(GQA_KV_W)],
        out_shape=[jax.ShapeDtypeStruct((nt, n), BF16) for n in outs] + [vt_shape(NA_W), vt_shape(GQA_KV_W)],
        compiler_params=_params(2),
        name="inproj_rope" if rope else "inproj_ctx",
    )(x2, mod3, mod3, norm_g, w, w_vt, qg2, kg2, bd, cos_t, sin_t)


def _conv_ln_swish(prev_ref, cur_ref, next_ref, w_ref, b_ref, lg_ref, lb_ref, win_ref, shift_ref, out_ref, first,
                   last):
    tl = cur_ref.shape[0]
    win_ref[0:CONV_HALO, :] = jnp.where(first, 0.0, prev_ref[...].astype(F32))
    win_ref[CONV_HALO:CONV_HALO + tl, :] = cur_ref[...].astype(F32)
    win_ref[CONV_HALO + tl:, :] = jnp.where(last, 0.0, next_ref[...].astype(F32))
    base = CONV_HALO - CONV_K // 2
    for r in range(SUBLANES):
        n_shift = tl + SUBLANES * (len(range(r, CONV_K, SUBLANES)) - 1)
        shift_ref[r, 0:n_shift, :] = win_ref[base + r:base + r + n_shift, :]
    for rb in range(0, tl, CONV_ROWS):
        acc = jnp.zeros((CONV_ROWS, CONV_W), F32)
        for k in range(CONV_K):
            off = rb + SUBLANES * (k // SUBLANES)
            acc = acc + shift_ref[k % SUBLANES, off:off + CONV_ROWS, :] * w_ref[k:k + 1, :]
        y = acc + b_ref[...]
        mu = jnp.mean(y, axis=-1, keepdims=True)
        yc = y - mu
        var = jnp.mean(yc * yc, axis=-1, keepdims=True)
        z = yc * lax.rsqrt(var + EPS) * lg_ref[...] + lb_ref[...]
        out_ref[rb:rb + CONV_ROWS, :] = (z * jax.nn.sigmoid(z)).astype(out_ref.dtype)


def _conv_halo_specs(n_tokens, tl):
    hb = tl // CONV_HALO
    last_hb = n_tokens // CONV_HALO - 1
    return [
        pl.BlockSpec((CONV_HALO, CONV_W), lambda i: (jnp.maximum(i * hb - 1, 0), 0)),
        pl.BlockSpec((tl, CONV_W), lambda i: (i, 0)),
        pl.BlockSpec((CONV_HALO, CONV_W), lambda i: (jnp.minimum((i + 1) * hb, last_hb), 0)),
    ]


def _na_bias_tables(rpb, rows):
    cols = np.arange(GRID_W)
    win0 = np.clip(cols - NA_KW // 2, 0, GRID_W - NA_KW)
    col_ok = (cols[:, None] >= win0[None, :]) & (cols[:, None] < win0[None, :] + NA_KW)
    pad = GRID_W - NA_KW
    rp = jnp.pad(rpb * LOG2E, ((0, 0), (0, 0), (pad, pad)))
    tiles = jnp.stack([rp[:, :, GRID_W - 1 - qc:2 * GRID_W - 1 - qc] for qc in range(GRID_W)], axis=-1)
    tiles = jnp.where(jnp.asarray(col_ok)[None, None], tiles, NEG * LOG2E)
    masked = jnp.full(tiles.shape[:1] + tiles.shape[2:], NEG * LOG2E, F32)
    n_groups = rows // NA_QROWS
    variants = []
    for g in (0, min(1, n_groups - 1), n_groups - 1):
        w = int(np.clip(NA_QROWS * g - NA_KH // 2, 0, rows - NA_WROWS))
        blocks = []
        for kl in range(NA_WROWS):
            kr = w + kl
            row = []
            for ql in range(NA_QROWS):
                r = NA_QROWS * g + ql
                r0 = int(np.clip(r - NA_KH // 2, 0, rows - NA_KH))
                row.append(tiles[:, kr - r + NA_KH - 1] if r0 <= kr < r0 + NA_KH else masked)
            blocks.append(jnp.concatenate(row, axis=-1))
        variants.append(jnp.concatenate(blocks, axis=1))
    return jnp.stack(variants)


def _na_kernel(q_ref, k_ref, vt_ref, kc_ref, vct_ref, bias_ref, o_ref, s_ref, ot_ref, *, n_chunks):
    g = pl.program_id(1)
    nq = NA_QROWS * GRID_W
    win_chunks = NA_WROWS * GRID_W // KEY_CHUNK
    c0 = jnp.clip(g - 1, 0, n_chunks - win_chunks)
    start = pl.multiple_of(c0 * KEY_CHUNK, KEY_CHUNK)
    q_t = q_ref[...].astype(F32).T.astype(BF16)
    zeros = jnp.zeros((HEAD_DIM, nq), BF16)
    units = [(h, t) for t in range(win_chunks + 1) for h in range(NA_HEADS)]

    def scores(h, t, s_ref):
        hp, j = divmod(h, LANES // HEAD_DIM)
        qh = q_t[h * HEAD_DIM:(h + 1) * HEAD_DIM]
        qz = jnp.concatenate([qh, zeros] if j == 0 else [zeros, qh], axis=0)
        pair = slice(hp * LANES, (hp + 1) * LANES)
        if t < win_chunks:
            rows_k = pl.ds(pl.multiple_of(start + t * KEY_CHUNK, KEY_CHUNK), KEY_CHUNK)
            s_ref[...] = (jnp.dot(k_ref[rows_k, pair], qz, preferred_element_type=F32)
                          + bias_ref[0, h, t * KEY_CHUNK:(t + 1) * KEY_CHUNK, :])
        else:
            s_ref[...] = jnp.dot(kc_ref[:, pair], qz, preferred_element_type=F32)

    def softmax_pv(h, t, s_ref, state):
        rows_h = slice(h * HEAD_DIM, (h + 1) * HEAD_DIM)
        v_t = vt_ref[0, c0 + t, rows_h, :] if t < win_chunks else vct_ref[0, 0, rows_h, :]
        s = s_ref[...]
        m_new = jnp.max(s, axis=0, keepdims=True)
        if state is not None:
            m_old, acc_old = state
            m_new = jnp.maximum(m_old, m_new)
        p = jnp.exp2(s - m_new).astype(BF16)
        acc = jnp.dot(_with_ones(v_t), p, preferred_element_type=F32)
        if state is not None:
            acc = acc + acc_old * jnp.exp2(m_old - m_new)
        return m_new, acc

    n_buf = NA_AHEAD + 1
    for u in range(NA_AHEAD):
        scores(*units[u], s_ref.at[u % n_buf])
    state = [None] * NA_HEADS
    for u, (h, t) in enumerate(units):
        if u + NA_AHEAD < len(units):
            scores(*units[u + NA_AHEAD], s_ref.at[(u + NA_AHEAD) % n_buf])
        state[h] = softmax_pv(h, t, s_ref.at[u % n_buf], state[h])
        if t == win_chunks:
            acc = state[h][1]
            ot_ref[h * HEAD_DIM:(h + 1) * HEAD_DIM, :] = acc[:HEAD_DIM] * (1.0 / acc[HEAD_DIM:HEAD_DIM + 1])
    o_ref[...] = ot_ref[...].T.astype(BF16)


def _na_attention(q, k, vt, kc, vct, bias, *, n_batch, seq, ctx_len):
    rows = seq // GRID_W
    n_groups = rows // NA_QROWS
    nq = NA_QROWS * GRID_W
    variant = lambda g: jnp.where(g == 0, 0, jnp.where(g == n_groups - 1, 2, 1))
    return pl.pallas_call(
        functools.partial(_na_kernel, n_chunks=seq // KEY_CHUNK),
        grid=(n_batch, n_groups),
        in_specs=[
            pl.BlockSpec((nq, NA_W), lambda b, g: (b * n_groups + g, 0)),
            pl.BlockSpec((seq, NA_W), lambda b, g: (b, 0)),
            pl.BlockSpec((1,) + vt.shape[1:], lambda b, g: (b, 0, 0, 0)),
            pl.BlockSpec((ctx_len, NA_W), lambda b, g: (b, 0)),
            pl.BlockSpec((1,) + vct.shape[1:], lambda b, g: (b, 0, 0, 0)),
            pl.BlockSpec((1,) + bias.shape[1:], lambda b, g: (variant(g), 0, 0, 0)),
        ],
        out_specs=pl.BlockSpec((nq, NA_W), lambda b, g: (b * n_groups + g, 0)),
        out_shape=jax.ShapeDtypeStruct(q.shape, BF16),
        scratch_shapes=[
            pltpu.VMEM((NA_AHEAD + 1, KEY_CHUNK, nq), F32),
            pltpu.VMEM((NA_W, nq), F32),
        ],
        compiler_params=_params(2),
        name="na_attention",
    )(q, k, vt, kc, vct, bias)


def _gqa_kernel(q_ref, k_ref, kc_ref, vt_ref, vct_ref, o_ref, qt_ref, s_ref, m_ref, acc_ref, *, n_chunks):
    tq = q_ref.shape[0]
    q_t = q_ref[...].astype(F32).T
    for hk in range(GQA_KV_HEADS):
        for gi in range(GQA_GROUP):
            hq = hk * GQA_GROUP + gi
            qt_ref[hk, :, gi * tq:(gi + 1) * tq] = q_t[hq * HEAD_DIM:(hq + 1) * HEAD_DIM, :].astype(BF16)
    m_ref[...] = jnp.full(m_ref.shape, NEG, F32)
    acc_ref[...] = jnp.zeros(acc_ref.shape, F32)

    def scores(k_blk, s_buf):
        for hk in range(GQA_KV_HEADS):
            kch = k_blk[:, hk * HEAD_DIM:(hk + 1) * HEAD_DIM]
            s_buf[hk] = jnp.dot(kch, qt_ref[hk], preferred_element_type=F32)

    def softmax_pv(v_t, s_ref):
        for hk in range(GQA_KV_HEADS):
            s = s_ref[hk]
            m_old = m_ref[hk]
            m_new = jnp.maximum(m_old, jnp.max(s, axis=0, keepdims=True))
            p = jnp.exp2(s - m_new).astype(BF16)
            pv = jnp.dot(_with_ones(v_t[hk * HEAD_DIM:(hk + 1) * HEAD_DIM]), p, preferred_element_type=F32)
            acc_ref[hk] = acc_ref[hk] * jnp.exp2(m_old - m_new) + pv
            m_ref[hk] = m_new

    def k_rows(c):
        if isinstance(c, int) and c == n_chunks:
            return kc_ref[...]
        return k_ref[pl.ds(pl.multiple_of(c * KEY_CHUNK, KEY_CHUNK), KEY_CHUNK), :]

    def v_cols(c):
        return vct_ref[0, 0] if isinstance(c, int) and c == n_chunks else vt_ref[0, c]

    n_buf = GQA_AHEAD + 1
    for c in range(GQA_AHEAD):
        scores(k_rows(c), s_ref.at[c % n_buf])

    def steps(c0, n):
        for j in range(n):
            c = c0 + j
            if not (isinstance(c, int) and c + GQA_AHEAD > n_chunks):
                scores(k_rows(c + GQA_AHEAD), s_ref.at[(j + GQA_AHEAD) % n_buf])
            softmax_pv(v_cols(c), s_ref.at[j % n_buf])

    trips = (n_chunks - GQA_AHEAD) // GQA_UNROLL

    def body(i, carry):
        steps(i * GQA_UNROLL, GQA_UNROLL)
        return carry

    lax.fori_loop(0, trips, body, 0)
    steps(trips * GQA_UNROLL, n_chunks + 1 - trips * GQA_UNROLL)

    heads_t = []
    for hk in range(GQA_KV_HEADS):
        acc = acc_ref[hk]
        out_t = acc[:HEAD_DIM] * (1.0 / acc[HEAD_DIM:HEAD_DIM + 1])
        heads_t += [out_t[:, gi * tq:(gi + 1) * tq] for gi in range(GQA_GROUP)]
    o_ref[...] = jnp.concatenate(heads_t, axis=0).T.astype(BF16)


def _gqa_attention(q, k, kc, vt, vct, *, n_batch, sq, ctx_len, tq):
    tpq = sq // tq
    m = GQA_GROUP * tq
    n_chunks = sq // KEY_CHUNK
    assert n_chunks >= GQA_AHEAD and GQA_UNROLL % (GQA_AHEAD + 1) == 0 and ctx_len == KEY_CHUNK
    return pl.pallas_call(
        functools.partial(_gqa_kernel, n_chunks=n_chunks),
        grid=(n_batch, tpq),
        in_specs=[
            pl.BlockSpec((tq, GQA_Q_W), lambda b, i: (b * tpq + i, 0)),
            pl.BlockSpec((sq, GQA_KV_W), lambda b, i: (b, 0)),
            pl.BlockSpec((ctx_len, GQA_KV_W), lambda b, i: (b, 0)),
            pl.BlockSpec((1,) + vt.shape[1:], lambda b, i: (b, 0, 0, 0)),
            pl.BlockSpec((1,) + vct.shape[1:], lambda b, i: (b, 0, 0, 0)),
        ],
        out_specs=pl.BlockSpec((tq, GQA_Q_W), lambda b, i: (b * tpq + i, 0)),
        out_shape=jax.ShapeDtypeStruct(q.shape, BF16),
        scratch_shapes=[
            pltpu.VMEM((GQA_KV_HEADS, HEAD_DIM, m), BF16),
            pltpu.VMEM((GQA_AHEAD + 1, GQA_KV_HEADS, KEY_CHUNK, m), F32),
            pltpu.VMEM((GQA_KV_HEADS, 1, m), F32),
            pltpu.VMEM((GQA_KV_HEADS, HEAD_DIM + ONES_ROWS, m), F32),
        ],
        compiler_params=_params(2),
        name="gqa_attention",
    )(q, k, kc, vt, vct)


def _ctx_attn_kernel(q_ref, k_ref, vt_ref, o_ref, ot_ref, *, n_kv, group):
    tq = q_ref.shape[0]
    q_t = q_ref[...].astype(F32).T.astype(BF16)
    for hk in range(n_kv):
        rows_k = slice(hk * HEAD_DIM, (hk + 1) * HEAD_DIM)
        qs = jnp.concatenate([q_t[(hk * group + gi) * HEAD_DIM:(hk * group + gi + 1) * HEAD_DIM]
                              for gi in range(group)], axis=1)
        s = jnp.dot(k_ref[:, rows_k], qs, preferred_element_type=F32)
        p = jnp.exp(s - jnp.max(s, axis=0, keepdims=True)).astype(BF16)
        acc = jnp.dot(_with_ones(vt_ref[0, 0, rows_k, :]), p, preferred_element_type=F32)
        out_t = acc[:HEAD_DIM] * (1.0 / acc[HEAD_DIM:HEAD_DIM + 1])
        for gi in range(group):
            hq = hk * group + gi
            ot_ref[hq * HEAD_DIM:(hq + 1) * HEAD_DIM, :] = out_t[:, gi * tq:(gi + 1) * tq]
    o_ref[...] = ot_ref[...].T.astype(BF16)


def _ctx_attention(q, k, vt, *, n_batch, ctx_len, n_kv, group):
    return pl.pallas_call(
        functools.partial(_ctx_attn_kernel, n_kv=n_kv, group=group),
        grid=(n_batch,),
        in_specs=[
            pl.BlockSpec((ctx_len, q.shape[1]), lambda b: (b, 0)),
            pl.BlockSpec((ctx_len, k.shape[1]), lambda b: (b, 0)),
            pl.BlockSpec((1,) + vt.shape[1:], lambda b: (b, 0, 0, 0)),
        ],
        out_specs=pl.BlockSpec((ctx_len, q.shape[1]), lambda b: (b, 0)),
        out_shape=jax.ShapeDtypeStruct(q.shape, BF16),
        scratch_shapes=[pltpu.VMEM((q.shape[1], ctx_len), F32)],
        compiler_params=_params(1),
        name=f"ctx_attention_g{group}",
    )(q, k, vt)


def _merge_kernel(x_ref, sh_ref, sc_ref, gate_ref, g_ref, wg_ref, uprev_ref, u_ref, unext_ref, cw_ref, cb_ref,
                  lg_ref, lb_ref, b_ref, c_ref, wa_ref, wb_ref, wc_ref, wo_ref, o_ref, win_ref, shift_ref, bra_ref,
                  *, tiles_per_seq):
    pos = pl.program_id(0) % tiles_per_seq
    x = x_ref[...]
    d = x.shape[1]
    h = _norm_modulate(x, g_ref[...], sh_ref[0], sc_ref[0]).astype(BF16)
    def gated(j, br, w_ref):
        gate = jax.nn.sigmoid(jnp.dot(h, wg_ref[:, j * d:(j + 1) * d], preferred_element_type=F32))
        return gate * jnp.dot(br, w_ref[...], preferred_element_type=F32)

    y = gated(1, b_ref[...], wb_ref) + gated(2, c_ref[...], wc_ref)
    gate_a = jax.nn.sigmoid(jnp.dot(h, wg_ref[:, 0:d], preferred_element_type=F32))
    _conv_ln_swish(uprev_ref, u_ref, unext_ref, cw_ref, cb_ref, lg_ref, lb_ref, win_ref, shift_ref, bra_ref,
                   pos == 0, pos == tiles_per_seq - 1)
    y = y + gate_a * jnp.dot(bra_ref[...], wa_ref[...], preferred_element_type=F32)
    o_ref[...] = x + gate_ref[0] * jnp.dot(y.astype(BF16), wo_ref[...], preferred_element_type=F32)


def _merge(x2, mod3, norm_g, w_gates, u, conv_w, conv_b, ln_g, ln_b, br_b, br_c, wa, wb, wc, wo, *, n_batch, seq,
           tm):
    nt, d = x2.shape
    tps = seq // tm
    tok = lambda i: (i, 0)
    mrow = lambda j: (lambda i: ((i // tps) * 6 + j, 0, 0))
    return pl.pallas_call(
        functools.partial(_merge_kernel, tiles_per_seq=tps),
        grid=(nt // tm,),
        in_specs=[
            pl.BlockSpec((tm, d), tok),
            pl.BlockSpec((1, 1, d), mrow(0)),
            pl.BlockSpec((1, 1, d), mrow(1)),
            pl.BlockSpec((1, 1, d), mrow(2)),
            _const_spec((1, d)),
            _const_spec(w_gates.shape),
            *_conv_halo_specs(nt, tm),
            _const_spec((CONV_K, CONV_W)),
            _const_spec((1, CONV_W)),
            _const_spec((1, CONV_W)),
            _const_spec((1, CONV_W)),
            pl.BlockSpec((tm, NA_W), tok),
            pl.BlockSpec((tm, GQA_Q_W), tok),
            _const_spec(wa.shape),
            _const_spec(wb.shape),
            _const_spec(wc.shape),
            _const_spec(wo.shape),
        ],
        out_specs=pl.BlockSpec((tm, d), tok),
        out_shape=jax.ShapeDtypeStruct((nt, d), F32),
        scratch_shapes=[pltpu.VMEM((tm + 2 * CONV_HALO, CONV_W), F32),
                        pltpu.VMEM((SUBLANES, tm + 2 * CONV_HALO, CONV_W), F32),
                        pltpu.VMEM((tm, CONV_W), BF16)],
        compiler_params=_params(1),
        name="merge",
    )(x2, mod3, mod3, mod3, norm_g, w_gates, u, u, u, conv_w, conv_b, ln_g, ln_b, br_b, br_c, wa, wb, wc, wo)


def _ffn_kernel(x_ref, sh_ref, sc_ref, gate_ref, g_ref, wi_ref, wo_ref, fg_ref, o_ref, *, n_split, final_norm):
    x = x_ref[...]
    hidden = wo_ref.shape[0]
    h = _norm_modulate(x, g_ref[...], sh_ref[0], sc_ref[0]).astype(BF16)
    step = hidden // n_split
    y = None
    for j in range(n_split):
        gt = jnp.dot(h, wi_ref[:, j * step:(j + 1) * step], preferred_element_type=F32)
        up = jnp.dot(h, wi_ref[:, hidden + j * step:hidden + (j + 1) * step], preferred_element_type=F32)
        act = (gt * jax.nn.sigmoid(gt) * up).astype(BF16)
        t = jnp.dot(act, wo_ref[j * step:(j + 1) * step, :], preferred_element_type=F32)
        y = t if y is None else y + t
    out = x + gate_ref[0] * y
    if final_norm:
        ms = jnp.mean(out * out, axis=-1, keepdims=True)
        out = out * lax.rsqrt(ms + EPS) * fg_ref[...]
    o_ref[...] = out


def _ffn(x2, mod3, norm_g, w_in, w_out, final_g, *, n_batch, seq, tm, final_norm):
    nt, d = x2.shape
    tps = seq // tm
    tok = lambda i: (i, 0)
    mrow = lambda j: (lambda i: ((i // tps) * 6 + j, 0, 0))
    return pl.pallas_call(
        functools.partial(_ffn_kernel, n_split=1, final_norm=final_norm),
        grid=(nt // tm,),
        in_specs=[
            pl.BlockSpec((tm, d), tok),
            pl.BlockSpec((1, 1, d), mrow(3)),
            pl.BlockSpec((1, 1, d), mrow(4)),
            pl.BlockSpec((1, 1, d), mrow(5)),
            _const_spec((1, d)),
            _const_spec(w_in.shape),
            _const_spec(w_out.shape),
            _const_spec((1, d)),
        ],
        out_specs=pl.BlockSpec((tm, d), tok),
        out_shape=jax.ShapeDtypeStruct((nt, d), F32),
        compiler_params=_params(1),
        name="ffn_final" if final_norm else "ffn",
    )(x2, mod3, mod3, mod3, norm_g, w_in, w_out, final_g)


def _rope_tables(seq):
    t = jnp.arange(seq)
    pos_row = (t // GRID_W).astype(F32)
    pos_col = (t % GRID_W).astype(F32)
    half = HEAD_DIM // 2
    freqs = jnp.power(ROPE_THETA, -jnp.arange(0, half, 2, dtype=F32) / half)

    def part(pos):
        ang = pos[:, None] * freqs
        c, s = jnp.cos(ang), jnp.sin(ang)
        return jnp.concatenate([c, c], axis=-1), jnp.concatenate([-s, s], axis=-1)

    cr, sr = part(pos_row)
    cc, sc = part(pos_col)
    cos = jnp.concatenate([cr, cc], axis=-1)
    sin = jnp.concatenate([sr, sc], axis=-1)
    return jnp.tile(cos, (1, LANES // HEAD_DIM)), jnp.tile(sin, (1, LANES // HEAD_DIM))


def kernel(x, c, ctx, c_ctx, w_mod, b_mod, norm1_g, norm2_g, w_in, conv_w, conv_b, conv_ln_g, conv_ln_b,
           w_conv_out, na_rpb, w_na_out, q_norm_g, k_norm_g, w_gqa_out, w_out, w_ffn_in, w_ffn_out, final_g):
    n_batch, seq, d = x.shape
    ctx_len = ctx.shape[1]
    depth = w_mod.shape[0]
    rows = seq // GRID_W
    assert seq % (NA_QROWS * GRID_W) == 0 and rows >= NA_WROWS and ctx_len == KEY_CHUNK
    tm = 512
    tmc = ctx_len

    n_rows = -(-(n_batch + 1) // 8) * 8
    crows = jnp.zeros((n_rows, d), F32).at[:n_batch].set(c).at[n_batch].set(c_ctx)
    mod = _modulation(crows, w_mod, b_mod)

    cos_t, sin_t = _rope_tables(seq)
    bd = jnp.asarray(np.kron(np.eye(LANES // HEAD_DIM), np.full((HEAD_DIM, HEAD_DIM), 1.0 / HEAD_DIM)), BF16)
    tile2 = lambda g: jnp.tile(g, LANES // HEAD_DIM).reshape(1, LANES)
    row = lambda g: g.reshape(1, -1)

    xl = x.reshape(n_batch * seq, d)
    xc = ctx.reshape(n_batch * ctx_len, d)
    for l in range(depth):
        last = l == depth - 1
        wl = w_in[l]
        w_proj = jnp.concatenate([wl[:, _IN_GLU:_IN_NAV], wl[:, _IN_GQ:_IN_GV]], axis=1).astype(BF16)
        w_vt = jnp.concatenate([wl[:, _IN_NAV:_IN_GQ], wl[:, _IN_GV:_IN_GATES]], axis=1).T.astype(BF16)
        w_gates = wl[:, _IN_GATES:].astype(BF16)
        mod_l = mod[l, :n_batch].reshape(n_batch * 6, 1, d)
        mod_c = jnp.broadcast_to(mod[l, n_batch], (n_batch, 6 * d)).reshape(n_batch * 6, 1, d)
        qg2, kg2 = tile2(q_norm_g[l]), tile2(k_norm_g[l])
        g1, g2 = row(norm1_g[l]), row(norm2_g[l])
        wa, wb, wc = w_conv_out[l].astype(BF16), w_na_out[l].astype(BF16), w_gqa_out[l].astype(BF16)
        wo = w_out[l].astype(BF16)
        wfi, wfo = w_ffn_in[l].astype(BF16), w_ffn_out[l].astype(BF16)
        conv_args = (conv_w[l], row(conv_b[l]), row(conv_ln_g[l]), row(conv_ln_b[l]))

        u, naq, nak, gq, gk, navt, gvt = _inproj(xl, mod_l, g1, w_proj, w_vt, qg2, kg2, bd, cos_t, sin_t,
                                                 n_batch=n_batch, seq=seq, tm=tm, rope=True)
        uc, naqc, nakc, gqc, gkc, navct, gvct = _inproj(xc, mod_c, g1, w_proj, w_vt, qg2, kg2, bd,
                                                        cos_t[:ctx_len], sin_t[:ctx_len],
                                                        n_batch=n_batch, seq=ctx_len, tm=tmc, rope=False)

        bias = _na_bias_tables(na_rpb[l], rows)
        br_b = _na_attention(naq, nak, navt, nakc, navct, bias, n_batch=n_batch, seq=seq, ctx_len=ctx_len)
        br_c = _gqa_attention(gq, gk, gkc, gvt, gvct, n_batch=n_batch, sq=seq, ctx_len=ctx_len, tq=256)
        xl = _merge(xl, mod_l, g1, w_gates, u, *conv_args, br_b, br_c, wa, wb, wc, wo,
                    n_batch=n_batch, seq=seq, tm=tm)
        xl = _ffn(xl, mod_l, g2, wfi, wfo, row(final_g), n_batch=n_batch, seq=seq, tm=tm, final_norm=last)

        if not last:
            cbr_b = _ctx_attention(naqc, nakc, navct, n_batch=n_batch, ctx_len=ctx_len, n_kv=NA_HEADS, group=1)
            cbr_c = _ctx_attention(gqc, gkc, gvct, n_batch=n_batch, ctx_len=ctx_len, n_kv=GQA_KV_HEADS,
                                   group=GQA_GROUP)
            xc = _merge(xc, mod_c, g1, w_gates, uc, *conv_args, cbr_b, cbr_c, wa, wb, wc, wo,
                        n_batch=n_batch, seq=ctx_len, tm=tmc)
            xc = _ffn(xc, mod_c, g2, wfi, wfo, row(final_g), n_batch=n_batch, seq=ctx_len, tm=tmc,
                      final_norm=False)
    return xl.reshape(n_batch, seq, d)
```

```python
import functools

import numpy as np
import jax
import jax.numpy as jnp
from jax import lax
from jax.experimental import pallas as pl
from jax.experimental.pallas import tpu as pltpu

F32 = jnp.float32
BF16 = jnp.bfloat16

GRID_W = 64
HEAD_DIM = 64
CONV_W = 512
CONV_K = 31
CONV_HALO = 16
CONV_ROWS = 64
NA_HEADS = 8
NA_W = NA_HEADS * HEAD_DIM
NA_KH = 8
NA_KW = 16
NA_QROWS = 4
NA_WROWS = 12
NA_AHEAD = 8
GQA_Q_HEADS = 8
GQA_KV_HEADS = 2
GQA_GROUP = GQA_Q_HEADS // GQA_KV_HEADS
GQA_Q_W = GQA_Q_HEADS * HEAD_DIM
GQA_KV_W = GQA_KV_HEADS * HEAD_DIM
GQA_AHEAD = 2
GQA_UNROLL = 6
ROPE_THETA = 10000.0
EPS = 1e-6
NEG = -1e30
SCALE = HEAD_DIM ** -0.5
LOG2E = 1.4426950408889634
LANES = 128
SUBLANES = 8
KEY_CHUNK = 256
ONES_ROWS = 16
VMEM_LIMIT = 56 * 1024 * 1024

_IN_GLU = 0
_IN_NAQ = _IN_GLU + 2 * CONV_W
_IN_NAK = _IN_NAQ + NA_W
_IN_NAV = _IN_NAK + NA_W
_IN_GQ = _IN_NAV + NA_W
_IN_GK = _IN_GQ + GQA_Q_W
_IN_GV = _IN_GK + GQA_KV_W
_IN_GATES = _IN_GV + GQA_KV_W
_P_GLU = 0
_P_NAQ = _P_GLU + 2 * CONV_W
_P_NAK = _P_NAQ + NA_W
_P_GQ = _P_NAK + NA_W
_P_GK = _P_GQ + GQA_Q_W
_P_END = _P_GK + GQA_KV_W


def _params(n_axes):
    return pltpu.CompilerParams(dimension_semantics=("arbitrary",) * n_axes, vmem_limit_bytes=VMEM_LIMIT)


def _const_spec(shape):
    zeros = (0,) * len(shape)
    return pl.BlockSpec(shape, lambda *_: zeros, pipeline_mode=pl.Buffered(1))


def _norm_modulate(x, g, shift, scale):
    ms = jnp.mean(x * x, axis=-1, keepdims=True)
    h = x * lax.rsqrt(ms + EPS) * g
    return h * (1.0 + scale) + shift


def _with_ones(v_t):
    return jnp.concatenate([v_t, jnp.ones((ONES_ROWS, v_t.shape[1]), v_t.dtype)], axis=0)


def _mod_kernel(c_ref, w_ref, b_ref, o_ref):
    c = c_ref[...]
    s = c * jax.nn.sigmoid(c)
    o_ref[0] = jnp.dot(s, w_ref[0], preferred_element_type=F32, precision=lax.Precision.HIGHEST) + b_ref[0]


def _modulation(crows, w_mod, b_mod):
    n_layers, d, d6 = w_mod.shape
    r = crows.shape[0]
    nb = d6 // d
    return pl.pallas_call(
        _mod_kernel,
        grid=(n_layers, nb),
        in_specs=[
            pl.BlockSpec((r, d), lambda l, j: (0, 0)),
            pl.BlockSpec((1, d, d), lambda l, j: (l, 0, j)),
            pl.BlockSpec((1, 1, d), lambda l, j: (l, 0, j)),
        ],
        out_specs=pl.BlockSpec((1, r, d), lambda l, j: (l, 0, j)),
        out_shape=jax.ShapeDtypeStruct((n_layers, r, d6), F32),
        compiler_params=_params(2),
        name="modulation",
    )(crows, w_mod, b_mod.reshape(n_layers, 1, d6))


def _head_rms(t, bd):
    ms = jnp.dot((t * t).astype(BF16), bd, preferred_element_type=F32)
    return t * lax.rsqrt(ms + EPS)


def _rope(t, cos, sin_signed, first_half):
    fwd = pltpu.roll(t, LANES - 16, axis=1)
    bwd = pltpu.roll(t, 16, axis=1)
    return t * cos + jnp.where(first_half, fwd, bwd) * sin_signed


def _inproj_kernel(x_ref, sh_ref, sc_ref, g_ref, w_ref, wvt_ref, qg_ref, kg_ref, bd_ref, cos_ref, sin_ref,
                   u_ref, naq_ref, nak_ref, gq_ref, gk_ref, navt_ref, gvt_ref, *, rope):
    h = _norm_modulate(x_ref[...], g_ref[...], sh_ref[0], sc_ref[0]).astype(BF16)

    def proj(lo, hi):
        return jnp.dot(h, w_ref[:, lo:hi], preferred_element_type=F32)

    q_scale = SCALE * LOG2E if rope else SCALE
    bd = bd_ref[...]
    if rope:
        cos = cos_ref[...]
        sin = sin_ref[...]
        lane = lax.broadcasted_iota(jnp.int32, cos.shape, 1)
        first_half = (lane % 32) < 16

    def finish(t, gain):
        t = _head_rms(t, bd) * gain
        if rope:
            t = _rope(t, cos, sin, first_half)
        return t

    q = proj(_P_GQ, _P_GK)
    for j in range(GQA_Q_W // LANES):
        qj = finish(q[:, j * LANES:(j + 1) * LANES], qg_ref[...])
        gq_ref[:, j * LANES:(j + 1) * LANES] = (qj * q_scale).astype(BF16)
    gk_ref[...] = finish(proj(_P_GK, _P_END), kg_ref[...]).astype(BF16)

    a = proj(_P_GLU, _P_NAQ)
    u_ref[...] = (a[:, :CONV_W] * jax.nn.sigmoid(a[:, CONV_W:])).astype(BF16)
    naq_ref[...] = (proj(_P_NAQ, _P_NAK) * q_scale).astype(BF16)
    nak_ref[...] = proj(_P_NAK, _P_GQ).astype(BF16)

    v_t = lax.dot_general(wvt_ref[...], h, (((1,), (1,)), ((), ())), preferred_element_type=F32).astype(BF16)
    for c in range(navt_ref.shape[1]):
        navt_ref[0, c] = v_t[:NA_W, c * KEY_CHUNK:(c + 1) * KEY_CHUNK]
        gvt_ref[0, c] = v_t[NA_W:, c * KEY_CHUNK:(c + 1) * KEY_CHUNK]


def _inproj(x2, mod3, norm_g, w, w_vt, qg2, kg2, bd, cos_t, sin_t, *, n_batch, seq, tm, rope):
    nt, d = x2.shape
    tps = seq // tm
    cpt = tm // KEY_CHUNK
    tok = lambda s, b: (b * tps + s, 0)
    outs = [CONV_W, NA_W, NA_W, GQA_Q_W, GQA_KV_W]
    vt_shape = lambda w_: jax.ShapeDtypeStruct((n_batch, seq // KEY_CHUNK, w_, KEY_CHUNK), BF16)
    vt_spec = lambda w_: pl.BlockSpec((1, cpt, w_, KEY_CHUNK), lambda s, b: (b, s, 0, 0))
    return pl.pallas_call(
        functools.partial(_inproj_kernel, rope=rope),
        grid=(tps, n_batch),
        in_specs=[
            pl.BlockSpec((tm, d), tok),
            pl.BlockSpec((1, 1, d), lambda s, b: (b * 6 + 0, 0, 0)),
            pl.BlockSpec((1, 1, d), lambda s, b: (b * 6 + 1, 0, 0)),
            _const_spec((1, d)),
            _const_spec(w.shape),
            _const_spec(w_vt.shape),
            _const_spec((1, LANES)),
            _const_spec((1, LANES)),
            _const_spec((LANES, LANES)),
            pl.BlockSpec((tm, LANES), lambda s, b: (s, 0)),
            pl.BlockSpec((tm, LANES), lambda s, b: (s, 0)),
        ],
        out_specs=[pl.BlockSpec((tm, n), tok) for n in outs] + [vt_spec(NA_W), vt_spec(GQA_KV_W)],
        out_shape=[jax.ShapeDtypeStruct((nt, n), BF16) for n in outs] + [vt_shape(NA_W), vt_shape(GQA_KV_W)],
        compiler_params=_params(2),
        name="inproj_rope" if rope else "inproj_ctx",
    )(x2, mod3, mod3, norm_g, w, w_vt, qg2, kg2, bd, cos_t, sin_t)


def _conv_ln_swish(prev_ref, cur_ref, next_ref, w_ref, b_ref, lg_ref, lb_ref, win_ref, shift_ref, out_ref, first,
                   last):
    tl = cur_ref.shape[0]
    win_ref[0:CONV_HALO, :] = jnp.where(first, 0.0, prev_ref[...].astype(F32))
    win_ref[CONV_HALO:CONV_HALO + tl, :] = cur_ref[...].astype(F32)
    win_ref[CONV_HALO + tl:, :] = jnp.where(last, 0.0, next_ref[...].astype(F32))
    base = CONV_HALO - CONV_K // 2
    for r in range(SUBLANES):
        n_shift = tl + SUBLANES * (len(range(r, CONV_K, SUBLANES)) - 1)
        shift_ref[r, 0:n_shift, :] = win_ref[base + r:base + r + n_shift, :]
    for rb in range(0, tl, CONV_ROWS):
        acc = jnp.zeros((CONV_ROWS, CONV_W), F32)
        for k in range(CONV_K):
            off = rb + SUBLANES * (k // SUBLANES)
            acc = acc + shift_ref[k % SUBLANES, off:off + CONV_ROWS, :] * w_ref[k:k + 1, :]
        y = acc + b_ref[...]
        mu = jnp.mean(y, axis=-1, keepdims=True)
        yc = y - mu
        var = jnp.mean(yc * yc, axis=-1, keepdims=True)
        z = yc * lax.rsqrt(var + EPS) * lg_ref[...] + lb_ref[...]
        out_ref[rb:rb + CONV_ROWS, :] = (z * jax.nn.sigmoid(z)).astype(out_ref.dtype)


def _conv_halo_specs(n_tokens, tl):
    hb = tl // CONV_HALO
    last_hb = n_tokens // CONV_HALO - 1
    return [
        pl.BlockSpec((CONV_HALO, CONV_W), lambda i: (jnp.maximum(i * hb - 1, 0), 0)),
        pl.BlockSpec((tl, CONV_W), lambda i: (i, 0)),
        pl.BlockSpec((CONV_HALO, CONV_W), lambda i: (jnp.minimum((i + 1) * hb, last_hb), 0)),
    ]


def _na_bias_tables(rpb, rows):
    cols = np.arange(GRID_W)
    win0 = np.clip(cols - NA_KW // 2, 0, GRID_W - NA_KW)
    col_ok = (cols[:, None] >= win0[None, :]) & (cols[:, None] < win0[None, :] + NA_KW)
    pad = GRID_W - NA_KW
    rp = jnp.pad(rpb * LOG2E, ((0, 0), (0, 0), (pad, pad)))
    tiles = jnp.stack([rp[:, :, GRID_W - 1 - qc:2 * GRID_W - 1 - qc] for qc in range(GRID_W)], axis=-1)
    tiles = jnp.where(jnp.asarray(col_ok)[None, None], tiles, NEG * LOG2E)
    masked = jnp.full(tiles.shape[:1] + tiles.shape[2:], NEG * LOG2E, F32)
    n_groups = rows // NA_QROWS
    variants = []
    for g in (0, min(1, n_groups - 1), n_groups - 1):
        w = int(np.clip(NA_QROWS * g - NA_KH // 2, 0, rows - NA_WROWS))
        blocks = []
        for kl in range(NA_WROWS):
            kr = w + kl
            row = []
            for ql in range(NA_QROWS):
                r = NA_QROWS * g + ql
                r0 = int(np.clip(r - NA_KH // 2, 0, rows - NA_KH))
                row.append(tiles[:, kr - r + NA_KH - 1] if r0 <= kr < r0 + NA_KH else masked)
            blocks.append(jnp.concatenate(row, axis=-1))
        variants.append(jnp.concatenate(blocks, axis=1))
    return jnp.stack(variants)


def _na_kernel(q_ref, k_ref, vt_ref, kc_ref, vct_ref, bias_ref, o_ref, s_ref, ot_ref, *, n_chunks):
    g = pl.program_id(1)
    nq = NA_QROWS * GRID_W
    win_chunks = NA_WROWS * GRID_W // KEY_CHUNK
    c0 = jnp.clip(g - 1, 0, n_chunks - win_chunks)
    start = pl.multiple_of(c0 * KEY_CHUNK, KEY_CHUNK)
    q_t = q_ref[...].astype(F32).T.astype(BF16)
    zeros = jnp.zeros((HEAD_DIM, nq), BF16)
    units = [(h, t) for t in range(win_chunks + 1) for h in range(NA_HEADS)]

    def scores(h, t, s_ref):
        hp, j = divmod(h, LANES // HEAD_DIM)
        qh = q_t[h * HEAD_DIM:(h + 1) * HEAD_DIM]
        qz = jnp.concatenate([qh, zeros] if j == 0 else [zeros, qh], axis=0)
        pair = slice(hp * LANES, (hp + 1) * LANES)
        if t < win_chunks:
            rows_k = pl.ds(pl.multiple_of(start + t * KEY_CHUNK, KEY_CHUNK), KEY_CHUNK)
            s_ref[...] = (jnp.dot(k_ref[rows_k, pair], qz, preferred_element_type=F32)
                          + bias_ref[0, h, t * KEY_CHUNK:(t + 1) * KEY_CHUNK, :])
        else:
            s_ref[...] = jnp.dot(kc_ref[:, pair], qz, preferred_element_type=F32)

    def softmax_pv(h, t, s_ref, state):
        rows_h = slice(h * HEAD_DIM, (h + 1) * HEAD_DIM)
        v_t = vt_ref[0, c0 + t, rows_h, :] if t < win_chunks else vct_ref[0, 0, rows_h, :]
        s = s_ref[...]
        m_new = jnp.max(s, axis=0, keepdims=True)
        if state is not None:
            m_old, acc_old = state
            m_new = jnp.maximum(m_old, m_new)
        p = jnp.exp2((s - m_new).astype(BF16))
        acc = jnp.dot(_with_ones(v_t), p, preferred_element_type=F32)
        if state is not None:
            acc = acc + acc_old * jnp.exp2(m_old - m_new)
        return m_new, acc

    n_buf = NA_AHEAD + 1
    for u in range(NA_AHEAD):
        scores(*units[u], s_ref.at[u % n_buf])
    state = [None] * NA_HEADS
    for u, (h, t) in enumerate(units):
        if u + NA_AHEAD < len(units):
            scores(*units[u + NA_AHEAD], s_ref.at[(u + NA_AHEAD) % n_buf])
        state[h] = softmax_pv(h, t, s_ref.at[u % n_buf], state[h])
        if t == win_chunks:
            acc = state[h][1]
            ot_ref[h * HEAD_DIM:(h + 1) * HEAD_DIM, :] = acc[:HEAD_DIM] * (1.0 / acc[HEAD_DIM:HEAD_DIM + 1])
    o_ref[...] = ot_ref[...].T.astype(BF16)


def _na_attention(q, k, vt, kc, vct, bias, *, n_batch, seq, ctx_len):
    rows = seq // GRID_W
    n_groups = rows // NA_QROWS
    nq = NA_QROWS * GRID_W
    variant = lambda g: jnp.where(g == 0, 0, jnp.where(g == n_groups - 1, 2, 1))
    return pl.pallas_call(
        functools.partial(_na_kernel, n_chunks=seq // KEY_CHUNK),
        grid=(n_batch, n_groups),
        in_specs=[
            pl.BlockSpec((nq, NA_W), lambda b, g: (b * n_groups + g, 0)),
            pl.BlockSpec((seq, NA_W), lambda b, g: (b, 0)),
            pl.BlockSpec((1,) + vt.shape[1:], lambda b, g: (b, 0, 0, 0)),
            pl.BlockSpec((ctx_len, NA_W), lambda b, g: (b, 0)),
            pl.BlockSpec((1,) + vct.shape[1:], lambda b, g: (b, 0, 0, 0)),
            pl.BlockSpec((1,) + bias.shape[1:], lambda b, g: (variant(g), 0, 0, 0)),
        ],
        out_specs=pl.BlockSpec((nq, NA_W), lambda b, g: (b * n_groups + g, 0)),
        out_shape=jax.ShapeDtypeStruct(q.shape, BF16),
        scratch_shapes=[
            pltpu.VMEM((NA_AHEAD + 1, KEY_CHUNK, nq), F32),
            pltpu.VMEM((NA_W, nq), F32),
        ],
        compiler_params=_params(2),
        name="na_attention",
    )(q, k, vt, kc, vct, bias)


def _gqa_kernel(q_ref, k_ref, kc_ref, vt_ref, vct_ref, o_ref, qt_ref, s_ref, m_ref, acc_ref, *, n_chunks):
    tq = q_ref.shape[0]
    q_t = q_ref[...].astype(F32).T
    for hk in range(GQA_KV_HEADS):
        for gi in range(GQA_GROUP):
            hq = hk * GQA_GROUP + gi
            qt_ref[hk, :, gi * tq:(gi + 1) * tq] = q_t[hq * HEAD_DIM:(hq + 1) * HEAD_DIM, :].astype(BF16)
    m_ref[...] = jnp.full(m_ref.shape, NEG, F32)
    acc_ref[...] = jnp.zeros(acc_ref.shape, F32)

    def scores(k_blk, s_buf):
        for hk in range(GQA_KV_HEADS):
            kch = k_blk[:, hk * HEAD_DIM:(hk + 1) * HEAD_DIM]
            s_buf[hk] = jnp.dot(kch, qt_ref[hk], preferred_element_type=F32)

    def softmax_pv(v_t, s_ref):
        for hk in range(GQA_KV_HEADS):
            s = s_ref[hk]
            m_old = m_ref[hk]
            m_new = jnp.maximum(m_old, jnp.max(s, axis=0, keepdims=True))
            p = jnp.exp2((s - m_new).astype(BF16))
            pv = jnp.dot(_with_ones(v_t[hk * HEAD_DIM:(hk + 1) * HEAD_DIM]), p, preferred_element_type=F32)
            acc_ref[hk] = acc_ref[hk] * jnp.exp2(m_old - m_new) + pv
            m_ref[hk] = m_new

    def k_rows(c):
        if isinstance(c, int) and c == n_chunks:
            return kc_ref[...]
        return k_ref[pl.ds(pl.multiple_of(c * KEY_CHUNK, KEY_CHUNK), KEY_CHUNK), :]

    def v_cols(c):
        return vct_ref[0, 0] if isinstance(c, int) and c == n_chunks else vt_ref[0, c]

    n_buf = GQA_AHEAD + 1
    for c in range(GQA_AHEAD):
        scores(k_rows(c), s_ref.at[c % n_buf])

    def steps(c0, n):
        for j in range(n):
            c = c0 + j
            if not (isinstance(c, int) and c + GQA_AHEAD > n_chunks):
                scores(k_rows(c + GQA_AHEAD), s_ref.at[(j + GQA_AHEAD) % n_buf])
            softmax_pv(v_cols(c), s_ref.at[j % n_buf])

    trips = (n_chunks - GQA_AHEAD) // GQA_UNROLL

    def body(i, carry):
        steps(i * GQA_UNROLL, GQA_UNROLL)
        return carry

    lax.fori_loop(0, trips, body, 0)
    steps(trips * GQA_UNROLL, n_chunks + 1 - trips * GQA_UNROLL)

    heads_t = []
    for hk in range(GQA_KV_HEADS):
        acc = acc_ref[hk]
        out_t = acc[:HEAD_DIM] * (1.0 / acc[HEAD_DIM:HEAD_DIM + 1])
        heads_t += [out_t[:, gi * tq:(gi + 1) * tq] for gi in range(GQA_GROUP)]
    o_ref[...] = jnp.concatenate(heads_t, axis=0).T.astype(BF16)


def _gqa_attention(q, k, kc, vt, vct, *, n_batch, sq, ctx_len, tq):
    tpq = sq // tq
    m = GQA_GROUP * tq
    n_chunks = sq // KEY_CHUNK
    assert n_chunks >= GQA_AHEAD and GQA_UNROLL % (GQA_AHEAD + 1) == 0 and ctx_len == KEY_CHUNK
    return pl.pallas_call(
        functools.partial(_gqa_kernel, n_chunks=n_chunks),
        grid=(n_batch, tpq),
        in_specs=[
            pl.BlockSpec((tq, GQA_Q_W), lambda b, i: (b * tpq + i, 0)),
            pl.BlockSpec((sq, GQA_KV_W), lambda b, i: (b, 0)),
            pl.BlockSpec((ctx_len, GQA_KV_W), lambda b, i: (b, 0)),
            pl.BlockSpec((1,) + vt.shape[1:], lambda b, i: (b, 0, 0, 0)),
            pl.BlockSpec((1,) + vct.shape[1:], lambda b, i: (b, 0, 0, 0)),
        ],
        out_specs=pl.BlockSpec((tq, GQA_Q_W), lambda b, i: (b * tpq + i, 0)),
        out_shape=jax.ShapeDtypeStruct(q.shape, BF16),
        scratch_shapes=[
            pltpu.VMEM((GQA_KV_HEADS, HEAD_DIM, m), BF16),
            pltpu.VMEM((GQA_AHEAD + 1, GQA_KV_HEADS, KEY_CHUNK, m), F32),
            pltpu.VMEM((GQA_KV_HEADS, 1, m), F32),
            pltpu.VMEM((GQA_KV_HEADS, HEAD_DIM + ONES_ROWS, m), F32),
        ],
        compiler_params=_params(2),
        name="gqa_attention",
    )(q, k, kc, vt, vct)


def _ctx_attn_kernel(q_ref, k_ref, vt_ref, o_ref, ot_ref, *, n_kv, group):
    tq = q_ref.shape[0]
    q_t = q_ref[...].astype(F32).T.astype(BF16)
    for hk in range(n_kv):
        rows_k = slice(hk * HEAD_DIM, (hk + 1) * HEAD_DIM)
        qs = jnp.concatenate([q_t[(hk * group + gi) * HEAD_DIM:(hk * group + gi + 1) * HEAD_DIM]
                              for gi in range(group)], axis=1)
        s = jnp.dot(k_ref[:, rows_k], qs, preferred_element_type=F32)
        p = jnp.exp(s - jnp.max(s, axis=0, keepdims=True)).astype(BF16)
        acc = jnp.dot(_with_ones(vt_ref[0, 0, rows_k, :]), p, preferred_element_type=F32)
        out_t = acc[:HEAD_DIM] * (1.0 / acc[HEAD_DIM:HEAD_DIM + 1])
        for gi in range(group):
            hq = hk * group + gi
            ot_ref[hq * HEAD_DIM:(hq + 1) * HEAD_DIM, :] = out_t[:, gi * tq:(gi + 1) * tq]
    o_ref[...] = ot_ref[...].T.astype(BF16)


def _ctx_attention(q, k, vt, *, n_batch, ctx_len, n_kv, group):
    return pl.pallas_call(
        functools.partial(_ctx_attn_kernel, n_kv=n_kv, group=group),
        grid=(n_batch,),
        in_specs=[
            pl.BlockSpec((ctx_len, q.shape[1]), lambda b: (b, 0)),
            pl.BlockSpec((ctx_len, k.shape[1]), lambda b: (b, 0)),
            pl.BlockSpec((1,) + vt.shape[1:], lambda b: (b, 0, 0, 0)),
        ],
        out_specs=pl.BlockSpec((ctx_len, q.shape[1]), lambda b: (b, 0)),
        out_shape=jax.ShapeDtypeStruct(q.shape, BF16),
        scratch_shapes=[pltpu.VMEM((q.shape[1], ctx_len), F32)],
        compiler_params=_params(1),
        name=f"ctx_attention_g{group}",
    )(q, k, vt)


def _merge_kernel(x_ref, sh_ref, sc_ref, gate_ref, g_ref, wg_ref, uprev_ref, u_ref, unext_ref, cw_ref, cb_ref,
                  lg_ref, lb_ref, b_ref, c_ref, wa_ref, wb_ref, wc_ref, wo_ref, o_ref, win_ref, shift_ref, bra_ref,
                  *, tiles_per_seq):
    pos = pl.program_id(0) % tiles_per_seq
    x = x_ref[...]
    d = x.shape[1]
    h = _norm_modulate(x, g_ref[...], sh_ref[0], sc_ref[0]).astype(BF16)
    def gated(j, br, w_ref):
        gate = jax.nn.sigmoid(jnp.dot(h, wg_ref[:, j * d:(j + 1) * d], preferred_element_type=F32))
        return gate * jnp.dot(br, w_ref[...], preferred_element_type=F32)

    y = gated(1, b_ref[...], wb_ref) + gated(2, c_ref[...], wc_ref)
    gate_a = jax.nn.sigmoid(jnp.dot(h, wg_ref[:, 0:d], preferred_element_type=F32))
    _conv_ln_swish(uprev_ref, u_ref, unext_ref, cw_ref, cb_ref, lg_ref, lb_ref, win_ref, shift_ref, bra_ref,
                   pos == 0, pos == tiles_per_seq - 1)
    y = y + gate_a * jnp.dot(bra_ref[...], wa_ref[...], preferred_element_type=F32)
    o_ref[...] = x + gate_ref[0] * jnp.dot(y.astype(BF16), wo_ref[...], preferred_element_type=F32)


def _merge(x2, mod3, norm_g, w_gates, u, conv_w, conv_b, ln_g, ln_b, br_b, br_c, wa, wb, wc, wo, *, n_batch, seq,
           tm):
    nt, d = x2.shape
    tps = seq // tm
    tok = lambda i: (i, 0)
    mrow = lambda j: (lambda i: ((i // tps) * 6 + j, 0, 0))
    return pl.pallas_call(
        functools.partial(_merge_kernel, tiles_per_seq=tps),
        grid=(nt // tm,),
        in_specs=[
            pl.BlockSpec((tm, d), tok),
            pl.BlockSpec((1, 1, d), mrow(0)),
            pl.BlockSpec((1, 1, d), mrow(1)),
            pl.BlockSpec((1, 1, d), mrow(2)),
            _const_spec((1, d)),
            _const_spec(w_gates.shape),
            *_conv_halo_specs(nt, tm),
            _const_spec((CONV_K, CONV_W)),
            _const_spec((1, CONV_W)),
            _const_spec((1, CONV_W)),
            _const_spec((1, CONV_W)),
            pl.BlockSpec((tm, NA_W), tok),
            pl.BlockSpec((tm, GQA_Q_W), tok),
            _const_spec(wa.shape),
            _const_spec(wb.shape),
            _const_spec(wc.shape),
            _const_spec(wo.shape),
        ],
        out_specs=pl.BlockSpec((tm, d), tok),
        out_shape=jax.ShapeDtypeStruct((nt, d), F32),
        scratch_shapes=[pltpu.VMEM((tm + 2 * CONV_HALO, CONV_W), F32),
                        pltpu.VMEM((SUBLANES, tm + 2 * CONV_HALO, CONV_W), F32),
                        pltpu.VMEM((tm, CONV_W), BF16)],
        compiler_params=_params(1),
        name="merge",
    )(x2, mod3, mod3, mod3, norm_g, w_gates, u, u, u, conv_w, conv_b, ln_g, ln_b, br_b, br_c, wa, wb, wc, wo)


def _ffn_kernel(x_ref, sh_ref, sc_ref, gate_ref, g_ref, wi_ref, wo_ref, fg_ref, o_ref, *, n_split, final_norm):
    x = x_ref[...]
    hidden = wo_ref.shape[0]
    h = _norm_modulate(x, g_ref[...], sh_ref[0], sc_ref[0]).astype(BF16)
    step = hidden // n_split
    y = None
    for j in range(n_split):
        gt = jnp.dot(h, wi_ref[:, j * step:(j + 1) * step], preferred_element_type=F32)
        up = jnp.dot(h, wi_ref[:, hidden + j * step:hidden + (j + 1) * step], preferred_element_type=F32)
        act = (gt * jax.nn.sigmoid(gt) * up).astype(BF16)
        t = jnp.dot(act, wo_ref[j * step:(j + 1) * step, :], preferred_element_type=F32)
        y = t if y is None else y + t
    out = x + gate_ref[0] * y
    if final_norm:
        ms = jnp.mean(out * out, axis=-1, keepdims=True)
        out = out * lax.rsqrt(ms + EPS) * fg_ref[...]
    o_ref[...] = out


def _ffn(x2, mod3, norm_g, w_in, w_out, final_g, *, n_batch, seq, tm, final_norm):
    nt, d = x2.shape
    tps = seq // tm
    tok = lambda i: (i, 0)
    mrow = lambda j: (lambda i: ((i // tps) * 6 + j, 0, 0))
    return pl.pallas_call(
        functools.partial(_ffn_kernel, n_split=1, final_norm=final_norm),
        grid=(nt // tm,),
        in_specs=[
            pl.BlockSpec((tm, d), tok),
            pl.BlockSpec((1, 1, d), mrow(3)),
            pl.BlockSpec((1, 1, d), mrow(4)),
            pl.BlockSpec((1, 1, d), mrow(5)),
            _const_spec((1, d)),
            _const_spec(w_in.shape),
            _const_spec(w_out.shape),
            _const_spec((1, d)),
        ],
        out_specs=pl.BlockSpec((tm, d), tok),
        out_shape=jax.ShapeDtypeStruct((nt, d), F32),
        compiler_params=_params(1),
        name="ffn_final" if final_norm else "ffn",
    )(x2, mod3, mod3, mod3, norm_g, w_in, w_out, final_g)


def _rope_tables(seq):
    t = jnp.arange(seq)
    pos_row = (t // GRID_W).astype(F32)
    pos_col = (t % GRID_W).astype(F32)
    half = HEAD_DIM // 2
    freqs = jnp.power(ROPE_THETA, -jnp.arange(0, half, 2, dtype=F32) / half)

    def part(pos):
        ang = pos[:, None] * freqs
        c, s = jnp.cos(ang), jnp.sin(ang)
        return jnp.concatenate([c, c], axis=-1), jnp.concatenate([-s, s], axis=-1)

    cr, sr = part(pos_row)
    cc, sc = part(pos_col)
    cos = jnp.concatenate([cr, cc], axis=-1)
    sin = jnp.concatenate([sr, sc], axis=-1)
    return jnp.tile(cos, (1, LANES // HEAD_DIM)), jnp.tile(sin, (1, LANES // HEAD_DIM))


def kernel(x, c, ctx, c_ctx, w_mod, b_mod, norm1_g, norm2_g, w_in, conv_w, conv_b, conv_ln_g, conv_ln_b,
           w_conv_out, na_rpb, w_na_out, q_norm_g, k_norm_g, w_gqa_out, w_out, w_ffn_in, w_ffn_out, final_g):
    n_batch, seq, d = x.shape
    ctx_len = ctx.shape[1]
    depth = w_mod.shape[0]
    rows = seq // GRID_W
    assert seq % (NA_QROWS * GRID_W) == 0 and rows >= NA_WROWS and ctx_len == KEY_CHUNK
    tm = 512
    tmc = ctx_len

    n_rows = -(-(n_batch + 1) // 8) * 8
    crows = jnp.zeros((n_rows, d), F32).at[:n_batch].set(c).at[n_batch].set(c_ctx)
    mod = _modulation(crows, w_mod, b_mod)

    cos_t, sin_t = _rope_tables(seq)
    bd = jnp.asarray(np.kron(np.eye(LANES // HEAD_DIM), np.full((HEAD_DIM, HEAD_DIM), 1.0 / HEAD_DIM)), BF16)
    tile2 = lambda g: jnp.tile(g, LANES // HEAD_DIM).reshape(1, LANES)
    row = lambda g: g.reshape(1, -1)

    xl = x.reshape(n_batch * seq, d)
    xc = ctx.reshape(n_batch * ctx_len, d)
    for l in range(depth):
        last = l == depth - 1
        wl = w_in[l]
        w_proj = jnp.concatenate([wl[:, _IN_GLU:_IN_NAV], wl[:, _IN_GQ:_IN_GV]], axis=1).astype(BF16)
        w_vt = jnp.concatenate([wl[:, _IN_NAV:_IN_GQ], wl[:, _IN_GV:_IN_GATES]], axis=1).T.astype(BF16)
        w_gates = wl[:, _IN_GATES:].astype(BF16)
        mod_l = mod[l, :n_batch].reshape(n_batch * 6, 1, d)
        mod_c = jnp.broadcast_to(mod[l, n_batch], (n_batch, 6 * d)).reshape(n_batch * 6, 1, d)
        qg2, kg2 = tile2(q_norm_g[l]), tile2(k_norm_g[l])
        g1, g2 = row(norm1_g[l]), row(norm2_g[l])
        wa, wb, wc = w_conv_out[l].astype(BF16), w_na_out[l].astype(BF16), w_gqa_out[l].astype(BF16)
        wo = w_out[l].astype(BF16)
        wfi, wfo = w_ffn_in[l].astype(BF16), w_ffn_out[l].astype(BF16)
        conv_args = (conv_w[l], row(conv_b[l]), row(conv_ln_g[l]), row(conv_ln_b[l]))

        u, naq, nak, gq, gk, navt, gvt = _inproj(xl, mod_l, g1, w_proj, w_vt, qg2, kg2, bd, cos_t, sin_t,
                                                 n_batch=n_batch, seq=seq, tm=tm, rope=True)
        uc, naqc, nakc, gqc, gkc, navct, gvct = _inproj(xc, mod_c, g1, w_proj, w_vt, qg2, kg2, bd,
                                                        cos_t[:ctx_len], sin_t[:ctx_len],
                                                        n_batch=n_batch, seq=ctx_len, tm=tmc, rope=False)

        bias = _na_bias_tables(na_rpb[l], rows)
        br_b = _na_attention(naq, nak, navt, nakc, navct, bias, n_batch=n_batch, seq=seq, ctx_len=ctx_len)
        br_c = _gqa_attention(gq, gk, gkc, gvt, gvct, n_batch=n_batch, sq=seq, ctx_len=ctx_len, tq=128)
        xl = _merge(xl, mod_l, g1, w_gates, u, *conv_args, br_b, br_c, wa, wb, wc, wo,
                    n_batch=n_batch, seq=seq, tm=tm)
        xl = _ffn(xl, mod_l, g2, wfi, wfo, row(final_g), n_batch=n_batch, seq=seq, tm=tm, final_norm=last)

        if not last:
            cbr_b = _ctx_attention(naqc, nakc, navct, n_batch=n_batch, ctx_len=ctx_len, n_kv=NA_HEADS, group=1)
            cbr_c = _ctx_attention(gqc, gkc, gvct, n_batch=n_batch, ctx_len=ctx_len, n_kv=GQA_KV_HEADS,
                                   group=GQA_GROUP)
            xc = _merge(xc, mod_c, g1, w_gates, uc, *conv_args, cbr_b, cbr_c, wa, wb, wc, wo,
                        n_batch=n_batch, seq=ctx_len, tm=tmc)
            xc = _ffn(xc, mod_c, g2, wfi, wfo, row(final_g), n_batch=n_batch, seq=ctx_len, tm=tmc,
                      final_norm=False)
    return xl.reshape(n_batch, seq, d)
```

```python
import functools

import numpy as np
import jax
import jax.numpy as jnp
from jax import lax
from jax.experimental import pallas as pl
from jax.experimental.pallas import tpu as pltpu

F32 = jnp.float32
BF16 = jnp.bfloat16

GRID_W = 64
HEAD_DIM = 64
CONV_W = 512
CONV_K = 31
CONV_HALO = 16
CONV_ROWS = 64
NA_HEADS = 8
NA_W = NA_HEADS * HEAD_DIM
NA_KH = 8
NA_KW = 16
NA_QROWS = 4
NA_WROWS = 12
NA_AHEAD = 8
GQA_Q_HEADS = 8
GQA_KV_HEADS = 2
GQA_GROUP = GQA_Q_HEADS // GQA_KV_HEADS
GQA_Q_W = GQA_Q_HEADS * HEAD_DIM
GQA_KV_W = GQA_KV_HEADS * HEAD_DIM
GQA_AHEAD = 2
GQA_UNROLL = 6
GQA_Q_TILE = 128
TOKEN_TILE = 512
ROPE_THETA = 10000.0
EPS = 1e-6
NEG = -1e30
SCALE = HEAD_DIM ** -0.5
LOG2E = 1.4426950408889634
LANES = 128
SUBLANES = 8
KEY_CHUNK = 256
ONES_ROWS = 16
VMEM_LIMIT = 56 * 1024 * 1024

_IN_GLU = 0
_IN_NAQ = _IN_GLU + 2 * CONV_W
_IN_NAK = _IN_NAQ + NA_W
_IN_NAV = _IN_NAK + NA_W
_IN_GQ = _IN_NAV + NA_W
_IN_GK = _IN_GQ + GQA_Q_W
_IN_GV = _IN_GK + GQA_KV_W
_IN_GATES = _IN_GV + GQA_KV_W
_P_GLU = 0
_P_NAQ = _P_GLU + 2 * CONV_W
_P_NAK = _P_NAQ + NA_W
_P_GQ = _P_NAK + NA_W
_P_GK = _P_GQ + GQA_Q_W
_P_END = _P_GK + GQA_KV_W


def _params(n_axes):
    return pltpu.CompilerParams(dimension_semantics=("arbitrary",) * n_axes, vmem_limit_bytes=VMEM_LIMIT)


def _const_spec(shape):
    zeros = (0,) * len(shape)
    return pl.BlockSpec(shape, lambda *_: zeros, pipeline_mode=pl.Buffered(1))


def _norm_modulate(x, g, shift, scale):
    ms = jnp.mean(x * x, axis=-1, keepdims=True)
    h = x * lax.rsqrt(ms + EPS) * g
    return h * (1.0 + scale) + shift


def _with_ones(v_t):
    return jnp.concatenate([v_t, jnp.ones((ONES_ROWS, v_t.shape[1]), v_t.dtype)], axis=0)


def _mod_kernel(c_ref, w_ref, b_ref, o_ref):
    c = c_ref[...]
    s = c * jax.nn.sigmoid(c)
    o_ref[0] = jnp.dot(s, w_ref[0], preferred_element_type=F32, precision=lax.Precision.HIGHEST) + b_ref[0]


def _modulation(crows, w_mod, b_mod):
    n_layers, d, d6 = w_mod.shape
    r = crows.shape[0]
    nb = d6 // d
    return pl.pallas_call(
        _mod_kernel,
        grid=(n_layers, nb),
        in_specs=[
            pl.BlockSpec((r, d), lambda l, j: (0, 0)),
            pl.BlockSpec((1, d, d), lambda l, j: (l, 0, j)),
            pl.BlockSpec((1, 1, d), lambda l, j: (l, 0, j)),
        ],
        out_specs=pl.BlockSpec((1, r, d), lambda l, j: (l, 0, j)),
        out_shape=jax.ShapeDtypeStruct((n_layers, r, d6), F32),
        compiler_params=_params(2),
        name="modulation",
    )(crows, w_mod, b_mod.reshape(n_layers, 1, d6))


def _rope(t, cos, sin_signed, first_half):
    fwd = pltpu.roll(t, LANES - 16, axis=1)
    bwd = pltpu.roll(t, 16, axis=1)
    return t * cos + jnp.where(first_half, fwd, bwd) * sin_signed


def _inproj_kernel(x_ref, sh_ref, sc_ref, g_ref, w_ref, wvt_ref, qg_ref, kg_ref, bd_ref, cos_ref, sin_ref,
                   u_ref, naq_ref, nak_ref, gq_ref, gk_ref, navt_ref, gvt_ref, *, rope):
    h = _norm_modulate(x_ref[...], g_ref[...], sh_ref[0], sc_ref[0]).astype(BF16)

    def proj(lo, hi):
        return jnp.dot(h, w_ref[:, lo:hi], preferred_element_type=F32)

    q_scale = SCALE * LOG2E if rope else SCALE
    if rope:
        cos = cos_ref[...]
        sin = sin_ref[...]
        lane = lax.broadcasted_iota(jnp.int32, cos.shape, 1)
        first_half = (lane % 32) < 16

    def head_ms(t):
        w = t.shape[1]
        return jnp.dot((t * t).astype(BF16), bd_ref[:w, :w], preferred_element_type=F32)

    def finish(t, ms, gain):
        t = t * lax.rsqrt(ms + EPS) * gain
        if rope:
            t = _rope(t, cos, sin, first_half)
        return t

    q = proj(_P_GQ, _P_GK)
    wide = bd_ref.shape[0]
    for c in range(GQA_Q_W // wide):
        qc = q[:, c * wide:(c + 1) * wide]
        ms = head_ms(qc)
        for j in range(wide // LANES):
            sl = slice(j * LANES, (j + 1) * LANES)
            out_sl = slice(c * wide + j * LANES, c * wide + (j + 1) * LANES)
            gq_ref[:, out_sl] = (finish(qc[:, sl], ms[:, sl], qg_ref[...]) * q_scale).astype(BF16)
    k = proj(_P_GK, _P_END)
    gk_ref[...] = finish(k, head_ms(k), kg_ref[...]).astype(BF16)

    a = proj(_P_GLU, _P_NAQ)
    u_ref[...] = (a[:, :CONV_W] * jax.nn.sigmoid(a[:, CONV_W:])).astype(BF16)
    naq_ref[...] = (proj(_P_NAQ, _P_NAK) * q_scale).astype(BF16)
    nak_ref[...] = proj(_P_NAK, _P_GQ).astype(BF16)

    v_t = lax.dot_general(wvt_ref[...], h, (((1,), (1,)), ((), ())), preferred_element_type=F32).astype(BF16)
    for c in range(navt_ref.shape[1]):
        navt_ref[0, c] = v_t[:NA_W, c * KEY_CHUNK:(c + 1) * KEY_CHUNK]
        gvt_ref[0, c] = v_t[NA_W:, c * KEY_CHUNK:(c + 1) * KEY_CHUNK]


def _inproj(x2, mod3, norm_g, w, w_vt, qg2, kg2, bd, cos_t, sin_t, *, n_batch, seq, tm, rope):
    nt, d = x2.shape
    tps = seq // tm
    cpt = tm // KEY_CHUNK
    tok = lambda s, b: (b * tps + s, 0)
    outs = [CONV_W, NA_W, NA_W, GQA_Q_W, GQA_KV_W]
    vt_shape = lambda w_: jax.ShapeDtypeStruct((n_batch, seq // KEY_CHUNK, w_, KEY_CHUNK), BF16)
    vt_spec = lambda w_: pl.BlockSpec((1, cpt, w_, KEY_CHUNK), lambda s, b: (b, s, 0, 0))
    return pl.pallas_call(
        functools.partial(_inproj_kernel, rope=rope),
        grid=(tps, n_batch),
        in_specs=[
            pl.BlockSpec((tm, d), tok),
            pl.BlockSpec((1, 1, d), lambda s, b: (b * 6 + 0, 0, 0)),
            pl.BlockSpec((1, 1, d), lambda s, b: (b * 6 + 1, 0, 0)),
            _const_spec((1, d)),
            _const_spec(w.shape),
            _const_spec(w_vt.shape),
            _const_spec((1, LANES)),
            _const_spec((1, LANES)),
            _const_spec(bd.shape),
            pl.BlockSpec((tm, LANES), lambda s, b: (s, 0)),
            pl.BlockSpec((tm, LANES), lambda s, b: (s, 0)),
        ],
        out_specs=[pl.BlockSpec((tm, n), tok) for n in outs] + [vt_spec(NA_W), vt_spec(GQA_KV_W)],
        out_shape=[jax.ShapeDtypeStruct((nt, n), BF16) for n in outs] + [vt_shape(NA_W), vt_shape(GQA_KV_W)],
        compiler_params=_params(2),
        name="inproj_rope" if rope else "inproj_ctx",
    )(x2, mod3, mod3, norm_g, w, w_vt, qg2, kg2, bd, cos_t, sin_t)


def _conv_ln_swish(prev_ref, cur_ref, next_ref, w_ref, b_ref, lg_ref, lb_ref, win_ref, shift_ref, out_ref, first,
                   last):
    tl = cur_ref.shape[0]
    win_ref[0:CONV_HALO, :] = jnp.where(first, 0.0, prev_ref[...].astype(F32))
    win_ref[CONV_HALO:CONV_HALO + tl, :] = cur_ref[...].astype(F32)
    win_ref[CONV_HALO + tl:, :] = jnp.where(last, 0.0, next_ref[...].astype(F32))
    base = CONV_HALO - CONV_K // 2
    for r in range(SUBLANES):
        n_shift = tl + SUBLANES * (len(range(r, CONV_K, SUBLANES)) - 1)
        shift_ref[r, 0:n_shift, :] = win_ref[base + r:base + r + n_shift, :]
    for rb in range(0, tl, CONV_ROWS):
        acc = jnp.zeros((CONV_ROWS, CONV_W), F32)
        for k in range(CONV_K):
            off = rb + SUBLANES * (k // SUBLANES)
            acc = acc + shift_ref[k % SUBLANES, off:off + CONV_ROWS, :] * w_ref[k:k + 1, :]
        y = acc + b_ref[...]
        mu = jnp.mean(y, axis=-1, keepdims=True)
        yc = y - mu
        var = jnp.mean(yc * yc, axis=-1, keepdims=True)
        z = yc * lax.rsqrt(var + EPS) * lg_ref[...] + lb_ref[...]
        out_ref[rb:rb + CONV_ROWS, :] = (z * jax.nn.sigmoid(z)).astype(out_ref.dtype)


def _conv_halo_specs(n_tokens, tl):
    hb = tl // CONV_HALO
    last_hb = n_tokens // CONV_HALO - 1
    return [
        pl.BlockSpec((CONV_HALO, CONV_W), lambda i: (jnp.maximum(i * hb - 1, 0), 0)),
        pl.BlockSpec((tl, CONV_W), lambda i: (i, 0)),
        pl.BlockSpec((CONV_HALO, CONV_W), lambda i: (jnp.minimum((i + 1) * hb, last_hb), 0)),
    ]


def _na_bias_tables(rpb, rows):
    cols = np.arange(GRID_W)
    win0 = np.clip(cols - NA_KW // 2, 0, GRID_W - NA_KW)
    col_ok = (cols[:, None] >= win0[None, :]) & (cols[:, None] < win0[None, :] + NA_KW)
    pad = GRID_W - NA_KW
    rp = jnp.pad(rpb * LOG2E, ((0, 0), (0, 0), (pad, pad)))
    tiles = jnp.stack([rp[:, :, GRID_W - 1 - qc:2 * GRID_W - 1 - qc] for qc in range(GRID_W)], axis=-1)
    tiles = jnp.where(jnp.asarray(col_ok)[None, None], tiles, NEG * LOG2E)
    masked = jnp.full(tiles.shape[:1] + tiles.shape[2:], NEG * LOG2E, F32)
    n_groups = rows // NA_QROWS
    variants = []
    for g in (0, min(1, n_groups - 1), n_groups - 1):
        w = int(np.clip(NA_QROWS * g - NA_KH // 2, 0, rows - NA_WROWS))
        blocks = []
        for kl in range(NA_WROWS):
            kr = w + kl
            row = []
            for ql in range(NA_QROWS):
                r = NA_QROWS * g + ql
                r0 = int(np.clip(r - NA_KH // 2, 0, rows - NA_KH))
                row.append(tiles[:, kr - r + NA_KH - 1] if r0 <= kr < r0 + NA_KH else masked)
            blocks.append(jnp.concatenate(row, axis=-1))
        variants.append(jnp.concatenate(blocks, axis=1))
    return jnp.stack(variants)


def _na_kernel(q_ref, k_ref, vt_ref, kc_ref, vct_ref, bias_ref, o_ref, s_ref, ot_ref, *, n_chunks):
    g = pl.program_id(1)
    nq = NA_QROWS * GRID_W
    win_chunks = NA_WROWS * GRID_W // KEY_CHUNK
    c0 = jnp.clip(g - 1, 0, n_chunks - win_chunks)
    start = pl.multiple_of(c0 * KEY_CHUNK, KEY_CHUNK)
    q_t = q_ref[...].astype(F32).T.astype(BF16)
    zeros = jnp.zeros((HEAD_DIM, nq), BF16)
    units = [(h, t) for t in range(win_chunks + 1) for h in range(NA_HEADS)]

    def scores(h, t, s_ref):
        hp, j = divmod(h, LANES // HEAD_DIM)
        qh = q_t[h * HEAD_DIM:(h + 1) * HEAD_DIM]
        qz = jnp.concatenate([qh, zeros] if j == 0 else [zeros, qh], axis=0)
        pair = slice(hp * LANES, (hp + 1) * LANES)
        if t < win_chunks:
            rows_k = pl.ds(pl.multiple_of(start + t * KEY_CHUNK, KEY_CHUNK), KEY_CHUNK)
            s_ref[...] = (jnp.dot(k_ref[rows_k, pair], qz, preferred_element_type=F32)
                          + bias_ref[0, h, t * KEY_CHUNK:(t + 1) * KEY_CHUNK, :])
        else:
            s_ref[...] = jnp.dot(kc_ref[:, pair], qz, preferred_element_type=F32)

    def softmax_pv(h, t, s_ref, state):
        rows_h = slice(h * HEAD_DIM, (h + 1) * HEAD_DIM)
        v_t = vt_ref[0, c0 + t, rows_h, :] if t < win_chunks else vct_ref[0, 0, rows_h, :]
        s = s_ref[...]
        m_new = jnp.max(s, axis=0, keepdims=True)
        if state is not None:
            m_old, acc_old = state
            m_new = jnp.maximum(m_old, m_new)
        p = jnp.exp2((s - m_new).astype(BF16))
        acc = jnp.dot(_with_ones(v_t), p, preferred_element_type=F32)
        if state is not None:
            acc = acc + acc_old * jnp.exp2(m_old - m_new)
        return m_new, acc

    n_buf = NA_AHEAD + 1
    for u in range(NA_AHEAD):
        scores(*units[u], s_ref.at[u % n_buf])
    state = [None] * NA_HEADS
    for u, (h, t) in enumerate(units):
        if u + NA_AHEAD < len(units):
            scores(*units[u + NA_AHEAD], s_ref.at[(u + NA_AHEAD) % n_buf])
        state[h] = softmax_pv(h, t, s_ref.at[u % n_buf], state[h])
        if t == win_chunks:
            acc = state[h][1]
            ot_ref[h * HEAD_DIM:(h + 1) * HEAD_DIM, :] = acc[:HEAD_DIM] * (1.0 / acc[HEAD_DIM:HEAD_DIM + 1])
    o_ref[...] = ot_ref[...].T.astype(BF16)


def _na_attention(q, k, vt, kc, vct, bias, *, n_batch, seq, ctx_len):
    rows = seq // GRID_W
    n_groups = rows // NA_QROWS
    nq = NA_QROWS * GRID_W
    variant = lambda g: jnp.where(g == 0, 0, jnp.where(g == n_groups - 1, 2, 1))
    return pl.pallas_call(
        functools.partial(_na_kernel, n_chunks=seq // KEY_CHUNK),
        grid=(n_batch, n_groups),
        in_specs=[
            pl.BlockSpec((nq, NA_W), lambda b, g: (b * n_groups + g, 0)),
            pl.BlockSpec((seq, NA_W), lambda b, g: (b, 0)),
            pl.BlockSpec((1,) + vt.shape[1:], lambda b, g: (b, 0, 0, 0)),
            pl.BlockSpec((ctx_len, NA_W), lambda b, g: (b, 0)),
            pl.BlockSpec((1,) + vct.shape[1:], lambda b, g: (b, 0, 0, 0)),
            pl.BlockSpec((1,) + bias.shape[1:], lambda b, g: (variant(g), 0, 0, 0)),
        ],
        out_specs=pl.BlockSpec((nq, NA_W), lambda b, g: (b * n_groups + g, 0)),
        out_shape=jax.ShapeDtypeStruct(q.shape, BF16),
        scratch_shapes=[
            pltpu.VMEM((NA_AHEAD + 1, KEY_CHUNK, nq), F32),
            pltpu.VMEM((NA_W, nq), F32),
        ],
        compiler_params=_params(2),
        name="na_attention",
    )(q, k, vt, kc, vct, bias)


def _gqa_kernel(q_ref, k_ref, kc_ref, vt_ref, vct_ref, o_ref, qt_ref, s_ref, m_ref, acc_ref, *, n_chunks):
    tq = q_ref.shape[0]
    q_t = q_ref[...].astype(F32).T
    for hk in range(GQA_KV_HEADS):
        for gi in range(GQA_GROUP):
            hq = hk * GQA_GROUP + gi
            qt_ref[hk, :, gi * tq:(gi + 1) * tq] = q_t[hq * HEAD_DIM:(hq + 1) * HEAD_DIM, :].astype(BF16)
    m_ref[...] = jnp.full(m_ref.shape, NEG, F32)
    acc_ref[...] = jnp.zeros(acc_ref.shape, F32)

    def scores(k_blk, s_buf):
        for hk in range(GQA_KV_HEADS):
            kch = k_blk[:, hk * HEAD_DIM:(hk + 1) * HEAD_DIM]
            s_buf[hk] = jnp.dot(kch, qt_ref[hk], preferred_element_type=F32)

    def softmax_pv(v_t, s_ref):
        for hk in range(GQA_KV_HEADS):
            s = s_ref[hk]
            m_old = m_ref[hk]
            m_new = jnp.maximum(m_old, jnp.max(s, axis=0, keepdims=True))
            p = jnp.exp2(s - m_new).astype(BF16)
            pv = jnp.dot(_with_ones(v_t[hk * HEAD_DIM:(hk + 1) * HEAD_DIM]), p, preferred_element_type=F32)
            acc_ref[hk] = acc_ref[hk] * jnp.exp2(m_old - m_new) + pv
            m_ref[hk] = m_new

    def k_rows(c):
        if isinstance(c, int) and c == n_chunks:
            return kc_ref[...]
        return k_ref[pl.ds(pl.multiple_of(c * KEY_CHUNK, KEY_CHUNK), KEY_CHUNK), :]

    def v_cols(c):
        return vct_ref[0, 0] if isinstance(c, int) and c == n_chunks else vt_ref[0, c]

    n_buf = GQA_AHEAD + 1
    for c in range(GQA_AHEAD):
        scores(k_rows(c), s_ref.at[c % n_buf])

    def steps(c0, n):
        for j in range(n):
            c = c0 + j
            if not (isinstance(c, int) and c + GQA_AHEAD > n_chunks):
                scores(k_rows(c + GQA_AHEAD), s_ref.at[(j + GQA_AHEAD) % n_buf])
            softmax_pv(v_cols(c), s_ref.at[j % n_buf])

    trips = (n_chunks - GQA_AHEAD) // GQA_UNROLL

    def body(i, carry):
        steps(i * GQA_UNROLL, GQA_UNROLL)
        return carry

    lax.fori_loop(0, trips, body, 0)
    steps(trips * GQA_UNROLL, n_chunks + 1 - trips * GQA_UNROLL)

    heads_t = []
    for hk in range(GQA_KV_HEADS):
        acc = acc_ref[hk]
        out_t = acc[:HEAD_DIM] * (1.0 / acc[HEAD_DIM:HEAD_DIM + 1])
        heads_t += [out_t[:, gi * tq:(gi + 1) * tq] for gi in range(GQA_GROUP)]
    o_ref[...] = jnp.concatenate(heads_t, axis=0).T.astype(BF16)


def _gqa_attention(q, k, kc, vt, vct, *, n_batch, sq, ctx_len, tq):
    tpq = sq // tq
    m = GQA_GROUP * tq
    n_chunks = sq // KEY_CHUNK
    assert n_chunks >= GQA_AHEAD and GQA_UNROLL % (GQA_AHEAD + 1) == 0 and ctx_len == KEY_CHUNK
    return pl.pallas_call(
        functools.partial(_gqa_kernel, n_chunks=n_chunks),
        grid=(n_batch, tpq),
        in_specs=[
            pl.BlockSpec((tq, GQA_Q_W), lambda b, i: (b * tpq + i, 0)),
            pl.BlockSpec((sq, GQA_KV_W), lambda b, i: (b, 0)),
            pl.BlockSpec((ctx_len, GQA_KV_W), lambda b, i: (b, 0)),
            pl.BlockSpec((1,) + vt.shape[1:], lambda b, i: (b, 0, 0, 0)),
            pl.BlockSpec((1,) + vct.shape[1:], lambda b, i: (b, 0, 0, 0)),
        ],
        out_specs=pl.BlockSpec((tq, GQA_Q_W), lambda b, i: (b * tpq + i, 0)),
        out_shape=jax.ShapeDtypeStruct(q.shape, BF16),
        scratch_shapes=[
            pltpu.VMEM((GQA_KV_HEADS, HEAD_DIM, m), BF16),
            pltpu.VMEM((GQA_AHEAD + 1, GQA_KV_HEADS, KEY_CHUNK, m), F32),
            pltpu.VMEM((GQA_KV_HEADS, 1, m), F32),
            pltpu.VMEM((GQA_KV_HEADS, HEAD_DIM + ONES_ROWS, m), F32),
        ],
        compiler_params=_params(2),
        name="gqa_attention",
    )(q, k, kc, vt, vct)


def _ctx_attn_kernel(q_ref, k_ref, vt_ref, o_ref, ot_ref, *, n_kv, group):
    tq = q_ref.shape[0]
    q_t = q_ref[...].astype(F32).T.astype(BF16)
    for hk in range(n_kv):
        rows_k = slice(hk * HEAD_DIM, (hk + 1) * HEAD_DIM)
        qs = jnp.concatenate([q_t[(hk * group + gi) * HEAD_DIM:(hk * group + gi + 1) * HEAD_DIM]
                              for gi in range(group)], axis=1)
        s = jnp.dot(k_ref[:, rows_k], qs, preferred_element_type=F32)
        p = jnp.exp(s - jnp.max(s, axis=0, keepdims=True)).astype(BF16)
        acc = jnp.dot(_with_ones(vt_ref[0, 0, rows_k, :]), p, preferred_element_type=F32)
        out_t = acc[:HEAD_DIM] * (1.0 / acc[HEAD_DIM:HEAD_DIM + 1])
        for gi in range(group):
            hq = hk * group + gi
            ot_ref[hq * HEAD_DIM:(hq + 1) * HEAD_DIM, :] = out_t[:, gi * tq:(gi + 1) * tq]
    o_ref[...] = ot_ref[...].T.astype(BF16)


def _ctx_attention(q, k, vt, *, n_batch, ctx_len, n_kv, group):
    return pl.pallas_call(
        functools.partial(_ctx_attn_kernel, n_kv=n_kv, group=group),
        grid=(n_batch,),
        in_specs=[
            pl.BlockSpec((ctx_len, q.shape[1]), lambda b: (b, 0)),
            pl.BlockSpec((ctx_len, k.shape[1]), lambda b: (b, 0)),
            pl.BlockSpec((1,) + vt.shape[1:], lambda b: (b, 0, 0, 0)),
        ],
        out_specs=pl.BlockSpec((ctx_len, q.shape[1]), lambda b: (b, 0)),
        out_shape=jax.ShapeDtypeStruct(q.shape, BF16),
        scratch_shapes=[pltpu.VMEM((q.shape[1], ctx_len), F32)],
        compiler_params=_params(1),
        name=f"ctx_attention_g{group}",
    )(q, k, vt)


def _merge_kernel(x_ref, sh_ref, sc_ref, gate_ref, g_ref, wg_ref, uprev_ref, u_ref, unext_ref, cw_ref, cb_ref,
                  lg_ref, lb_ref, b_ref, c_ref, wa_ref, wb_ref, wc_ref, wo_ref, o_ref, win_ref, shift_ref, bra_ref,
                  *, tiles_per_seq):
    pos = pl.program_id(0) % tiles_per_seq
    x = x_ref[...]
    d = x.shape[1]
    h = _norm_modulate(x, g_ref[...], sh_ref[0], sc_ref[0]).astype(BF16)

    def gated(j, br, w_ref):
        gate = jax.nn.sigmoid(jnp.dot(h, wg_ref[:, j * d:(j + 1) * d], preferred_element_type=F32))
        return gate * jnp.dot(br, w_ref[...], preferred_element_type=F32)

    y = gated(1, b_ref[...], wb_ref) + gated(2, c_ref[...], wc_ref)
    gate_a = jax.nn.sigmoid(jnp.dot(h, wg_ref[:, 0:d], preferred_element_type=F32))
    _conv_ln_swish(uprev_ref, u_ref, unext_ref, cw_ref, cb_ref, lg_ref, lb_ref, win_ref, shift_ref, bra_ref,
                   pos == 0, pos == tiles_per_seq - 1)
    y = y + gate_a * jnp.dot(bra_ref[...], wa_ref[...], preferred_element_type=F32)
    o_ref[...] = x + gate_ref[0] * jnp.dot(y.astype(BF16), wo_ref[...], preferred_element_type=F32)


def _merge(x2, mod3, norm_g, w_gates, u, conv_w, conv_b, ln_g, ln_b, br_b, br_c, wa, wb, wc, wo, *, n_batch, seq,
           tm):
    nt, d = x2.shape
    tps = seq // tm
    tok = lambda i: (i, 0)
    mrow = lambda j: (lambda i: ((i // tps) * 6 + j, 0, 0))
    return pl.pallas_call(
        functools.partial(_merge_kernel, tiles_per_seq=tps),
        grid=(nt // tm,),
        in_specs=[
            pl.BlockSpec((tm, d), tok),
            pl.BlockSpec((1, 1, d), mrow(0)),
            pl.BlockSpec((1, 1, d), mrow(1)),
            pl.BlockSpec((1, 1, d), mrow(2)),
            _const_spec((1, d)),
            _const_spec(w_gates.shape),
            *_conv_halo_specs(nt, tm),
            _const_spec((CONV_K, CONV_W)),
            _const_spec((1, CONV_W)),
            _const_spec((1, CONV_W)),
            _const_spec((1, CONV_W)),
            pl.BlockSpec((tm, NA_W), tok),
            pl.BlockSpec((tm, GQA_Q_W), tok),
            _const_spec(wa.shape),
            _const_spec(wb.shape),
            _const_spec(wc.shape),
            _const_spec(wo.shape),
        ],
        out_specs=pl.BlockSpec((tm, d), tok),
        out_shape=jax.ShapeDtypeStruct((nt, d), F32),
        scratch_shapes=[pltpu.VMEM((tm + 2 * CONV_HALO, CONV_W), F32),
                        pltpu.VMEM((SUBLANES, tm + 2 * CONV_HALO, CONV_W), F32),
                        pltpu.VMEM((tm, CONV_W), BF16)],
        compiler_params=_params(1),
        name="merge",
    )(x2, mod3, mod3, mod3, norm_g, w_gates, u, u, u, conv_w, conv_b, ln_g, ln_b, br_b, br_c, wa, wb, wc, wo)


def _ffn_kernel(x_ref, sh_ref, sc_ref, gate_ref, g_ref, wi_ref, wo_ref, fg_ref, o_ref, *, n_split, final_norm):
    x = x_ref[...]
    hidden = wo_ref.shape[0]
    h = _norm_modulate(x, g_ref[...], sh_ref[0], sc_ref[0]).astype(BF16)
    step = hidden // n_split
    y = None
    for j in range(n_split):
        gt = jnp.dot(h, wi_ref[:, j * step:(j + 1) * step], preferred_element_type=F32)
        up = jnp.dot(h, wi_ref[:, hidden + j * step:hidden + (j + 1) * step], preferred_element_type=F32)
        act = (gt * jax.nn.sigmoid(gt) * up).astype(BF16)
        t = jnp.dot(act, wo_ref[j * step:(j + 1) * step, :], preferred_element_type=F32)
        y = t if y is None else y + t
    out = x + gate_ref[0] * y
    if final_norm:
        ms = jnp.mean(out * out, axis=-1, keepdims=True)
        out = out * lax.rsqrt(ms + EPS) * fg_ref[...]
    o_ref[...] = out


def _ffn(x2, mod3, norm_g, w_in, w_out, final_g, *, n_batch, seq, tm, final_norm):
    nt, d = x2.shape
    tps = seq // tm
    tok = lambda i: (i, 0)
    mrow = lambda j: (lambda i: ((i // tps) * 6 + j, 0, 0))
    return pl.pallas_call(
        functools.partial(_ffn_kernel, n_split=1, final_norm=final_norm),
        grid=(nt // tm,),
        in_specs=[
            pl.BlockSpec((tm, d), tok),
            pl.BlockSpec((1, 1, d), mrow(3)),
            pl.BlockSpec((1, 1, d), mrow(4)),
            pl.BlockSpec((1, 1, d), mrow(5)),
            _const_spec((1, d)),
            _const_spec(w_in.shape),
            _const_spec(w_out.shape),
            _const_spec((1, d)),
        ],
        out_specs=pl.BlockSpec((tm, d), tok),
        out_shape=jax.ShapeDtypeStruct((nt, d), F32),
        compiler_params=_params(1),
        name="ffn_final" if final_norm else "ffn",
    )(x2, mod3, mod3, mod3, norm_g, w_in, w_out, final_g)


def _rope_tables(seq):
    t = jnp.arange(seq)
    pos_row = (t // GRID_W).astype(F32)
    pos_col = (t % GRID_W).astype(F32)
    half = HEAD_DIM // 2
    freqs = jnp.power(ROPE_THETA, -jnp.arange(0, half, 2, dtype=F32) / half)

    def part(pos):
        ang = pos[:, None] * freqs
        c, s = jnp.cos(ang), jnp.sin(ang)
        return jnp.concatenate([c, c], axis=-1), jnp.concatenate([-s, s], axis=-1)

    cr, sr = part(pos_row)
    cc, sc = part(pos_col)
    cos = jnp.concatenate([cr, cc], axis=-1)
    sin = jnp.concatenate([sr, sc], axis=-1)
    return jnp.tile(cos, (1, LANES // HEAD_DIM)), jnp.tile(sin, (1, LANES // HEAD_DIM))


def kernel(x, c, ctx, c_ctx, w_mod, b_mod, norm1_g, norm2_g, w_in, conv_w, conv_b, conv_ln_g, conv_ln_b,
           w_conv_out, na_rpb, w_na_out, q_norm_g, k_norm_g, w_gqa_out, w_out, w_ffn_in, w_ffn_out, final_g):
    n_batch, seq, d = x.shape
    ctx_len = ctx.shape[1]
    depth = w_mod.shape[0]
    rows = seq // GRID_W
    assert seq % (NA_QROWS * GRID_W) == 0 and rows >= NA_WROWS and ctx_len == KEY_CHUNK
    tm = TOKEN_TILE
    tmc = ctx_len

    n_rows = -(-(n_batch + 1) // 8) * 8
    crows = jnp.zeros((n_rows, d), F32).at[:n_batch].set(c).at[n_batch].set(c_ctx)
    mod = _modulation(crows, w_mod, b_mod)

    cos_t, sin_t = _rope_tables(seq)
    bd = jnp.asarray(np.kron(np.eye(KEY_CHUNK // HEAD_DIM), np.full((HEAD_DIM, HEAD_DIM), 1.0 / HEAD_DIM)), BF16)
    tile2 = lambda g: jnp.tile(g, LANES // HEAD_DIM).reshape(1, LANES)
    row = lambda g: g.reshape(1, -1)

    xl = x.reshape(n_batch * seq, d)
    xc = ctx.reshape(n_batch * ctx_len, d)
    for l in range(depth):
        last = l == depth - 1
        wl = w_in[l]
        w_proj = jnp.concatenate([wl[:, _IN_GLU:_IN_NAV], wl[:, _IN_GQ:_IN_GV]], axis=1).astype(BF16)
        w_vt = jnp.concatenate([wl[:, _IN_NAV:_IN_GQ], wl[:, _IN_GV:_IN_GATES]], axis=1).T.astype(BF16)
        w_gates = wl[:, _IN_GATES:].astype(BF16)
        mod_l = mod[l, :n_batch].reshape(n_batch * 6, 1, d)
        mod_c = jnp.broadcast_to(mod[l, n_batch], (n_batch, 6 * d)).reshape(n_batch * 6, 1, d)
        qg2, kg2 = tile2(q_norm_g[l]), tile2(k_norm_g[l])
        g1, g2 = row(norm1_g[l]), row(norm2_g[l])
        wa, wb, wc = w_conv_out[l].astype(BF16), w_na_out[l].astype(BF16), w_gqa_out[l].astype(BF16)
        wo = w_out[l].astype(BF16)
        wfi, wfo = w_ffn_in[l].astype(BF16), w_ffn_out[l].astype(BF16)
        conv_args = (conv_w[l], row(conv_b[l]), row(conv_ln_g[l]), row(conv_ln_b[l]))

        u, naq, nak, gq, gk, navt, gvt = _inproj(xl, mod_l, g1, w_proj, w_vt, qg2, kg2, bd, cos_t, sin_t,
                                                 n_batch=n_batch, seq=seq, tm=tm, rope=True)
        uc, naqc, nakc, gqc, gkc, navct, gvct = _inproj(xc, mod_c, g1, w_proj, w_vt, qg2, kg2, bd,
                                                        cos_t[:ctx_len], sin_t[:ctx_len],
                                                        n_batch=n_batch, seq=ctx_len, tm=tmc, rope=False)

        bias = _na_bias_tables(na_rpb[l], rows)
        br_b = _na_attention(naq, nak, navt, nakc, navct, bias, n_batch=n_batch, seq=seq, ctx_len=ctx_len)
        br_c = _gqa_attention(gq, gk, gkc, gvt, gvct, n_batch=n_batch, sq=seq, ctx_len=ctx_len, tq=GQA_Q_TILE)
        xl = _merge(xl, mod_l, g1, w_gates, u, *conv_args, br_b, br_c, wa, wb, wc, wo,
                    n_batch=n_batch, seq=seq, tm=tm)
        xl = _ffn(xl, mod_l, g2, wfi, wfo, row(final_g), n_batch=n_batch, seq=seq, tm=tm, final_norm=last)

        if not last:
            cbr_b = _ctx_attention(naqc, nakc, navct, n_batch=n_batch, ctx_len=ctx_len, n_kv=NA_HEADS, group=1)
            cbr_c = _ctx_attention(gqc, gkc, gvct, n_batch=n_batch, ctx_len=ctx_len, n_kv=GQA_KV_HEADS,
                                   group=GQA_GROUP)
            xc = _merge(xc, mod_c, g1, w_gates, uc, *conv_args, cbr_b, cbr_c, wa, wb, wc, wo,
                        n_batch=n_batch, seq=ctx_len, tm=tmc)
            xc = _ffn(xc, mod_c, g2, wfi, wfo, row(final_g), n_batch=n_batch, seq=ctx_len, tm=tmc,
                      final_norm=False)
    return xl.reshape(n_batch, seq, d)
```

```python
import functools

import numpy as np
import jax
import jax.numpy as jnp
from jax import lax
from jax.experimental import pallas as pl
from jax.experimental.pallas import tpu as pltpu

F32 = jnp.float32
BF16 = jnp.bfloat16

GRID_W = 64
HEAD_DIM = 64
CONV_W = 512
CONV_K = 31
CONV_HALO = 16
CONV_ROWS = 64
NA_HEADS = 8
NA_W = NA_HEADS * HEAD_DIM
NA_KH = 8
NA_KW = 16
NA_QROWS = 4
NA_WROWS = 12
NA_AHEAD = 8
GQA_Q_HEADS = 8
GQA_KV_HEADS = 2
GQA_GROUP = GQA_Q_HEADS // GQA_KV_HEADS
GQA_Q_W = GQA_Q_HEADS * HEAD_DIM
GQA_KV_W = GQA_KV_HEADS * HEAD_DIM
GQA_AHEAD = 2
GQA_UNROLL = 6
GQA_Q_TILE = 128
TOKEN_TILE = 512
ROPE_THETA = 10000.0
EPS = 1e-6
NEG = -1e30
SCALE = HEAD_DIM ** -0.5
LOG2E = 1.4426950408889634
LANES = 128
SUBLANES = 8
KEY_CHUNK = 256
ONES_ROWS = 16
VMEM_LIMIT = 56 * 1024 * 1024

_IN_GLU = 0
_IN_NAQ = _IN_GLU + 2 * CONV_W
_IN_NAK = _IN_NAQ + NA_W
_IN_NAV = _IN_NAK + NA_W
_IN_GQ = _IN_NAV + NA_W
_IN_GK = _IN_GQ + GQA_Q_W
_IN_GV = _IN_GK + GQA_KV_W
_IN_GATES = _IN_GV + GQA_KV_W
_P_GLU = 0
_P_NAQ = _P_GLU + 2 * CONV_W
_P_NAK = _P_NAQ + NA_W
_P_GQ = _P_NAK + NA_W
_P_GK = _P_GQ + GQA_Q_W
_P_END = _P_GK + GQA_KV_W


def _params(n_axes):
    return pltpu.CompilerParams(dimension_semantics=("arbitrary",) * n_axes, vmem_limit_bytes=VMEM_LIMIT)


def _const_spec(shape):
    zeros = (0,) * len(shape)
    return pl.BlockSpec(shape, lambda *_: zeros, pipeline_mode=pl.Buffered(1))


def _norm_modulate(x, g, shift, scale):
    ms = jnp.mean(x * x, axis=-1, keepdims=True)
    h = x * lax.rsqrt(ms + EPS) * g
    return h * (1.0 + scale) + shift


def _with_ones(v_t):
    return jnp.concatenate([v_t, jnp.ones((ONES_ROWS, v_t.shape[1]), v_t.dtype)], axis=0)


def _mod_kernel(c_ref, w_ref, b_ref, o_ref):
    c = c_ref[...]
    s = c * jax.nn.sigmoid(c)
    o_ref[0] = jnp.dot(s, w_ref[0], preferred_element_type=F32, precision=lax.Precision.HIGHEST) + b_ref[0]


def _modulation(crows, w_mod, b_mod):
    n_layers, d, d6 = w_mod.shape
    r = crows.shape[0]
    nb = d6 // d
    return pl.pallas_call(
        _mod_kernel,
        grid=(n_layers, nb),
        in_specs=[
            pl.BlockSpec((r, d), lambda l, j: (0, 0)),
            pl.BlockSpec((1, d, d), lambda l, j: (l, 0, j)),
            pl.BlockSpec((1, 1, d), lambda l, j: (l, 0, j)),
        ],
        out_specs=pl.BlockSpec((1, r, d), lambda l, j: (l, 0, j)),
        out_shape=jax.ShapeDtypeStruct((n_layers, r, d6), F32),
        compiler_params=_params(2),
        name="modulation",
    )(crows, w_mod, b_mod.reshape(n_layers, 1, d6))


def _rope(t, cos, sin_signed, first_half):
    fwd = pltpu.roll(t, LANES - 16, axis=1)
    bwd = pltpu.roll(t, 16, axis=1)
    return t * cos + jnp.where(first_half, fwd, bwd) * sin_signed


def _inproj_kernel(x_ref, sh_ref, sc_ref, g_ref, w_ref, wvt_ref, qg_ref, kg_ref, bd_ref, cos_ref, sin_ref,
                   u_ref, naq_ref, nak_ref, gq_ref, gk_ref, navt_ref, gvt_ref, *, rope):
    h = _norm_modulate(x_ref[...], g_ref[...], sh_ref[0], sc_ref[0]).astype(BF16)

    def proj(lo, hi):
        return jnp.dot(h, w_ref[:, lo:hi], preferred_element_type=F32)

    q_scale = SCALE * LOG2E if rope else SCALE
    if rope:
        cos = cos_ref[...]
        sin = sin_ref[...]
        lane = lax.broadcasted_iota(jnp.int32, cos.shape, 1)
        first_half = (lane % 32) < 16

    def head_ms(t):
        w = t.shape[1]
        return jnp.dot((t * t).astype(BF16), bd_ref[:w, :w], preferred_element_type=F32)

    def finish(t, ms, gain):
        t = t * lax.rsqrt(ms + EPS) * gain
        if rope:
            t = _rope(t, cos, sin, first_half)
        return t

    q = proj(_P_GQ, _P_GK)
    wide = bd_ref.shape[0]
    for c in range(GQA_Q_W // wide):
        qc = q[:, c * wide:(c + 1) * wide]
        ms = head_ms(qc)
        for j in range(wide // LANES):
            sl = slice(j * LANES, (j + 1) * LANES)
            out_sl = slice(c * wide + j * LANES, c * wide + (j + 1) * LANES)
            gq_ref[:, out_sl] = (finish(qc[:, sl], ms[:, sl], qg_ref[...]) * q_scale).astype(BF16)
    k = proj(_P_GK, _P_END)
    gk_ref[...] = finish(k, head_ms(k), kg_ref[...]).astype(BF16)

    a = proj(_P_GLU, _P_NAQ)
    u_ref[...] = (a[:, :CONV_W] * jax.nn.sigmoid(a[:, CONV_W:])).astype(BF16)
    naq_ref[...] = (proj(_P_NAQ, _P_NAK) * q_scale).astype(BF16)
    nak_ref[...] = proj(_P_NAK, _P_GQ).astype(BF16)

    v_t = lax.dot_general(wvt_ref[...], h, (((1,), (1,)), ((), ())), preferred_element_type=F32).astype(BF16)
    for c in range(navt_ref.shape[1]):
        navt_ref[0, c] = v_t[:NA_W, c * KEY_CHUNK:(c + 1) * KEY_CHUNK]
        gvt_ref[0, c] = v_t[NA_W:, c * KEY_CHUNK:(c + 1) * KEY_CHUNK]


def _inproj(x2, mod3, norm_g, w, w_vt, qg2, kg2, bd, cos_t, sin_t, *, n_batch, seq, tm, rope):
    nt, d = x2.shape
    tps = seq // tm
    cpt = tm // KEY_CHUNK
    tok = lambda s, b: (b * tps + s, 0)
    outs = [CONV_W, NA_W, NA_W, GQA_Q_W, GQA_KV_W]
    vt_shape = lambda w_: jax.ShapeDtypeStruct((n_batch, seq // KEY_CHUNK, w_, KEY_CHUNK), BF16)
    vt_spec = lambda w_: pl.BlockSpec((1, cpt, w_, KEY_CHUNK), lambda s, b: (b, s, 0, 0))
    return pl.pallas_call(
        functools.partial(_inproj_kernel, rope=rope),
        grid=(tps, n_batch),
        in_specs=[
            pl.BlockSpec((tm, d), tok),
            pl.BlockSpec((1, 1, d), lambda s, b: (b * 6 + 0, 0, 0)),
            pl.BlockSpec((1, 1, d), lambda s, b: (b * 6 + 1, 0, 0)),
            _const_spec((1, d)),
            _const_spec(w.shape),
            _const_spec(w_vt.shape),
            _const_spec((1, LANES)),
            _const_spec((1, LANES)),
            _const_spec(bd.shape),
            pl.BlockSpec((tm, LANES), lambda s, b: (s, 0)),
            pl.BlockSpec((tm, LANES), lambda s, b: (s, 0)),
        ],
        out_specs=[pl.BlockSpec((tm, n), tok) for n in outs] + [vt_spec(NA_W), vt_spec(GQA_KV_W)],
        out_shape=[jax.ShapeDtypeStruct((nt, n), BF16) for n in outs] + [vt_shape(NA_W), vt_shape(GQA_KV_W)],
        compiler_params=_params(2),
        name="inproj_rope" if rope else "inproj_ctx",
    )(x2, mod3, mod3, norm_g, w, w_vt, qg2, kg2, bd, cos_t, sin_t)


def _conv_ln_swish(prev_ref, cur_ref, next_ref, w_ref, b_ref, lg_ref, lb_ref, win_ref, shift_ref, out_ref, first,
                   last):
    tl = cur_ref.shape[0]
    win_ref[0:CONV_HALO, :] = jnp.where(first, 0.0, prev_ref[...].astype(F32))
    win_ref[CONV_HALO:CONV_HALO + tl, :] = cur_ref[...].astype(F32)
    win_ref[CONV_HALO + tl:, :] = jnp.where(last, 0.0, next_ref[...].astype(F32))
    base = CONV_HALO - CONV_K // 2
    for r in range(SUBLANES):
        n_shift = tl + SUBLANES * (len(range(r, CONV_K, SUBLANES)) - 1)
        shift_ref[r, 0:n_shift, :] = win_ref[base + r:base + r + n_shift, :]
    for rb in range(0, tl, CONV_ROWS):
        acc = jnp.zeros((CONV_ROWS, CONV_W), F32)
        for k in range(CONV_K):
            off = rb + SUBLANES * (k // SUBLANES)
            acc = acc + shift_ref[k % SUBLANES, off:off + CONV_ROWS, :] * w_ref[k:k + 1, :]
        y = acc + b_ref[...]
        mu = jnp.mean(y, axis=-1, keepdims=True)
        yc = y - mu
        var = jnp.mean(yc * yc, axis=-1, keepdims=True)
        z = yc * lax.rsqrt(var + EPS) * lg_ref[...] + lb_ref[...]
        out_ref[rb:rb + CONV_ROWS, :] = (z * jax.nn.sigmoid(z)).astype(out_ref.dtype)


def _conv_halo_specs(n_tokens, tl):
    hb = tl // CONV_HALO
    last_hb = n_tokens // CONV_HALO - 1
    return [
        pl.BlockSpec((CONV_HALO, CONV_W), lambda i: (jnp.maximum(i * hb - 1, 0), 0)),
        pl.BlockSpec((tl, CONV_W), lambda i: (i, 0)),
        pl.BlockSpec((CONV_HALO, CONV_W), lambda i: (jnp.minimum((i + 1) * hb, last_hb), 0)),
    ]


def _na_bias_tables(rpb, rows):
    cols = np.arange(GRID_W)
    win0 = np.clip(cols - NA_KW // 2, 0, GRID_W - NA_KW)
    col_ok = (cols[:, None] >= win0[None, :]) & (cols[:, None] < win0[None, :] + NA_KW)
    pad = GRID_W - NA_KW
    rp = jnp.pad(rpb * LOG2E, ((0, 0), (0, 0), (pad, pad)))
    tiles = jnp.stack([rp[:, :, GRID_W - 1 - qc:2 * GRID_W - 1 - qc] for qc in range(GRID_W)], axis=-1)
    tiles = jnp.where(jnp.asarray(col_ok)[None, None], tiles, NEG * LOG2E)
    masked = jnp.full(tiles.shape[:1] + tiles.shape[2:], NEG * LOG2E, F32)
    n_groups = rows // NA_QROWS
    variants = []
    for g in (0, min(1, n_groups - 1), n_groups - 1):
        w = int(np.clip(NA_QROWS * g - NA_KH // 2, 0, rows - NA_WROWS))
        blocks = []
        for kl in range(NA_WROWS):
            kr = w + kl
            row = []
            for ql in range(NA_QROWS):
                r = NA_QROWS * g + ql
                r0 = int(np.clip(r - NA_KH // 2, 0, rows - NA_KH))
                row.append(tiles[:, kr - r + NA_KH - 1] if r0 <= kr < r0 + NA_KH else masked)
            blocks.append(jnp.concatenate(row, axis=-1))
        variants.append(jnp.concatenate(blocks, axis=1))
    return jnp.stack(variants)


def _na_kernel(q_ref, k_ref, vt_ref, kc_ref, vct_ref, bias_ref, o_ref, s_ref, ot_ref, *, n_chunks):
    g = pl.program_id(1)
    nq = NA_QROWS * GRID_W
    win_chunks = NA_WROWS * GRID_W // KEY_CHUNK
    c0 = jnp.clip(g - 1, 0, n_chunks - win_chunks)
    start = pl.multiple_of(c0 * KEY_CHUNK, KEY_CHUNK)
    q_t = q_ref[...].astype(F32).T.astype(BF16)
    zeros = jnp.zeros((HEAD_DIM, nq), BF16)
    units = [(h, t) for t in range(win_chunks + 1) for h in range(NA_HEADS)]

    def scores(h, t, s_ref):
        hp, j = divmod(h, LANES // HEAD_DIM)
        qh = q_t[h * HEAD_DIM:(h + 1) * HEAD_DIM]
        qz = jnp.concatenate([qh, zeros] if j == 0 else [zeros, qh], axis=0)
        pair = slice(hp * LANES, (hp + 1) * LANES)
        if t < win_chunks:
            rows_k = pl.ds(pl.multiple_of(start + t * KEY_CHUNK, KEY_CHUNK), KEY_CHUNK)
            s_ref[...] = (jnp.dot(k_ref[rows_k, pair], qz, preferred_element_type=F32)
                          + bias_ref[0, h, t * KEY_CHUNK:(t + 1) * KEY_CHUNK, :])
        else:
            s_ref[...] = jnp.dot(kc_ref[:, pair], qz, preferred_element_type=F32)

    def softmax_pv(h, t, s_ref, state):
        rows_h = slice(h * HEAD_DIM, (h + 1) * HEAD_DIM)
        v_t = vt_ref[0, c0 + t, rows_h, :] if t < win_chunks else vct_ref[0, 0, rows_h, :]
        s = s_ref[...]
        m_new = jnp.max(s, axis=0, keepdims=True)
        if state is not None:
            m_old, acc_old = state
            m_new = jnp.maximum(m_old, m_new)
        p = jnp.exp2((s - m_new).astype(BF16))
        acc = jnp.dot(_with_ones(v_t), p, preferred_element_type=F32)
        if state is not None:
            acc = acc + acc_old * jnp.exp2(m_old - m_new)
        return m_new, acc

    n_buf = NA_AHEAD + 1
    for u in range(NA_AHEAD):
        scores(*units[u], s_ref.at[u % n_buf])
    state = [None] * NA_HEADS
    for u, (h, t) in enumerate(units):
        if u + NA_AHEAD < len(units):
            scores(*units[u + NA_AHEAD], s_ref.at[(u + NA_AHEAD) % n_buf])
        state[h] = softmax_pv(h, t, s_ref.at[u % n_buf], state[h])
        if t == win_chunks:
            acc = state[h][1]
            ot_ref[h * HEAD_DIM:(h + 1) * HEAD_DIM, :] = acc[:HEAD_DIM] * (1.0 / acc[HEAD_DIM:HEAD_DIM + 1])
    o_ref[...] = ot_ref[...].T.astype(BF16)


def _na_attention(q, k, vt, kc, vct, bias, *, n_batch, seq, ctx_len):
    rows = seq // GRID_W
    n_groups = rows // NA_QROWS
    nq = NA_QROWS * GRID_W
    variant = lambda g: jnp.where(g == 0, 0, jnp.where(g == n_groups - 1, 2, 1))
    return pl.pallas_call(
        functools.partial(_na_kernel, n_chunks=seq // KEY_CHUNK),
        grid=(n_batch, n_groups),
        in_specs=[
            pl.BlockSpec((nq, NA_W), lambda b, g: (b * n_groups + g, 0)),
            pl.BlockSpec((seq, NA_W), lambda b, g: (b, 0)),
            pl.BlockSpec((1,) + vt.shape[1:], lambda b, g: (b, 0, 0, 0)),
            pl.BlockSpec((ctx_len, NA_W), lambda b, g: (b, 0)),
            pl.BlockSpec((1,) + vct.shape[1:], lambda b, g: (b, 0, 0, 0)),
            pl.BlockSpec((1,) + bias.shape[1:], lambda b, g: (variant(g), 0, 0, 0)),
        ],
        out_specs=pl.BlockSpec((nq, NA_W), lambda b, g: (b * n_groups + g, 0)),
        out_shape=jax.ShapeDtypeStruct(q.shape, BF16),
        scratch_shapes=[
            pltpu.VMEM((NA_AHEAD + 1, KEY_CHUNK, nq), F32),
            pltpu.VMEM((NA_W, nq), F32),
        ],
        compiler_params=_params(2),
        name="na_attention",
    )(q, k, vt, kc, vct, bias)


def _gqa_kernel(q_ref, k_ref, kc_ref, vt_ref, vct_ref, o_ref, qt_ref, s_ref, m_ref, acc_ref, *, n_chunks):
    tq = q_ref.shape[0]
    q_t = q_ref[...].astype(F32).T
    for hk in range(GQA_KV_HEADS):
        for gi in range(GQA_GROUP):
            hq = hk * GQA_GROUP + gi
            qt_ref[hk, :, gi * tq:(gi + 1) * tq] = q_t[hq * HEAD_DIM:(hq + 1) * HEAD_DIM, :].astype(BF16)
    m_ref[...] = jnp.full(m_ref.shape, NEG, F32)
    acc_ref[...] = jnp.zeros(acc_ref.shape, F32)

    def scores(k_blk, s_buf):
        for hk in range(GQA_KV_HEADS):
            kch = k_blk[:, hk * HEAD_DIM:(hk + 1) * HEAD_DIM]
            s_buf[hk] = jnp.dot(kch, qt_ref[hk], preferred_element_type=F32)

    def softmax_pv(v_t, s_ref):
        for hk in range(GQA_KV_HEADS):
            s = s_ref[hk]
            m_old = m_ref[hk]
            m_new = jnp.maximum(m_old, jnp.max(s, axis=0, keepdims=True))
            p = jnp.exp2(s - m_new).astype(BF16)
            pv = jnp.dot(_with_ones(v_t[hk * HEAD_DIM:(hk + 1) * HEAD_DIM]), p, preferred_element_type=F32)
            acc_ref[hk] = acc_ref[hk] * jnp.exp2(m_old - m_new) + pv
            m_ref[hk] = m_new

    def k_rows(c):
        if isinstance(c, int) and c == n_chunks:
            return kc_ref[...]
        return k_ref[pl.ds(pl.multiple_of(c * KEY_CHUNK, KEY_CHUNK), KEY_CHUNK), :]

    def v_cols(c):
        return vct_ref[0, 0] if isinstance(c, int) and c == n_chunks else vt_ref[0, c]

    n_buf = GQA_AHEAD + 1
    for c in range(GQA_AHEAD):
        scores(k_rows(c), s_ref.at[c % n_buf])

    def steps(c0, n):
        for j in range(n):
            c = c0 + j
            if not (isinstance(c, int) and c + GQA_AHEAD > n_chunks):
                scores(k_rows(c + GQA_AHEAD), s_ref.at[(j + GQA_AHEAD) % n_buf])
            softmax_pv(v_cols(c), s_ref.at[j % n_buf])

    trips = (n_chunks - GQA_AHEAD) // GQA_UNROLL

    def body(i, carry):
        steps(i * GQA_UNROLL, GQA_UNROLL)
        return carry

    lax.fori_loop(0, trips, body, 0)
    steps(trips * GQA_UNROLL, n_chunks + 1 - trips * GQA_UNROLL)

    heads_t = []
    for hk in range(GQA_KV_HEADS):
        acc = acc_ref[hk]
        out_t = acc[:HEAD_DIM] * (1.0 / acc[HEAD_DIM:HEAD_DIM + 1])
        heads_t += [out_t[:, gi * tq:(gi + 1) * tq] for gi in range(GQA_GROUP)]
    o_ref[...] = jnp.concatenate(heads_t, axis=0).T.astype(BF16)


def _gqa_attention(q, k, kc, vt, vct, *, n_batch, sq, ctx_len, tq):
    tpq = sq // tq
    m = GQA_GROUP * tq
    n_chunks = sq // KEY_CHUNK
    assert n_chunks >= GQA_AHEAD and GQA_UNROLL % (GQA_AHEAD + 1) == 0 and ctx_len == KEY_CHUNK
    return pl.pallas_call(
        functools.partial(_gqa_kernel, n_chunks=n_chunks),
        grid=(n_batch, tpq),
        in_specs=[
            pl.BlockSpec((tq, GQA_Q_W), lambda b, i: (b * tpq + i, 0)),
            pl.BlockSpec((sq, GQA_KV_W), lambda b, i: (b, 0)),
            pl.BlockSpec((ctx_len, GQA_KV_W), lambda b, i: (b, 0)),
            pl.BlockSpec((1,) + vt.shape[1:], lambda b, i: (b, 0, 0, 0)),
            pl.BlockSpec((1,) + vct.shape[1:], lambda b, i: (b, 0, 0, 0)),
        ],
        out_specs=pl.BlockSpec((tq, GQA_Q_W), lambda b, i: (b * tpq + i, 0)),
        out_shape=jax.ShapeDtypeStruct(q.shape, BF16),
        scratch_shapes=[
            pltpu.VMEM((GQA_KV_HEADS, HEAD_DIM, m), BF16),
            pltpu.VMEM((GQA_AHEAD + 1, GQA_KV_HEADS, KEY_CHUNK, m), F32),
            pltpu.VMEM((GQA_KV_HEADS, 1, m), F32),
            pltpu.VMEM((GQA_KV_HEADS, HEAD_DIM + ONES_ROWS, m), F32),
        ],
        compiler_params=_params(2),
        name="gqa_attention",
    )(q, k, kc, vt, vct)


def _ctx_attn_kernel(q_ref, k_ref, vt_ref, o_ref, ot_ref, *, n_kv, group):
    tq = q_ref.shape[0]
    q_t = q_ref[...].astype(F32).T.astype(BF16)
    for hk in range(n_kv):
        rows_k = slice(hk * HEAD_DIM, (hk + 1) * HEAD_DIM)
        qs = jnp.concatenate([q_t[(hk * group + gi) * HEAD_DIM:(hk * group + gi + 1) * HEAD_DIM]
                              for gi in range(group)], axis=1)
        s = jnp.dot(k_ref[:, rows_k], qs, preferred_element_type=F32)
        p = jnp.exp(s - jnp.max(s, axis=0, keepdims=True)).astype(BF16)
        acc = jnp.dot(_with_ones(vt_ref[0, 0, rows_k, :]), p, preferred_element_type=F32)
        out_t = acc[:HEAD_DIM] * (1.0 / acc[HEAD_DIM:HEAD_DIM + 1])
        for gi in range(group):
            hq = hk * group + gi
            ot_ref[hq * HEAD_DIM:(hq + 1) * HEAD_DIM, :] = out_t[:, gi * tq:(gi + 1) * tq]
    o_ref[...] = ot_ref[...].T.astype(BF16)


def _ctx_attention(q, k, vt, *, n_batch, ctx_len, n_kv, group):
    return pl.pallas_call(
        functools.partial(_ctx_attn_kernel, n_kv=n_kv, group=group),
        grid=(n_batch,),
        in_specs=[
            pl.BlockSpec((ctx_len, q.shape[1]), lambda b: (b, 0)),
            pl.BlockSpec((ctx_len, k.shape[1]), lambda b: (b, 0)),
            pl.BlockSpec((1,) + vt.shape[1:], lambda b: (b, 0, 0, 0)),
        ],
        out_specs=pl.BlockSpec((ctx_len, q.shape[1]), lambda b: (b, 0)),
        out_shape=jax.ShapeDtypeStruct(q.shape, BF16),
        scratch_shapes=[pltpu.VMEM((q.shape[1], ctx_len), F32)],
        compiler_params=_params(1),
        name=f"ctx_attention_g{group}",
    )(q, k, vt)


def _merge_kernel(x_ref, sh_ref, sc_ref, gate_ref, g_ref, wg_ref, uprev_ref, u_ref, unext_ref, cw_ref, cb_ref,
                  lg_ref, lb_ref, b_ref, c_ref, wa_ref, wb_ref, wc_ref, wo_ref, o_ref, win_ref, shift_ref, bra_ref,
                  *, tiles_per_seq):
    pos = pl.program_id(0) % tiles_per_seq
    x = x_ref[...]
    d = x.shape[1]
    h = _norm_modulate(x, g_ref[...], sh_ref[0], sc_ref[0]).astype(BF16)

    def gated(j, br, w_ref):
        gate = jax.nn.sigmoid(jnp.dot(h, wg_ref[:, j * d:(j + 1) * d], preferred_element_type=F32))
        return gate * jnp.dot(br, w_ref[...], preferred_element_type=F32)

    y = gated(1, b_ref[...], wb_ref) + gated(2, c_ref[...], wc_ref)
    gate_a = jax.nn.sigmoid(jnp.dot(h, wg_ref[:, 0:d], preferred_element_type=F32))
    _conv_ln_swish(uprev_ref, u_ref, unext_ref, cw_ref, cb_ref, lg_ref, lb_ref, win_ref, shift_ref, bra_ref,
                   pos == 0, pos == tiles_per_seq - 1)
    y = y + gate_a * jnp.dot(bra_ref[...], wa_ref[...], preferred_element_type=F32)
    o_ref[...] = x + gate_ref[0] * jnp.dot(y.astype(BF16), wo_ref[...], preferred_element_type=F32)


def _merge(x2, mod3, norm_g, w_gates, u, conv_w, conv_b, ln_g, ln_b, br_b, br_c, wa, wb, wc, wo, *, n_batch, seq,
           tm):
    nt, d = x2.shape
    tps = seq // tm
    tok = lambda i: (i, 0)
    mrow = lambda j: (lambda i: ((i // tps) * 6 + j, 0, 0))
    return pl.pallas_call(
        functools.partial(_merge_kernel, tiles_per_seq=tps),
        grid=(nt // tm,),
        in_specs=[
            pl.BlockSpec((tm, d), tok),
            pl.BlockSpec((1, 1, d), mrow(0)),
            pl.BlockSpec((1, 1, d), mrow(1)),
            pl.BlockSpec((1, 1, d), mrow(2)),
            _const_spec((1, d)),
            _const_spec(w_gates.shape),
            *_conv_halo_specs(nt, tm),
            _const_spec((CONV_K, CONV_W)),
            _const_spec((1, CONV_W)),
            _const_spec((1, CONV_W)),
            _const_spec((1, CONV_W)),
            pl.BlockSpec((tm, NA_W), tok),
            pl.BlockSpec((tm, GQA_Q_W), tok),
            _const_spec(wa.shape),
            _const_spec(wb.shape),
            _const_spec(wc.shape),
            _const_spec(wo.shape),
        ],
        out_specs=pl.BlockSpec((tm, d), tok),
        out_shape=jax.ShapeDtypeStruct((nt, d), F32),
        scratch_shapes=[pltpu.VMEM((tm + 2 * CONV_HALO, CONV_W), F32),
                        pltpu.VMEM((SUBLANES, tm + 2 * CONV_HALO, CONV_W), F32),
                        pltpu.VMEM((tm, CONV_W), BF16)],
        compiler_params=_params(1),
        name="merge",
    )(x2, mod3, mod3, mod3, norm_g, w_gates, u, u, u, conv_w, conv_b, ln_g, ln_b, br_b, br_c, wa, wb, wc, wo)


def _ffn_kernel(x_ref, sh_ref, sc_ref, gate_ref, g_ref, wi_ref, wo_ref, fg_ref, o_ref, *, n_split, final_norm):
    x = x_ref[...]
    hidden = wo_ref.shape[0]
    h = _norm_modulate(x, g_ref[...], sh_ref[0], sc_ref[0]).astype(BF16)
    step = hidden // n_split
    y = None
    for j in range(n_split):
        gt = jnp.dot(h, wi_ref[:, j * step:(j + 1) * step], preferred_element_type=F32)
        up = jnp.dot(h, wi_ref[:, hidden + j * step:hidden + (j + 1) * step], preferred_element_type=F32)
        act = (gt * jax.nn.sigmoid(gt) * up).astype(BF16)
        t = jnp.dot(act, wo_ref[j * step:(j + 1) * step, :], preferred_element_type=F32)
        y = t if y is None else y + t
    out = x + gate_ref[0] * y
    if final_norm:
        ms = jnp.mean(out * out, axis=-1, keepdims=True)
        out = out * lax.rsqrt(ms + EPS) * fg_ref[...]
    o_ref[...] = out


def _ffn(x2, mod3, norm_g, w_in, w_out, final_g, *, n_batch, seq, tm, final_norm):
    nt, d = x2.shape
    tps = seq // tm
    hidden = w_out.shape[0]
    assert hidden % KEY_CHUNK == 0
    tok = lambda i: (i, 0)
    mrow = lambda j: (lambda i: ((i // tps) * 6 + j, 0, 0))
    return pl.pallas_call(
        functools.partial(_ffn_kernel, n_split=hidden // KEY_CHUNK, final_norm=final_norm),
        grid=(nt // tm,),
        in_specs=[
            pl.BlockSpec((tm, d), tok),
            pl.BlockSpec((1, 1, d), mrow(3)),
            pl.BlockSpec((1, 1, d), mrow(4)),
            pl.BlockSpec((1, 1, d), mrow(5)),
            _const_spec((1, d)),
            _const_spec(w_in.shape),
            _const_spec(w_out.shape),
            _const_spec((1, d)),
        ],
        out_specs=pl.BlockSpec((tm, d), tok),
        out_shape=jax.ShapeDtypeStruct((nt, d), F32),
        compiler_params=_params(1),
        name="ffn_final" if final_norm else "ffn",
    )(x2, mod3, mod3, mod3, norm_g, w_in, w_out, final_g)


def _rope_tables(seq):
    t = jnp.arange(seq)
    pos_row = (t // GRID_W).astype(F32)
    pos_col = (t % GRID_W).astype(F32)
    half = HEAD_DIM // 2
    freqs = jnp.power(ROPE_THETA, -jnp.arange(0, half, 2, dtype=F32) / half)

    def part(pos):
        ang = pos[:, None] * freqs
        c, s = jnp.cos(ang), jnp.sin(ang)
        return jnp.concatenate([c, c], axis=-1), jnp.concatenate([-s, s], axis=-1)

    cr, sr = part(pos_row)
    cc, sc = part(pos_col)
    cos = jnp.concatenate([cr, cc], axis=-1)
    sin = jnp.concatenate([sr, sc], axis=-1)
    return jnp.tile(cos, (1, LANES // HEAD_DIM)), jnp.tile(sin, (1, LANES // HEAD_DIM))


def kernel(x, c, ctx, c_ctx, w_mod, b_mod, norm1_g, norm2_g, w_in, conv_w, conv_b, conv_ln_g, conv_ln_b,
           w_conv_out, na_rpb, w_na_out, q_norm_g, k_norm_g, w_gqa_out, w_out, w_ffn_in, w_ffn_out, final_g):
    n_batch, seq, d = x.shape
    ctx_len = ctx.shape[1]
    depth = w_mod.shape[0]
    rows = seq // GRID_W
    assert seq % (NA_QROWS * GRID_W) == 0 and rows >= NA_WROWS and ctx_len == KEY_CHUNK
    tm = TOKEN_TILE
    tmc = ctx_len

    n_rows = -(-(n_batch + 1) // 8) * 8
    crows = jnp.zeros((n_rows, d), F32).at[:n_batch].set(c).at[n_batch].set(c_ctx)
    mod = _modulation(crows, w_mod, b_mod)

    cos_t, sin_t = _rope_tables(seq)
    bd = jnp.asarray(np.kron(np.eye(KEY_CHUNK // HEAD_DIM), np.full((HEAD_DIM, HEAD_DIM), 1.0 / HEAD_DIM)), BF16)
    tile2 = lambda g: jnp.tile(g, LANES // HEAD_DIM).reshape(1, LANES)
    row = lambda g: g.reshape(1, -1)

    xl = x.reshape(n_batch * seq, d)
    xc = ctx.reshape(n_batch * ctx_len, d)
    for l in range(depth):
        last = l == depth - 1
        wl = w_in[l]
        w_proj = jnp.concatenate([wl[:, _IN_GLU:_IN_NAV], wl[:, _IN_GQ:_IN_GV]], axis=1).astype(BF16)
        w_vt = jnp.concatenate([wl[:, _IN_NAV:_IN_GQ], wl[:, _IN_GV:_IN_GATES]], axis=1).T.astype(BF16)
        w_gates = wl[:, _IN_GATES:].astype(BF16)
        mod_l = mod[l, :n_batch].reshape(n_batch * 6, 1, d)
        mod_c = jnp.broadcast_to(mod[l, n_batch], (n_batch, 6 * d)).reshape(n_batch * 6, 1, d)
        qg2, kg2 = tile2(q_norm_g[l]), tile2(k_norm_g[l])
        g1, g2 = row(norm1_g[l]), row(norm2_g[l])
        wa, wb, wc = w_conv_out[l].astype(BF16), w_na_out[l].astype(BF16), w_gqa_out[l].astype(BF16)
        wo = w_out[l].astype(BF16)
        wfi, wfo = w_ffn_in[l].astype(BF16), w_ffn_out[l].astype(BF16)
        conv_args = (conv_w[l], row(conv_b[l]), row(conv_ln_g[l]), row(conv_ln_b[l]))

        u, naq, nak, gq, gk, navt, gvt = _inproj(xl, mod_l, g1, w_proj, w_vt, qg2, kg2, bd, cos_t, sin_t,
                                                 n_batch=n_batch, seq=seq, tm=tm, rope=True)
        uc, naqc, nakc, gqc, gkc, navct, gvct = _inproj(xc, mod_c, g1, w_proj, w_vt, qg2, kg2, bd,
                                                        cos_t[:ctx_len], sin_t[:ctx_len],
                                                        n_batch=n_batch, seq=ctx_len, tm=tmc, rope=False)

        bias = _na_bias_tables(na_rpb[l], rows)
        br_b = _na_attention(naq, nak, navt, nakc, navct, bias, n_batch=n_batch, seq=seq, ctx_len=ctx_len)
        br_c = _gqa_attention(gq, gk, gkc, gvt, gvct, n_batch=n_batch, sq=seq, ctx_len=ctx_len, tq=GQA_Q_TILE)
        xl = _merge(xl, mod_l, g1, w_gates, u, *conv_args, br_b, br_c, wa, wb, wc, wo,
                    n_batch=n_batch, seq=seq, tm=tm)
        xl = _ffn(xl, mod_l, g2, wfi, wfo, row(final_g), n_batch=n_batch, seq=seq, tm=tm, final_norm=last)

        if not last:
            cbr_b = _ctx_attention(naqc, nakc, navct, n_batch=n_batch, ctx_len=ctx_len, n_kv=NA_HEADS, group=1)
            cbr_c = _ctx_attention(gqc, gkc, gvct, n_batch=n_batch, ctx_len=ctx_len, n_kv=GQA_KV_HEADS,
                                   group=GQA_GROUP)
            xc = _merge(xc, mod_c, g1, w_gates, uc, *conv_args, cbr_b, cbr_c, wa, wb, wc, wo,
                        n_batch=n_batch, seq=ctx_len, tm=tmc)
            xc = _ffn(xc, mod_c, g2, wfi, wfo, row(final_g), n_batch=n_batch, seq=ctx_len, tm=tmc,
                      final_norm=False)
    return xl.reshape(n_batch, seq, d)
```

```python
import functools

import numpy as np
import jax
import jax.numpy as jnp
from jax import lax
from jax.experimental import pallas as pl
from jax.experimental.pallas import tpu as pltpu

F32 = jnp.float32
BF16 = jnp.bfloat16

GRID_W = 64
HEAD_DIM = 64
CONV_W = 512
CONV_K = 31
CONV_HALO = 16
CONV_ROWS = 64
NA_HEADS = 8
NA_W = NA_HEADS * HEAD_DIM
NA_KH = 8
NA_KW = 16
NA_QROWS = 4
NA_WROWS = 12
NA_AHEAD = 8
GQA_Q_HEADS = 8
GQA_KV_HEADS = 2
GQA_GROUP = GQA_Q_HEADS // GQA_KV_HEADS
GQA_Q_W = GQA_Q_HEADS * HEAD_DIM
GQA_KV_W = GQA_KV_HEADS * HEAD_DIM
GQA_AHEAD = 2
GQA_UNROLL = 6
GQA_Q_TILE = 128
TOKEN_TILE = 512
ROPE_THETA = 10000.0
EPS = 1e-6
NEG = -1e30
SCALE = HEAD_DIM ** -0.5
LOG2E = 1.4426950408889634
LANES = 128
SUBLANES = 8
KEY_CHUNK = 256
ONES_ROWS = 16
VMEM_LIMIT = 56 * 1024 * 1024

_IN_GLU = 0
_IN_NAQ = _IN_GLU + 2 * CONV_W
_IN_NAK = _IN_NAQ + NA_W
_IN_NAV = _IN_NAK + NA_W
_IN_GQ = _IN_NAV + NA_W
_IN_GK = _IN_GQ + GQA_Q_W
_IN_GV = _IN_GK + GQA_KV_W
_IN_GATES = _IN_GV + GQA_KV_W
_P_GLU = 0
_P_NAQ = _P_GLU + 2 * CONV_W
_P_NAK = _P_NAQ + NA_W
_P_GQ = _P_NAK + NA_W
_P_GK = _P_GQ + GQA_Q_W
_P_END = _P_GK + GQA_KV_W


def _params(n_axes):
    return pltpu.CompilerParams(dimension_semantics=("arbitrary",) * n_axes, vmem_limit_bytes=VMEM_LIMIT)


def _const_spec(shape):
    zeros = (0,) * len(shape)
    return pl.BlockSpec(shape, lambda *_: zeros, pipeline_mode=pl.Buffered(1))


def _norm_modulate(x, g, shift, scale):
    ms = jnp.mean(x * x, axis=-1, keepdims=True)
    h = x * lax.rsqrt(ms + EPS) * g
    return h * (1.0 + scale) + shift


def _with_ones(v_t):
    return jnp.concatenate([v_t, jnp.ones((ONES_ROWS, v_t.shape[1]), v_t.dtype)], axis=0)


def _mod_kernel(c_ref, w_ref, b_ref, o_ref):
    c = c_ref[...]
    s = c * jax.nn.sigmoid(c)
    o_ref[0] = jnp.dot(s, w_ref[0], preferred_element_type=F32, precision=lax.Precision.HIGHEST) + b_ref[0]


def _modulation(crows, w_mod, b_mod):
    n_layers, d, d6 = w_mod.shape
    r = crows.shape[0]
    nb = d6 // d
    return pl.pallas_call(
        _mod_kernel,
        grid=(n_layers, nb),
        in_specs=[
            pl.BlockSpec((r, d), lambda l, j: (0, 0)),
            pl.BlockSpec((1, d, d), lambda l, j: (l, 0, j)),
            pl.BlockSpec((1, 1, d), lambda l, j: (l, 0, j)),
        ],
        out_specs=pl.BlockSpec((1, r, d), lambda l, j: (l, 0, j)),
        out_shape=jax.ShapeDtypeStruct((n_layers, r, d6), F32),
        compiler_params=_params(2),
        name="modulation",
    )(crows, w_mod, b_mod.reshape(n_layers, 1, d6))


def _rope(t, cos, sin_signed, first_half):
    fwd = pltpu.roll(t, LANES - 16, axis=1)
    bwd = pltpu.roll(t, 16, axis=1)
    return t * cos + jnp.where(first_half, fwd, bwd) * sin_signed


def _inproj_kernel(x_ref, sh_ref, sc_ref, g_ref, w_ref, wvt_ref, qg_ref, kg_ref, bd_ref, cos_ref, sin_ref,
                   *out_refs, rope, kv_only):
    if kv_only:
        nak_ref, gk_ref, navt_ref, gvt_ref = out_refs
    else:
        u_ref, naq_ref, nak_ref, gq_ref, gk_ref, navt_ref, gvt_ref = out_refs
    h = _norm_modulate(x_ref[...], g_ref[...], sh_ref[0], sc_ref[0]).astype(BF16)

    def proj(lo, hi):
        return jnp.dot(h, w_ref[:, lo:hi], preferred_element_type=F32)

    q_scale = SCALE * LOG2E if rope else SCALE
    if rope:
        cos = cos_ref[...]
        sin = sin_ref[...]
        lane = lax.broadcasted_iota(jnp.int32, cos.shape, 1)
        first_half = (lane % 32) < 16

    def head_ms(t):
        w = t.shape[1]
        return jnp.dot((t * t).astype(BF16), bd_ref[:w, :w], preferred_element_type=F32)

    def finish(t, ms, gain):
        t = t * lax.rsqrt(ms + EPS) * gain
        if rope:
            t = _rope(t, cos, sin, first_half)
        return t

    if not kv_only:
        q = proj(_P_GQ, _P_GK)
        wide = bd_ref.shape[0]
        for c in range(GQA_Q_W // wide):
            qc = q[:, c * wide:(c + 1) * wide]
            ms = head_ms(qc)
            for j in range(wide // LANES):
                sl = slice(j * LANES, (j + 1) * LANES)
                out_sl = slice(c * wide + j * LANES, c * wide + (j + 1) * LANES)
                gq_ref[:, out_sl] = (finish(qc[:, sl], ms[:, sl], qg_ref[...]) * q_scale).astype(BF16)
    k = proj(_P_GK, _P_END)
    gk_ref[...] = finish(k, head_ms(k), kg_ref[...]).astype(BF16)

    if not kv_only:
        a = proj(_P_GLU, _P_NAQ)
        u_ref[...] = (a[:, :CONV_W] * jax.nn.sigmoid(a[:, CONV_W:])).astype(BF16)
        naq_ref[...] = (proj(_P_NAQ, _P_NAK) * q_scale).astype(BF16)
    nak_ref[...] = proj(_P_NAK, _P_GQ).astype(BF16)

    v_t = lax.dot_general(wvt_ref[...], h, (((1,), (1,)), ((), ())), preferred_element_type=F32).astype(BF16)
    for c in range(navt_ref.shape[1]):
        navt_ref[0, c] = v_t[:NA_W, c * KEY_CHUNK:(c + 1) * KEY_CHUNK]
        gvt_ref[0, c] = v_t[NA_W:, c * KEY_CHUNK:(c + 1) * KEY_CHUNK]


def _inproj(x2, mod3, norm_g, w, w_vt, qg2, kg2, bd, cos_t, sin_t, *, n_batch, seq, tm, rope, kv_only=False):
    nt, d = x2.shape
    tps = seq // tm
    cpt = tm // KEY_CHUNK
    tok = lambda s, b: (b * tps + s, 0)
    outs = [NA_W, GQA_KV_W] if kv_only else [CONV_W, NA_W, NA_W, GQA_Q_W, GQA_KV_W]
    vt_shape = lambda w_: jax.ShapeDtypeStruct((n_batch, seq // KEY_CHUNK, w_, KEY_CHUNK), BF16)
    vt_spec = lambda w_: pl.BlockSpec((1, cpt, w_, KEY_CHUNK), lambda s, b: (b, s, 0, 0))
    return pl.pallas_call(
        functools.partial(_inproj_kernel, rope=rope, kv_only=kv_only),
        grid=(tps, n_batch),
        in_specs=[
            pl.BlockSpec((tm, d), tok),
            pl.BlockSpec((1, 1, d), lambda s, b: (b * 6 + 0, 0, 0)),
            pl.BlockSpec((1, 1, d), lambda s, b: (b * 6 + 1, 0, 0)),
            _const_spec((1, d)),
            _const_spec(w.shape),
            _const_spec(w_vt.shape),
            _const_spec((1, LANES)),
            _const_spec((1, LANES)),
            _const_spec(bd.shape),
            pl.BlockSpec((tm, LANES), lambda s, b: (s, 0)),
            pl.BlockSpec((tm, LANES), lambda s, b: (s, 0)),
        ],
        out_specs=[pl.BlockSpec((tm, n), tok) for n in outs] + [vt_spec(NA_W), vt_spec(GQA_KV_W)],
        out_shape=[jax.ShapeDtypeStruct((nt, n), BF16) for n in outs] + [vt_shape(NA_W), vt_shape(GQA_KV_W)],
        compiler_params=_params(2),
        name="inproj_rope" if rope else ("inproj_ctx_kv" if kv_only else "inproj_ctx"),
    )(x2, mod3, mod3, norm_g, w, w_vt, qg2, kg2, bd, cos_t, sin_t)


def _conv_ln_swish(prev_ref, cur_ref, next_ref, w_ref, b_ref, lg_ref, lb_ref, win_ref, shift_ref, out_ref, first,
                   last):
    tl = cur_ref.shape[0]
    win_ref[0:CONV_HALO, :] = jnp.where(first, 0.0, prev_ref[...].astype(F32))
    win_ref[CONV_HALO:CONV_HALO + tl, :] = cur_ref[...].astype(F32)
    win_ref[CONV_HALO + tl:, :] = jnp.where(last, 0.0, next_ref[...].astype(F32))
    base = CONV_HALO - CONV_K // 2
    for r in range(SUBLANES):
        n_shift = tl + SUBLANES * (len(range(r, CONV_K, SUBLANES)) - 1)
        shift_ref[r, 0:n_shift, :] = win_ref[base + r:base + r + n_shift, :]
    for rb in range(0, tl, CONV_ROWS):
        acc = jnp.zeros((CONV_ROWS, CONV_W), F32)
        for k in range(CONV_K):
            off = rb + SUBLANES * (k // SUBLANES)
            acc = acc + shift_ref[k % SUBLANES, off:off + CONV_ROWS, :] * w_ref[k:k + 1, :]
        y = acc + b_ref[...]
        mu = jnp.mean(y, axis=-1, keepdims=True)
        yc = y - mu
        var = jnp.mean(yc * yc, axis=-1, keepdims=True)
        z = yc * lax.rsqrt(var + EPS) * lg_ref[...] + lb_ref[...]
        out_ref[rb:rb + CONV_ROWS, :] = (z * jax.nn.sigmoid(z)).astype(out_ref.dtype)


def _conv_halo_specs(n_tokens, tl):
    hb = tl // CONV_HALO
    last_hb = n_tokens // CONV_HALO - 1
    return [
        pl.BlockSpec((CONV_HALO, CONV_W), lambda i: (jnp.maximum(i * hb - 1, 0), 0)),
        pl.BlockSpec((tl, CONV_W), lambda i: (i, 0)),
        pl.BlockSpec((CONV_HALO, CONV_W), lambda i: (jnp.minimum((i + 1) * hb, last_hb), 0)),
    ]


def _na_bias_tables(rpb, rows):
    cols = np.arange(GRID_W)
    win0 = np.clip(cols - NA_KW // 2, 0, GRID_W - NA_KW)
    col_ok = (cols[:, None] >= win0[None, :]) & (cols[:, None] < win0[None, :] + NA_KW)
    pad = GRID_W - NA_KW
    rp = jnp.pad(rpb * LOG2E, ((0, 0), (0, 0), (pad, pad)))
    tiles = jnp.stack([rp[:, :, GRID_W - 1 - qc:2 * GRID_W - 1 - qc] for qc in range(GRID_W)], axis=-1)
    tiles = jnp.where(jnp.asarray(col_ok)[None, None], tiles, NEG * LOG2E)
    masked = jnp.full(tiles.shape[:1] + tiles.shape[2:], NEG * LOG2E, F32)
    n_groups = rows // NA_QROWS
    variants = []
    for g in (0, min(1, n_groups - 1), n_groups - 1):
        w = int(np.clip(NA_QROWS * g - NA_KH // 2, 0, rows - NA_WROWS))
        blocks = []
        for kl in range(NA_WROWS):
            kr = w + kl
            row = []
            for ql in range(NA_QROWS):
                r = NA_QROWS * g + ql
                r0 = int(np.clip(r - NA_KH // 2, 0, rows - NA_KH))
                row.append(tiles[:, kr - r + NA_KH - 1] if r0 <= kr < r0 + NA_KH else masked)
            blocks.append(jnp.concatenate(row, axis=-1))
        variants.append(jnp.concatenate(blocks, axis=1))
    return jnp.stack(variants)


def _na_kernel(q_ref, k_ref, vt_ref, kc_ref, vct_ref, bias_ref, o_ref, s_ref, ot_ref, *, n_chunks):
    g = pl.program_id(1)
    nq = NA_QROWS * GRID_W
    win_chunks = NA_WROWS * GRID_W // KEY_CHUNK
    c0 = jnp.clip(g - 1, 0, n_chunks - win_chunks)
    start = pl.multiple_of(c0 * KEY_CHUNK, KEY_CHUNK)
    q_t = q_ref[...].astype(F32).T.astype(BF16)
    zeros = jnp.zeros((HEAD_DIM, nq), BF16)
    units = [(h, t) for t in range(win_chunks + 1) for h in range(NA_HEADS)]

    def scores(h, t, s_ref):
        hp, j = divmod(h, LANES // HEAD_DIM)
        qh = q_t[h * HEAD_DIM:(h + 1) * HEAD_DIM]
        qz = jnp.concatenate([qh, zeros] if j == 0 else [zeros, qh], axis=0)
        pair = slice(hp * LANES, (hp + 1) * LANES)
        if t < win_chunks:
            rows_k = pl.ds(pl.multiple_of(start + t * KEY_CHUNK, KEY_CHUNK), KEY_CHUNK)
            s_ref[...] = (jnp.dot(k_ref[rows_k, pair], qz, preferred_element_type=F32)
                          + bias_ref[0, h, t * KEY_CHUNK:(t + 1) * KEY_CHUNK, :])
        else:
            s_ref[...] = jnp.dot(kc_ref[:, pair], qz, preferred_element_type=F32)

    def softmax_pv(h, t, s_ref, state):
        rows_h = slice(h * HEAD_DIM, (h + 1) * HEAD_DIM)
        v_t = vt_ref[0, c0 + t, rows_h, :] if t < win_chunks else vct_ref[0, 0, rows_h, :]
        s = s_ref[...]
        m_new = jnp.max(s, axis=0, keepdims=True)
        if state is not None:
            m_old, acc_old = state
            m_new = jnp.maximum(m_old, m_new)
        p = jnp.exp2((s - m_new).astype(BF16))
        acc = jnp.dot(_with_ones(v_t), p, preferred_element_type=F32)
        if state is not None:
            acc = acc + acc_old * jnp.exp2(m_old - m_new)
        return m_new, acc

    n_buf = NA_AHEAD + 1
    for u in range(NA_AHEAD):
        scores(*units[u], s_ref.at[u % n_buf])
    state = [None] * NA_HEADS
    for u, (h, t) in enumerate(units):
        if u + NA_AHEAD < len(units):
            scores(*units[u + NA_AHEAD], s_ref.at[(u + NA_AHEAD) % n_buf])
        state[h] = softmax_pv(h, t, s_ref.at[u % n_buf], state[h])
        if t == win_chunks:
            acc = state[h][1]
            ot_ref[h * HEAD_DIM:(h + 1) * HEAD_DIM, :] = acc[:HEAD_DIM] * (1.0 / acc[HEAD_DIM:HEAD_DIM + 1])
    o_ref[...] = ot_ref[...].T.astype(BF16)


def _na_attention(q, k, vt, kc, vct, bias, *, n_batch, seq, ctx_len):
    rows = seq // GRID_W
    n_groups = rows // NA_QROWS
    nq = NA_QROWS * GRID_W
    variant = lambda g: jnp.where(g == 0, 0, jnp.where(g == n_groups - 1, 2, 1))
    return pl.pallas_call(
        functools.partial(_na_kernel, n_chunks=seq // KEY_CHUNK),
        grid=(n_batch, n_groups),
        in_specs=[
            pl.BlockSpec((nq, NA_W), lambda b, g: (b * n_groups + g, 0)),
            pl.BlockSpec((seq, NA_W), lambda b, g: (b, 0)),
            pl.BlockSpec((1,) + vt.shape[1:], lambda b, g: (b, 0, 0, 0)),
            pl.BlockSpec((ctx_len, NA_W), lambda b, g: (b, 0)),
            pl.BlockSpec((1,) + vct.shape[1:], lambda b, g: (b, 0, 0, 0)),
            pl.BlockSpec((1,) + bias.shape[1:], lambda b, g: (variant(g), 0, 0, 0)),
        ],
        out_specs=pl.BlockSpec((nq, NA_W), lambda b, g: (b * n_groups + g, 0)),
        out_shape=jax.ShapeDtypeStruct(q.shape, BF16),
        scratch_shapes=[
            pltpu.VMEM((NA_AHEAD + 1, KEY_CHUNK, nq), F32),
            pltpu.VMEM((NA_W, nq), F32),
        ],
        compiler_params=_params(2),
        name="na_attention",
    )(q, k, vt, kc, vct, bias)


def _gqa_kernel(q_ref, k_ref, kc_ref, vt_ref, vct_ref, o_ref, qt_ref, s_ref, m_ref, acc_ref, *, n_chunks):
    tq = q_ref.shape[0]
    q_t = q_ref[...].astype(F32).T
    for hk in range(GQA_KV_HEADS):
        for gi in range(GQA_GROUP):
            hq = hk * GQA_GROUP + gi
            qt_ref[hk, :, gi * tq:(gi + 1) * tq] = q_t[hq * HEAD_DIM:(hq + 1) * HEAD_DIM, :].astype(BF16)
    m_ref[...] = jnp.full(m_ref.shape, NEG, F32)
    acc_ref[...] = jnp.zeros(acc_ref.shape, F32)

    def scores(k_blk, s_buf):
        for hk in range(GQA_KV_HEADS):
            kch = k_blk[:, hk * HEAD_DIM:(hk + 1) * HEAD_DIM]
            s_buf[hk] = jnp.dot(kch, qt_ref[hk], preferred_element_type=F32)

    def softmax_pv(v_t, s_ref):
        for hk in range(GQA_KV_HEADS):
            s = s_ref[hk]
            m_old = m_ref[hk]
            m_new = jnp.maximum(m_old, jnp.max(s, axis=0, keepdims=True))
            p = jnp.exp2(s - m_new).astype(BF16)
            pv = jnp.dot(_with_ones(v_t[hk * HEAD_DIM:(hk + 1) * HEAD_DIM]), p, preferred_element_type=F32)
            acc_ref[hk] = acc_ref[hk] * jnp.exp2(m_old - m_new) + pv
            m_ref[hk] = m_new

    def k_rows(c):
        if isinstance(c, int) and c == n_chunks:
            return kc_ref[...]
        return k_ref[pl.ds(pl.multiple_of(c * KEY_CHUNK, KEY_CHUNK), KEY_CHUNK), :]

    def v_cols(c):
        return vct_ref[0, 0] if isinstance(c, int) and c == n_chunks else vt_ref[0, c]

    n_buf = GQA_AHEAD + 1
    for c in range(GQA_AHEAD):
        scores(k_rows(c), s_ref.at[c % n_buf])

    def steps(c0, n):
        for j in range(n):
            c = c0 + j
            if not (isinstance(c, int) and c + GQA_AHEAD > n_chunks):
                scores(k_rows(c + GQA_AHEAD), s_ref.at[(j + GQA_AHEAD) % n_buf])
            softmax_pv(v_cols(c), s_ref.at[j % n_buf])

    trips = (n_chunks - GQA_AHEAD) // GQA_UNROLL

    def body(i, carry):
        steps(i * GQA_UNROLL, GQA_UNROLL)
        return carry

    lax.fori_loop(0, trips, body, 0)
    steps(trips * GQA_UNROLL, n_chunks + 1 - trips * GQA_UNROLL)

    heads_t = []
    for hk in range(GQA_KV_HEADS):
        acc = acc_ref[hk]
        out_t = acc[:HEAD_DIM] * (1.0 / acc[HEAD_DIM:HEAD_DIM + 1])
        heads_t += [out_t[:, gi * tq:(gi + 1) * tq] for gi in range(GQA_GROUP)]
    o_ref[...] = jnp.concatenate(heads_t, axis=0).T.astype(BF16)


def _gqa_attention(q, k, kc, vt, vct, *, n_batch, sq, ctx_len, tq):
    tpq = sq // tq
    m = GQA_GROUP * tq
    n_chunks = sq // KEY_CHUNK
    assert n_chunks >= GQA_AHEAD and GQA_UNROLL % (GQA_AHEAD + 1) == 0 and ctx_len == KEY_CHUNK
    return pl.pallas_call(
        functools.partial(_gqa_kernel, n_chunks=n_chunks),
        grid=(n_batch, tpq),
        in_specs=[
            pl.BlockSpec((tq, GQA_Q_W), lambda b, i: (b * tpq + i, 0)),
            pl.BlockSpec((sq, GQA_KV_W), lambda b, i: (b, 0)),
            pl.BlockSpec((ctx_len, GQA_KV_W), lambda b, i: (b, 0)),
            pl.BlockSpec((1,) + vt.shape[1:], lambda b, i: (b, 0, 0, 0)),
            pl.BlockSpec((1,) + vct.shape[1:], lambda b, i: (b, 0, 0, 0)),
        ],
        out_specs=pl.BlockSpec((tq, GQA_Q_W), lambda b, i: (b * tpq + i, 0)),
        out_shape=jax.ShapeDtypeStruct(q.shape, BF16),
        scratch_shapes=[
            pltpu.VMEM((GQA_KV_HEADS, HEAD_DIM, m), BF16),
            pltpu.VMEM((GQA_AHEAD + 1, GQA_KV_HEADS, KEY_CHUNK, m), F32),
            pltpu.VMEM((GQA_KV_HEADS, 1, m), F32),
            pltpu.VMEM((GQA_KV_HEADS, HEAD_DIM + ONES_ROWS, m), F32),
        ],
        compiler_params=_params(2),
        name="gqa_attention",
    )(q, k, kc, vt, vct)


def _ctx_attn_kernel(q_ref, k_ref, vt_ref, o_ref, ot_ref, *, n_kv, group):
    tq = q_ref.shape[0]
    q_t = q_ref[...].astype(F32).T.astype(BF16)
    for hk in range(n_kv):
        rows_k = slice(hk * HEAD_DIM, (hk + 1) * HEAD_DIM)
        qs = jnp.concatenate([q_t[(hk * group + gi) * HEAD_DIM:(hk * group + gi + 1) * HEAD_DIM]
                              for gi in range(group)], axis=1)
        s = jnp.dot(k_ref[:, rows_k], qs, preferred_element_type=F32)
        p = jnp.exp(s - jnp.max(s, axis=0, keepdims=True)).astype(BF16)
        acc = jnp.dot(_with_ones(vt_ref[0, 0, rows_k, :]), p, preferred_element_type=F32)
        out_t = acc[:HEAD_DIM] * (1.0 / acc[HEAD_DIM:HEAD_DIM + 1])
        for gi in range(group):
            hq = hk * group + gi
            ot_ref[hq * HEAD_DIM:(hq + 1) * HEAD_DIM, :] = out_t[:, gi * tq:(gi + 1) * tq]
    o_ref[...] = ot_ref[...].T.astype(BF16)


def _ctx_attention(q, k, vt, *, n_batch, ctx_len, n_kv, group):
    return pl.pallas_call(
        functools.partial(_ctx_attn_kernel, n_kv=n_kv, group=group),
        grid=(n_batch,),
        in_specs=[
            pl.BlockSpec((ctx_len, q.shape[1]), lambda b: (b, 0)),
            pl.BlockSpec((ctx_len, k.shape[1]), lambda b: (b, 0)),
            pl.BlockSpec((1,) + vt.shape[1:], lambda b: (b, 0, 0, 0)),
        ],
        out_specs=pl.BlockSpec((ctx_len, q.shape[1]), lambda b: (b, 0)),
        out_shape=jax.ShapeDtypeStruct(q.shape, BF16),
        scratch_shapes=[pltpu.VMEM((q.shape[1], ctx_len), F32)],
        compiler_params=_params(1),
        name=f"ctx_attention_g{group}",
    )(q, k, vt)


def _merge_kernel(x_ref, sh_ref, sc_ref, gate_ref, g_ref, wg_ref, uprev_ref, u_ref, unext_ref, cw_ref, cb_ref,
                  lg_ref, lb_ref, b_ref, c_ref, wa_ref, wb_ref, wc_ref, wo_ref, o_ref, win_ref, shift_ref, bra_ref,
                  *, tiles_per_seq):
    pos = pl.program_id(0) % tiles_per_seq
    x = x_ref[...]
    d = x.shape[1]
    h = _norm_modulate(x, g_ref[...], sh_ref[0], sc_ref[0]).astype(BF16)

    def gated(j, br, w_ref):
        gate = jax.nn.sigmoid(jnp.dot(h, wg_ref[:, j * d:(j + 1) * d], preferred_element_type=F32))
        return gate * jnp.dot(br, w_ref[...], preferred_element_type=F32)

    y = gated(1, b_ref[...], wb_ref) + gated(2, c_ref[...], wc_ref)
    gate_a = jax.nn.sigmoid(jnp.dot(h, wg_ref[:, 0:d], preferred_element_type=F32))
    _conv_ln_swish(uprev_ref, u_ref, unext_ref, cw_ref, cb_ref, lg_ref, lb_ref, win_ref, shift_ref, bra_ref,
                   pos == 0, pos == tiles_per_seq - 1)
    y = y + gate_a * jnp.dot(bra_ref[...], wa_ref[...], preferred_element_type=F32)
    o_ref[...] = x + gate_ref[0] * jnp.dot(y.astype(BF16), wo_ref[...], preferred_element_type=F32)


def _merge(x2, mod3, norm_g, w_gates, u, conv_w, conv_b, ln_g, ln_b, br_b, br_c, wa, wb, wc, wo, *, n_batch, seq,
           tm):
    nt, d = x2.shape
    tps = seq // tm
    tok = lambda i: (i, 0)
    mrow = lambda j: (lambda i: ((i // tps) * 6 + j, 0, 0))
    return pl.pallas_call(
        functools.partial(_merge_kernel, tiles_per_seq=tps),
        grid=(nt // tm,),
        in_specs=[
            pl.BlockSpec((tm, d), tok),
            pl.BlockSpec((1, 1, d), mrow(0)),
            pl.BlockSpec((1, 1, d), mrow(1)),
            pl.BlockSpec((1, 1, d), mrow(2)),
            _const_spec((1, d)),
            _const_spec(w_gates.shape),
            *_conv_halo_specs(nt, tm),
            _const_spec((CONV_K, CONV_W)),
            _const_spec((1, CONV_W)),
            _const_spec((1, CONV_W)),
            _const_spec((1, CONV_W)),
            pl.BlockSpec((tm, NA_W), tok),
            pl.BlockSpec((tm, GQA_Q_W), tok),
            _const_spec(wa.shape),
            _const_spec(wb.shape),
            _const_spec(wc.shape),
            _const_spec(wo.shape),
        ],
        out_specs=pl.BlockSpec((tm, d), tok),
        out_shape=jax.ShapeDtypeStruct((nt, d), F32),
        scratch_shapes=[pltpu.VMEM((tm + 2 * CONV_HALO, CONV_W), F32),
                        pltpu.VMEM((SUBLANES, tm + 2 * CONV_HALO, CONV_W), F32),
                        pltpu.VMEM((tm, CONV_W), BF16)],
        compiler_params=_params(1),
        name="merge",
    )(x2, mod3, mod3, mod3, norm_g, w_gates, u, u, u, conv_w, conv_b, ln_g, ln_b, br_b, br_c, wa, wb, wc, wo)


def _ffn_kernel(x_ref, sh_ref, sc_ref, gate_ref, g_ref, wi_ref, wo_ref, fg_ref, o_ref, *, n_split, final_norm):
    x = x_ref[...]
    hidden = wo_ref.shape[0]
    h = _norm_modulate(x, g_ref[...], sh_ref[0], sc_ref[0]).astype(BF16)
    step = hidden // n_split
    y = None
    for j in range(n_split):
        gt = jnp.dot(h, wi_ref[:, j * step:(j + 1) * step], preferred_element_type=F32)
        up = jnp.dot(h, wi_ref[:, hidden + j * step:hidden + (j + 1) * step], preferred_element_type=F32)
        act = (gt * jax.nn.sigmoid(gt) * up).astype(BF16)
        t = jnp.dot(act, wo_ref[j * step:(j + 1) * step, :], preferred_element_type=F32)
        y = t if y is None else y + t
    out = x + gate_ref[0] * y
    if final_norm:
        ms = jnp.mean(out * out, axis=-1, keepdims=True)
        out = out * lax.rsqrt(ms + EPS) * fg_ref[...]
    o_ref[...] = out


def _ffn(x2, mod3, norm_g, w_in, w_out, final_g, *, n_batch, seq, tm, final_norm):
    nt, d = x2.shape
    tps = seq // tm
    hidden = w_out.shape[0]
    assert hidden % KEY_CHUNK == 0
    tok = lambda i: (i, 0)
    mrow = lambda j: (lambda i: ((i // tps) * 6 + j, 0, 0))
    return pl.pallas_call(
        functools.partial(_ffn_kernel, n_split=hidden // KEY_CHUNK if tm >= TOKEN_TILE else 1,
                          final_norm=final_norm),
        grid=(nt // tm,),
        in_specs=[
            pl.BlockSpec((tm, d), tok),
            pl.BlockSpec((1, 1, d), mrow(3)),
            pl.BlockSpec((1, 1, d), mrow(4)),
            pl.BlockSpec((1, 1, d), mrow(5)),
            _const_spec((1, d)),
            _const_spec(w_in.shape),
            _const_spec(w_out.shape),
            _const_spec((1, d)),
        ],
        out_specs=pl.BlockSpec((tm, d), tok),
        out_shape=jax.ShapeDtypeStruct((nt, d), F32),
        compiler_params=_params(1),
        name="ffn_final" if final_norm else "ffn",
    )(x2, mod3, mod3, mod3, norm_g, w_in, w_out, final_g)


def _rope_tables(seq):
    t = jnp.arange(seq)
    pos_row = (t // GRID_W).astype(F32)
    pos_col = (t % GRID_W).astype(F32)
    half = HEAD_DIM // 2
    freqs = jnp.power(ROPE_THETA, -jnp.arange(0, half, 2, dtype=F32) / half)

    def part(pos):
        ang = pos[:, None] * freqs
        c, s = jnp.cos(ang), jnp.sin(ang)
        return jnp.concatenate([c, c], axis=-1), jnp.concatenate([-s, s], axis=-1)

    cr, sr = part(pos_row)
    cc, sc = part(pos_col)
    cos = jnp.concatenate([cr, cc], axis=-1)
    sin = jnp.concatenate([sr, sc], axis=-1)
    return jnp.tile(cos, (1, LANES // HEAD_DIM)), jnp.tile(sin, (1, LANES // HEAD_DIM))


def kernel(x, c, ctx, c_ctx, w_mod, b_mod, norm1_g, norm2_g, w_in, conv_w, conv_b, conv_ln_g, conv_ln_b,
           w_conv_out, na_rpb, w_na_out, q_norm_g, k_norm_g, w_gqa_out, w_out, w_ffn_in, w_ffn_out, final_g):
    n_batch, seq, d = x.shape
    ctx_len = ctx.shape[1]
    depth = w_mod.shape[0]
    rows = seq // GRID_W
    assert seq % (NA_QROWS * GRID_W) == 0 and rows >= NA_WROWS and ctx_len == KEY_CHUNK
    tm = TOKEN_TILE
    tmc = ctx_len

    n_rows = -(-(n_batch + 1) // 8) * 8
    crows = jnp.zeros((n_rows, d), F32).at[:n_batch].set(c).at[n_batch].set(c_ctx)
    mod = _modulation(crows, w_mod, b_mod)

    cos_t, sin_t = _rope_tables(seq)
    bd = jnp.asarray(np.kron(np.eye(KEY_CHUNK // HEAD_DIM), np.full((HEAD_DIM, HEAD_DIM), 1.0 / HEAD_DIM)), BF16)
    tile2 = lambda g: jnp.tile(g, LANES // HEAD_DIM).reshape(1, LANES)
    row = lambda g: g.reshape(1, -1)

    xl = x.reshape(n_batch * seq, d)
    xc = ctx.reshape(n_batch * ctx_len, d)
    for l in range(depth):
        last = l == depth - 1
        wl = w_in[l]
        w_proj = jnp.concatenate([wl[:, _IN_GLU:_IN_NAV], wl[:, _IN_GQ:_IN_GV]], axis=1).astype(BF16)
        w_vt = jnp.concatenate([wl[:, _IN_NAV:_IN_GQ], wl[:, _IN_GV:_IN_GATES]], axis=1).T.astype(BF16)
        w_gates = wl[:, _IN_GATES:].astype(BF16)
        mod_l = mod[l, :n_batch].reshape(n_batch * 6, 1, d)
        mod_c = jnp.broadcast_to(mod[l, n_batch], (n_batch, 6 * d)).reshape(n_batch * 6, 1, d)
        qg2, kg2 = tile2(q_norm_g[l]), tile2(k_norm_g[l])
        g1, g2 = row(norm1_g[l]), row(norm2_g[l])
        wa, wb, wc = w_conv_out[l].astype(BF16), w_na_out[l].astype(BF16), w_gqa_out[l].astype(BF16)
        wo = w_out[l].astype(BF16)
        wfi, wfo = w_ffn_in[l].astype(BF16), w_ffn_out[l].astype(BF16)
        conv_args = (conv_w[l], row(conv_b[l]), row(conv_ln_g[l]), row(conv_ln_b[l]))

        u, naq, nak, gq, gk, navt, gvt = _inproj(xl, mod_l, g1, w_proj, w_vt, qg2, kg2, bd, cos_t, sin_t,
                                                 n_batch=n_batch, seq=seq, tm=tm, rope=True)
        ctx_proj = _inproj(xc, mod_c, g1, w_proj, w_vt, qg2, kg2, bd, cos_t[:ctx_len], sin_t[:ctx_len],
                           n_batch=n_batch, seq=ctx_len, tm=tmc, rope=False, kv_only=last)
        if last:
            nakc, gkc, navct, gvct = ctx_proj
        else:
            uc, naqc, nakc, gqc, gkc, navct, gvct = ctx_proj

        bias = _na_bias_tables(na_rpb[l], rows)
        br_b = _na_attention(naq, nak, navt, nakc, navct, bias, n_batch=n_batch, seq=seq, ctx_len=ctx_len)
        br_c = _gqa_attention(gq, gk, gkc, gvt, gvct, n_batch=n_batch, sq=seq, ctx_len=ctx_len, tq=GQA_Q_TILE)
        xl = _merge(xl, mod_l, g1, w_gates, u, *conv_args, br_b, br_c, wa, wb, wc, wo,
                    n_batch=n_batch, seq=seq, tm=tm)
        xl = _ffn(xl, mod_l, g2, wfi, wfo, row(final_g), n_batch=n_batch, seq=seq, tm=tm, final_norm=last)

        if not last:
            cbr_b = _ctx_attention(naqc, nakc, navct, n_batch=n_batch, ctx_len=ctx_len, n_kv=NA_HEADS, group=1)
            cbr_c = _ctx_attention(gqc, gkc, gvct, n_batch=n_batch, ctx_len=ctx_len, n_kv=GQA_KV_HEADS,
                                   group=GQA_GROUP)
            xc = _merge(xc, mod_c, g1, w_gates, uc, *conv_args, cbr_b, cbr_c, wa, wb, wc, wo,
                        n_batch=n_batch, seq=ctx_len, tm=tmc)
            xc = _ffn(xc, mod_c, g2, wfi, wfo, row(final_g), n_batch=n_batch, seq=ctx_len, tm=tmc,
                      final_norm=False)
    return xl.reshape(n_batch, seq, d)
```
